```python
import math
import numpy as np
import jax
import jax.numpy as jnp
from jax import lax


D_MODEL = 1024
BATCH = 16
SEQ = 2048
DEPTH = 1

N_META = 16
RMS_EPS = 1e-6
N_HEADS = 16
HEAD_DIM = 64
ATTN_WIDTH = N_HEADS * HEAD_DIM
ROT_DIM = HEAD_DIM // 4
ROPE_THETA = 500000.0
IDX_HEADS = 8
IDX_DIM = 64
IDX_ROT_DIM = IDX_DIM // 4
TOPK_KEYS_MAX = 256
Q_BLOCK = 128
SSM_INNER = 2 * D_MODEL
SSM_HEAD_DIM = 64
SSM_HEADS = SSM_INNER // SSM_HEAD_DIM
SSM_GROUPS = 4
SSM_STATE = 128
CONV_WIDTH = 4
CONV_CH = SSM_INNER + 2 * SSM_GROUPS * SSM_STATE
CHUNK = 128
N_EXPERTS = 32
TOP_K = 4
EXPERT_FF = D_MODEL
SWIGLU_LIMIT = 7.0
SWIGLU_ALPHA = 1.702
N_BRANCH = 2
IN_SPLITS = (ATTN_WIDTH, ATTN_WIDTH, ATTN_WIDTH, IDX_HEADS * IDX_DIM, IDX_DIM, IDX_HEADS, SSM_INNER, SSM_INNER, SSM_GROUPS * SSM_STATE, SSM_GROUPS * SSM_STATE, SSM_HEADS, N_BRANCH * D_MODEL)
IN_WIDTH = sum(IN_SPLITS)

kernel_name = "hybrid_dsa_ssd_moe_block"


def rms_norm(x, g):
    xf = x.astype(jnp.float32)
    y = xf * lax.rsqrt(jnp.mean(xf * xf, axis=-1, keepdims=True) + RMS_EPS)
    return (y * g.astype(jnp.float32)).astype(x.dtype)


def rope_partial(x, pos, rot_dim):
    half = rot_dim // 2
    inv_freq = jnp.exp(-math.log(ROPE_THETA) * jnp.arange(half, dtype=jnp.float32) / half)
    ang = pos.astype(jnp.float32)[:, None] * inv_freq[None, :]
    cos = jnp.cos(ang)[None, :, None, :]
    sin = jnp.sin(ang)[None, :, None, :]
    xr = x[..., :rot_dim].astype(jnp.float32)
    x1, x2 = xr[..., :half], xr[..., half:]
    rot = jnp.concatenate([x1 * cos - x2 * sin, x2 * cos + x1 * sin], axis=-1)
    return jnp.concatenate([rot.astype(x.dtype), x[..., rot_dim:]], axis=-1)


def sparse_rows(q, qi, wi, qpos, k_real, v_real, ki_real, k_meta, v_meta):
    n_keys = k_real.shape[0]
    k_sel = min(TOPK_KEYS_MAX, n_keys // 4)
    kpos = N_META + jnp.arange(n_keys)
    idx_logits = jnp.einsum('nhd,sd->nhs', qi, ki_real).astype(jnp.float32) * (IDX_DIM ** -0.5)
    score = jnp.einsum('nh,nhs->ns', wi.astype(jnp.float32), jax.nn.relu(idx_logits))
    score = jnp.where(kpos[None, :] <= qpos[:, None], score, -jnp.inf)
    _, sel = lax.top_k(score, k_sel)
    sel_ok = kpos[sel] <= qpos[:, None]
    k_g = k_real[sel]
    v_g = v_real[sel]
    scale = HEAD_DIM ** -0.5
    s_meta = jnp.einsum('nhd,mhd->nhm', q, k_meta).astype(jnp.float32) * scale
    s_sel = jnp.einsum('nhd,nkhd->nhk', q, k_g).astype(jnp.float32) * scale
    meta_ok = jnp.arange(N_META)[None, :] <= qpos[:, None]
    s = jnp.concatenate([jnp.where(meta_ok[:, None, :], s_meta, -jnp.inf),
                         jnp.where(sel_ok[:, None, :], s_sel, -jnp.inf)], axis=-1)
    p = jax.nn.softmax(s, axis=-1).astype(v_real.dtype)
    return (jnp.einsum('nhm,mhd->nhd', p[..., :N_META], v_meta)
            + jnp.einsum('nhk,nkhd->nhd', p[..., N_META:], v_g))


def indexer_sparse_attention(q, k, v, qi, ki, wi, pos):
    def one_seq(args):
        q_s, k_s, v_s, qi_s, ki_s, wi_s = args
        t = q_s.shape[0]

        def rows(blk):
            return sparse_rows(blk[0], blk[1], blk[2], blk[3],
                               k_s[N_META:], v_s[N_META:], ki_s[N_META:], k_s[:N_META], v_s[:N_META])

        out_meta = rows((q_s[:N_META], qi_s[:N_META], wi_s[:N_META], pos[:N_META]))
        n_blk = (t - N_META) // Q_BLOCK
        blks = tuple(a[N_META:].reshape((n_blk, Q_BLOCK) + a.shape[1:]) for a in (q_s, qi_s, wi_s, pos))
        out_real = lax.map(rows, blks).reshape((t - N_META,) + q_s.shape[1:])
        return jnp.concatenate([out_meta, out_real], axis=0)

    return lax.map(one_seq, (q, k, v, qi, ki, wi))


def segsum_exp(a_cs):
    l = a_cs.shape[-1]
    diff = a_cs[..., :, None] - a_cs[..., None, :]
    mask = jnp.tril(jnp.ones((l, l), dtype=bool))
    return jnp.exp(jnp.where(mask, diff, -jnp.inf))


def ssd_chunked(x, dt, a, bm, cm):
    b, length, h, p = x.shape
    g, n = bm.shape[2], bm.shape[3]
    e = h // g
    c = length // CHUNK
    xdt = (x * dt[..., None].astype(x.dtype)).reshape(b, c, CHUNK, g, e, p)
    adt = (dt * a).reshape(b, c, CHUNK, g, e).transpose(0, 3, 4, 1, 2)
    bc = bm.reshape(b, c, CHUNK, g, n)
    cc = cm.reshape(b, c, CHUNK, g, n)
    a_cs = jnp.cumsum(adt, axis=-1)
    lmat = segsum_exp(a_cs)
    cb = jnp.einsum('bclgn,bcsgn->bgcls', cc, bc)
    y_diag = jnp.einsum('bgcls,bgecls,bcsgep->bclgep', cb, lmat, xdt)
    decay_states = jnp.exp(a_cs[..., -1:] - a_cs)
    states = jnp.einsum('bcsgn,bgecs,bcsgep->bcgepn', bc, decay_states, xdt)
    chunk_decay = jnp.exp(a_cs[..., -1])

    def step(carry, inp):
        st, dec = inp
        return carry * dec[..., None, None] + st, carry

    init = jnp.zeros((b, g, e, p, n), dtype=states.dtype)
    _, prev = lax.scan(step, init, (jnp.moveaxis(states, 1, 0), jnp.moveaxis(chunk_decay, -1, 0)))
    y_off = jnp.einsum('bclgn,cbgepn,bgecl->bclgep', cc, prev, jnp.exp(a_cs))
    return (y_diag + y_off).reshape(b, length, h, p).astype(x.dtype)


def ssd_branch(z, xs, bm, cm, dt_raw, conv_w, conv_b, dt_bias, a_log, d_skip, ssm_norm_g):
    b, t, _ = xs.shape
    xbc = jnp.concatenate([xs, bm, cm], axis=-1)
    xbc = lax.conv_general_dilated(xbc, conv_w.astype(xbc.dtype)[:, None, :], (1,), [(CONV_WIDTH - 1, 0)],
                                   dimension_numbers=('NWC', 'WIO', 'NWC'), feature_group_count=CONV_CH)
    xbc = jax.nn.silu(xbc + conv_b)
    xs, bm, cm = jnp.split(xbc, [SSM_INNER, SSM_INNER + SSM_GROUPS * SSM_STATE], axis=-1)
    dt = jax.nn.softplus((dt_raw + dt_bias).astype(jnp.float32))
    a = -jnp.exp(a_log.astype(jnp.float32))
    pad = (-N_META) % CHUNK

    def padt(arr):
        return jnp.pad(arr, ((0, 0), (pad, 0)) + ((0, 0),) * (arr.ndim - 2))

    xh = xs.reshape(b, t, SSM_HEADS, SSM_HEAD_DIM)
    y = ssd_chunked(padt(xh), padt(dt), a,
                    padt(bm.reshape(b, t, SSM_GROUPS, SSM_STATE)),
                    padt(cm.reshape(b, t, SSM_GROUPS, SSM_STATE)))[:, pad:]
    y = y + xh * d_skip[:, None].astype(xh.dtype)
    y = y.reshape(b, t, SSM_INNER) * jax.nn.silu(z)
    yg = rms_norm(y.reshape(b, t, SSM_GROUPS, SSM_INNER // SSM_GROUPS), ssm_norm_g.reshape(SSM_GROUPS, -1))
    return yg.reshape(b, t, SSM_INNER)


def hybrid_mixer(hn, pos, w_in, conv_w, conv_b, dt_bias, a_log, d_skip, ssm_norm_g, w_o_attn, w_o_ssm, w_out):
    b, t, _ = hn.shape
    proj = hn @ w_in
    split_points = np.cumsum(IN_SPLITS)[:-1].tolist()
    q, k, v, qi, ki, wi, z, xs, bm, cm, dt_raw, gate_raw = jnp.split(proj, split_points, axis=-1)
    q = rope_partial(q.reshape(b, t, N_HEADS, HEAD_DIM), pos, ROT_DIM)
    k = rope_partial(k.reshape(b, t, N_HEADS, HEAD_DIM), pos, ROT_DIM)
    v = v.reshape(b, t, N_HEADS, HEAD_DIM)
    qi = rope_partial(qi.reshape(b, t, IDX_HEADS, IDX_DIM), pos, IDX_ROT_DIM)
    ki = rope_partial(ki.reshape(b, t, 1, IDX_DIM), pos, IDX_ROT_DIM)[:, :, 0]
    wi = wi * (IDX_HEADS ** -0.5)
    attn = indexer_sparse_attention(q, k, v, qi, ki, wi, pos).reshape(b, t, ATTN_WIDTH)
    ssm = ssd_branch(z, xs, bm, cm, dt_raw, conv_w, conv_b, dt_bias, a_log, d_skip, ssm_norm_g)
    gates = jax.nn.sigmoid(gate_raw.astype(jnp.float32)).astype(hn.dtype).reshape(b, t, N_BRANCH, D_MODEL)
    u = gates[:, :, 0] * (attn @ w_o_attn) + gates[:, :, 1] * (ssm @ w_o_ssm)
    return u @ w_out


def moe_ffn(hn, w_router, b_router, w_gate_up, b_gate_up, w_down, b_down):
    b, t, d = hn.shape
    tok = hn.reshape(b * t, d)
    logits = (tok @ w_router + b_router).astype(jnp.float32)
    top_v, top_i = lax.top_k(logits, TOP_K)
    top_w = jax.nn.softmax(top_v, axis=-1)
    dense_w = jnp.sum(jax.nn.one_hot(top_i, N_EXPERTS, dtype=jnp.float32) * top_w[..., None], axis=1).astype(tok.dtype)
    out = jnp.zeros_like(tok)
    for e in range(N_EXPERTS):
        gu = tok @ w_gate_up[e] + b_gate_up[e]
        gate, up = gu[:, :EXPERT_FF], gu[:, EXPERT_FF:]
        gate = jnp.minimum(gate, SWIGLU_LIMIT)
        up = jnp.clip(up, -SWIGLU_LIMIT, SWIGLU_LIMIT)
        act = (up + 1.0) * (gate * jax.nn.sigmoid(SWIGLU_ALPHA * gate))
        out = out + dense_w[:, e:e + 1] * (act @ w_down[e] + b_down[e])
    return out.reshape(b, t, d)


def setup_inputs(seed: int = 0) -> dict:
    key = jax.random.key(seed)
    ks = jax.random.split(key, 21)

    def nrm(k, shape, scale):
        return jax.random.normal(k, shape, jnp.float32) * scale

    dt0 = jnp.exp(jax.random.uniform(ks[6], (DEPTH, SSM_HEADS), jnp.float32) * (math.log(0.1) - math.log(0.001)) + math.log(0.001))
    return {
        'x': nrm(ks[0], (BATCH, SEQ, D_MODEL), 1.0),
        'meta_tokens': nrm(ks[1], (N_META, D_MODEL), 1.0),
        'norm1_g': 1.0 + nrm(ks[2], (DEPTH, D_MODEL), 0.05),
        'w_in': nrm(ks[3], (DEPTH, D_MODEL, IN_WIDTH), D_MODEL ** -0.5),
        'conv_w': nrm(ks[4], (DEPTH, CONV_WIDTH, CONV_CH), CONV_WIDTH ** -0.5),
        'conv_b': nrm(ks[5], (DEPTH, CONV_CH), 0.02),
        'dt_bias': dt0 + jnp.log(-jnp.expm1(-dt0)),
        'a_log': jnp.log(jax.random.uniform(ks[7], (DEPTH, SSM_HEADS), jnp.float32, 1.0, 16.0)),
        'd_skip': 1.0 + nrm(ks[8], (DEPTH, SSM_HEADS), 0.1),
        'ssm_norm_g': 1.0 + nrm(ks[9], (DEPTH, SSM_INNER), 0.05),
        'w_o_attn': nrm(ks[10], (DEPTH, ATTN_WIDTH, D_MODEL), ATTN_WIDTH ** -0.5),
        'w_o_ssm': nrm(ks[11], (DEPTH, SSM_INNER, D_MODEL), SSM_INNER ** -0.5),
        'w_out': nrm(ks[12], (DEPTH, D_MODEL, D_MODEL), D_MODEL ** -0.5),
        'norm2_g': 1.0 + nrm(ks[13], (DEPTH, D_MODEL), 0.05),
        'w_router': nrm(ks[14], (DEPTH, D_MODEL, N_EXPERTS), D_MODEL ** -0.5),
        'b_router': nrm(ks[15], (DEPTH, N_EXPERTS), 0.01),
        'w_gate_up': nrm(ks[16], (DEPTH, N_EXPERTS, D_MODEL, 2 * EXPERT_FF), D_MODEL ** -0.5),
        'b_gate_up': nrm(ks[17], (DEPTH, N_EXPERTS, 2 * EXPERT_FF), 0.02),
        'w_down': nrm(ks[18], (DEPTH, N_EXPERTS, EXPERT_FF, D_MODEL), EXPERT_FF ** -0.5),
        'b_down': nrm(ks[19], (DEPTH, N_EXPERTS, D_MODEL), 0.02),
        'final_g': 1.0 + nrm(ks[20], (D_MODEL,), 0.05),
    }


def reference(x, meta_tokens, norm1_g, w_in, conv_w, conv_b, dt_bias, a_log, d_skip, ssm_norm_g,
              w_o_attn, w_o_ssm, w_out, norm2_g, w_router, b_router, w_gate_up, b_gate_up,
              w_down, b_down, final_g):
    b = x.shape[0]
    meta = jnp.broadcast_to(meta_tokens.astype(x.dtype)[None], (b, N_META, D_MODEL))
    h = jnp.concatenate([meta, x], axis=1)
    pos = jnp.arange(h.shape[1], dtype=jnp.int32)
    for layer in range(DEPTH):
        hn = rms_norm(h, norm1_g[layer])
        h = h + hybrid_mixer(hn, pos, w_in[layer], conv_w[layer], conv_b[layer], dt_bias[layer],
                             a_log[layer], d_skip[layer], ssm_norm_g[layer], w_o_attn[layer],
                             w_o_ssm[layer], w_out[layer])
        h = h + moe_ffn(rms_norm(h, norm2_g[layer]), w_router[layer], b_router[layer],
                        w_gate_up[layer], b_gate_up[layer], w_down[layer], b_down[layer])
    h = rms_norm(h, final_g)
    return h[:, N_META:]
```

```python
import functools
import math

import numpy as np
import jax
import jax.numpy as jnp
from jax import lax
from jax.experimental import pallas as pl
from jax.experimental.pallas import tpu as pltpu

F32 = jnp.float32
BF16 = jnp.bfloat16
I32 = jnp.int32

D_MODEL = 1024
N_META = 16
RMS_EPS = 1e-6
N_HEADS = 16
HEAD_DIM = 64
ATTN_WIDTH = N_HEADS * HEAD_DIM
ROT_DIM = HEAD_DIM // 4
ROPE_THETA = 500000.0
IDX_HEADS = 8
IDX_DIM = 64
TOPK_KEYS_MAX = 256
SSM_INNER = 2 * D_MODEL
SSM_HEAD_DIM = 64
SSM_HEADS = SSM_INNER // SSM_HEAD_DIM
SSM_GROUPS = 4
SSM_STATE = 128
CONV_WIDTH = 4
CHUNK = 128
N_EXPERTS = 32
TOP_K = 4
EXPERT_FF = D_MODEL
SWIGLU_LIMIT = 7.0
SWIGLU_ALPHA = 1.702
IN_SPLITS = (ATTN_WIDTH, ATTN_WIDTH, ATTN_WIDTH, IDX_HEADS * IDX_DIM, IDX_DIM, IDX_HEADS,
             SSM_INNER, SSM_INNER, SSM_GROUPS * SSM_STATE, SSM_GROUPS * SSM_STATE, SSM_HEADS,
             2 * D_MODEL)

LANES = 128
GROUP_CH = SSM_INNER // SSM_GROUPS
GROUP_HEADS = SSM_HEADS // SSM_GROUPS
META_PAD = CHUNK - N_META
TQ = 256
MOE_TM = 512
INT_MIN = -2147483648
VMEM_LIMIT = 56 * 1024 * 1024

NAT_K, NAT_Z, NAT_XS, NAT_B, NAT_C, NAT_G = 0, 1024, 3072, 5120, 5632, 6144
NAT_W = 8192
SN_KI, SN_DT, SN_W = 0, 128, 256
ST_QI, ST_WI, ST_W = 0, 512, 640


def _cparams(sem):
    return pltpu.CompilerParams(dimension_semantics=sem, vmem_limit_bytes=VMEM_LIMIT)


def _rms_rows(x, g):
    ms = jnp.mean(x * x, axis=-1, keepdims=True)
    return x * lax.rsqrt(ms + RMS_EPS) * g


def _proj_nat_kernel(x_ref, g_ref, w_ref, cos_ref, sa_ref, sb_ref, o_ref, hn_ref, *,
                     rope_tiles, rope_slabs):
    j = pl.program_id(1)

    @pl.when(j == 0)
    def _():
        hn_ref[...] = _rms_rows(x_ref[...], g_ref[...]).astype(hn_ref.dtype)

    acc = jnp.dot(hn_ref[...], w_ref[...], preferred_element_type=F32)
    n_slabs = acc.shape[1] // LANES

    def plain():
        o_ref[...] = acc.astype(o_ref.dtype)

    def roped():
        cos, sa, sb = cos_ref[...], sa_ref[...], sb_ref[...]
        for s in range(n_slabs):
            a = acc[:, s * LANES:(s + 1) * LANES]
            if s < rope_slabs:
                a = a * cos + pltpu.roll(a, 8, 1) * sa + pltpu.roll(a, LANES - 8, 1) * sb
            o_ref[:, s * LANES:(s + 1) * LANES] = a.astype(o_ref.dtype)

    if rope_tiles == 0:
        plain()
    else:
        pl.when(j < rope_tiles)(roped)
        pl.when(j >= rope_tiles)(plain)


def _proj_nat(x, g, w, tabs, *, tm, tn, out_dtype, rope_tiles, rope_slabs, seq_tiles, name):
    n, d = x.shape
    width = w.shape[1]
    cos, sa, sb = tabs
    tab_spec = pl.BlockSpec((tm, LANES), lambda i, j: (i % seq_tiles, 0))
    return pl.pallas_call(
        functools.partial(_proj_nat_kernel, rope_tiles=rope_tiles, rope_slabs=rope_slabs),
        out_shape=jax.ShapeDtypeStruct((n, width), out_dtype),
        grid=(n // tm, width // tn),
        in_specs=[
            pl.BlockSpec((tm, d), lambda i, j: (i, 0)),
            pl.BlockSpec((1, d), lambda i, j: (0, 0)),
            pl.BlockSpec((d, tn), lambda i, j: (0, j)),
            tab_spec, tab_spec, tab_spec,
        ],
        out_specs=pl.BlockSpec((tm, tn), lambda i, j: (i, j)),
        scratch_shapes=[pltpu.VMEM((tm, d), BF16)],
        compiler_params=_cparams(("parallel", "arbitrary")),
        name=name,
    )(x, g, w, cos, sa, sb)


def _proj_tr_kernel(x_ref, g_ref, wt_ref, cos_ref, sa_ref, sb_ref, o_ref, *, rope_slabs):
    hn = _rms_rows(x_ref[...], g_ref[...]).astype(BF16)
    acc = lax.dot_general(wt_ref[...], hn, (((1,), (1,)), ((), ())),
                          preferred_element_type=F32)
    cos, sa, sb = cos_ref[...], sa_ref[...], sb_ref[...]
    for s in range(acc.shape[0] // LANES):
        a = acc[s * LANES:(s + 1) * LANES, :]
        if s < rope_slabs:
            a = a * cos + pltpu.roll(a, 8, 0) * sa + pltpu.roll(a, LANES - 8, 0) * sb
        o_ref[s * LANES:(s + 1) * LANES, :] = a.astype(o_ref.dtype)


def _proj_tr(x, g, wt, tabs_t, *, tm, nb, out_dtype, rope_slabs, name):
    n, d = x.shape
    width = wt.shape[0]
    seq_tiles = n // nb // tm
    cos, sa, sb = tabs_t
    tab_spec = pl.BlockSpec((LANES, tm), lambda i: (0, i % seq_tiles))
    return pl.pallas_call(
        functools.partial(_proj_tr_kernel, rope_slabs=rope_slabs),
        out_shape=jax.ShapeDtypeStruct((nb, seq_tiles, width, tm), out_dtype),
        grid=(n // tm,),
        in_specs=[
            pl.BlockSpec((tm, d), lambda i: (i, 0)),
            pl.BlockSpec((1, d), lambda i: (0, 0)),
            pl.BlockSpec((width, d), lambda i: (0, 0)),
            tab_spec, tab_spec, tab_spec,
        ],
        out_specs=pl.BlockSpec((None, None, width, tm),
                               lambda i: (i // seq_tiles, i % seq_tiles, 0, 0)),
        compiler_params=_cparams(("parallel",)),
        name=name,
    )(x, g, wt, cos, sa, sb)


def _rope_tables(pos):
    half = ROT_DIM // 2
    inv_freq = jnp.exp(-math.log(ROPE_THETA) * jnp.arange(half, dtype=F32) / half)
    ang = pos.astype(F32)[:, None] * inv_freq[None, :]
    cos8, sin8 = jnp.cos(ang), jnp.sin(ang)
    t = pos.shape[0]
    cc = np.arange(LANES) % HEAD_DIM
    in_rot = jnp.asarray(cc < ROT_DIM)
    lo = jnp.asarray(cc < half)
    hi = jnp.asarray((cc >= half) & (cc < ROT_DIM))
    idx = jnp.asarray(cc % half)
    cos_l = jnp.take(cos8, idx, axis=1)
    sin_l = jnp.take(sin8, idx, axis=1)
    cos = jnp.where(in_rot[None, :], cos_l, 1.0)
    sa = jnp.where(hi[None, :], sin_l, 0.0)
    sb = jnp.where(lo[None, :], -sin_l, 0.0)
    del t
    return cos, sa, sb


def _attn_kernel(qt_ref, vt_ref, k_ref, qit_ref, wit_ref, ki_ref, kmeta_ref, vtmeta_ref,
                 o_ref, keys_ref, bias_ref, qm_ref, m_ref, l_ref, acc_ref, *, ksel, nbits):
    j = pl.program_id(1)
    tq = TQ
    nch = j + 1
    int_min = jnp.int32(INT_MIN)
    neg_inf = jnp.float32(-jnp.inf)

    row128 = lax.broadcasted_iota(I32, (LANES, tq), 0)
    for h in range(N_HEADS):
        slab = qt_ref[(h // 2) * LANES:(h // 2 + 1) * LANES, :]
        keep = (row128 < HEAD_DIM) if h % 2 == 0 else (row128 >= HEAD_DIM)
        scaled = slab.astype(F32) * HEAD_DIM ** -0.5
        qm_ref[h] = jnp.where(keep, scaled, 0.0).astype(qm_ref.dtype)

    w_all = wit_ref[0:IDX_HEADS, :] * (IDX_HEADS ** -0.5 * IDX_DIM ** -0.5)
    qi_bf = [qit_ref[h * IDX_DIM:(h + 1) * IDX_DIM, :].astype(BF16) for h in range(IDX_HEADS)]
    qidx = j * tq + lax.broadcasted_iota(I32, (tq, tq), 1)
    krow = lax.broadcasted_iota(I32, (tq, tq), 0)

    def score_body(c, carry):
        start = pl.multiple_of(c * tq, tq)
        kic = ki_ref[pl.ds(start, tq), :][:, :IDX_DIM].astype(BF16)
        sc = jnp.zeros((tq, tq), F32)
        for h in range(IDX_HEADS):
            lg = jnp.dot(kic, qi_bf[h], preferred_element_type=F32)
            sc = sc + jnp.maximum(lg, 0.0) * w_all[h:h + 1, :]
        bits = pltpu.bitcast(sc, I32)
        skey = jnp.where(bits < 0, bits ^ jnp.int32(0x7FFFFFFF), bits)
        skey = jnp.where(c * tq + krow <= qidx, skey, int_min)
        keys_ref[c] = skey
        return carry

    lax.fori_loop(0, nch, score_body, 0)

    def count(pred):
        def body(c, cnt):
            hit = pred(keys_ref[c], c).astype(I32)
            return cnt + hit.reshape(tq // 8, 8, tq).sum(axis=0)
        cnt8 = lax.fori_loop(0, nch, body, jnp.zeros((8, tq), I32))
        return jnp.sum(cnt8, axis=0, keepdims=True)

    def bit_body(it, t_b):
        cand_b = t_b | lax.shift_left(jnp.int32(1), 31 - it)
        cand = cand_b ^ int_min
        cnt = count(lambda kc, c: kc >= cand)
        return jnp.where(cnt >= ksel, cand_b, t_b)

    t_b = lax.fori_loop(0, 32, bit_body, jnp.zeros((1, tq), I32))
    thr = t_b ^ int_min
    n_gt = count(lambda kc, c: kc > thr)
    n_eq = count(lambda kc, c: kc == thr)
    need = ksel - n_gt
    fix = (n_eq > need) & (thr != int_min)
    thr_eff = jnp.maximum(thr, int_min + 1)

    def bias_body(c, carry):
        bias_ref[c] = jnp.where(keys_ref[c] >= thr_eff, 0.0, neg_inf)
        return carry

    lax.fori_loop(0, nch, bias_body, 0)

    @pl.when(jnp.max(fix.astype(I32)) > 0)
    def _():
        def idx_body(it, v):
            cand = v | lax.shift_left(jnp.int32(1), nbits - 1 - it)
            cnt = count(lambda kc, c: jnp.where(c * tq + krow < cand, kc, int_min) == thr)
            return jnp.where(cnt <= need - 1, cand, v)

        last = lax.fori_loop(0, nbits, idx_body, jnp.zeros((1, tq), I32))

        def fix_body(c, carry):
            kc = keys_ref[c]
            tie_lim = jnp.where(fix, last, jnp.int32(2 ** 30))
            tie_ok = jnp.where(c * tq + krow <= tie_lim, thr_eff, thr_eff + 1)
            bias_ref[c] = jnp.where(kc >= tie_ok, 0.0, neg_inf)
            return carry

        lax.fori_loop(0, nch, fix_body, 0)

    def head_step(h, kslab, vt_chunk, bias_c, first):
        s = jnp.dot(kslab, qm_ref[h], preferred_element_type=F32) + bias_c
        smax = jnp.max(s, axis=0, keepdims=True)
        if first:
            m_new = smax
        else:
            m_old = m_ref[h:h + 1, :]
            m_new = jnp.maximum(m_old, smax)
            alpha = jnp.exp(m_old - m_new)
        p = jnp.exp(s - m_new)
        psum = jnp.sum(p, axis=0, keepdims=True)
        pv = jnp.dot(vt_chunk, p.astype(BF16), preferred_element_type=F32)
        rows = slice(h * HEAD_DIM, (h + 1) * HEAD_DIM)
        if first:
            l_ref[h:h + 1, :] = psum
            acc_ref[rows, :] = pv
        else:
            l_ref[h:h + 1, :] = l_ref[h:h + 1, :] * alpha + psum
            acc_ref[rows, :] = acc_ref[rows, :] * alpha + pv
        m_ref[h:h + 1, :] = m_new

    meta_row = lax.broadcasted_iota(I32, (CHUNK, tq), 0)
    bias_meta = jnp.where(meta_row >= META_PAD, 0.0, neg_inf)
    for h in range(N_HEADS):
        head_step(h, kmeta_ref[:, (h // 2) * LANES:(h // 2 + 1) * LANES],
                  vtmeta_ref[h * HEAD_DIM:(h + 1) * HEAD_DIM, :], bias_meta, True)

    def flash_body(c, carry):
        start = pl.multiple_of(c * tq, tq)
        bias_c = bias_ref[c]
        for h in range(N_HEADS):
            head_step(h, k_ref[pl.ds(start, tq), (h // 2) * LANES:(h // 2 + 1) * LANES],
                      vt_ref[c, h * HEAD_DIM:(h + 1) * HEAD_DIM, :], bias_c, False)
        return carry

    lax.fori_loop(0, nch, flash_body, 0)

    for h in range(N_HEADS):
        rows = slice(h * HEAD_DIM, (h + 1) * HEAD_DIM)
        acc_ref[rows, :] = acc_ref[rows, :] / l_ref[h:h + 1, :]
    o_ref[...] = acc_ref[...].T.astype(o_ref.dtype)


def _attention(qvt, nat, sm_t, sm_n, nat_meta, qvt_meta, *, nb, seq):
    nq = seq // TQ
    ksel = min(TOPK_KEYS_MAX, seq // 4)
    nbits = max(1, int(math.ceil(math.log2(seq))))
    kernel = functools.partial(_attn_kernel, ksel=ksel, nbits=nbits)
    return pl.pallas_call(
        kernel,
        out_shape=jax.ShapeDtypeStruct((nb * seq, ATTN_WIDTH), BF16),
        grid=(nb, nq),
        in_specs=[
            pl.BlockSpec((None, None, ATTN_WIDTH, TQ), lambda b, j: (b, j, 0, 0)),
            pl.BlockSpec((None, nq, ATTN_WIDTH, TQ), lambda b, j: (b, 0, 1, 0)),
            pl.BlockSpec((seq, ATTN_WIDTH), lambda b, j: (b, NAT_K // ATTN_WIDTH)),
            pl.BlockSpec((None, None, IDX_HEADS * IDX_DIM, TQ), lambda b, j: (b, j, 0, 0)),
            pl.BlockSpec((None, None, LANES, TQ), lambda b, j: (b, j, ST_WI // LANES, 0)),
            pl.BlockSpec((seq, LANES), lambda b, j: (b, SN_KI // LANES)),
            pl.BlockSpec((CHUNK, ATTN_WIDTH), lambda b, j: (0, NAT_K // ATTN_WIDTH)),
            pl.BlockSpec((None, None, ATTN_WIDTH, CHUNK), lambda b, j: (0, 0, 1, 0)),
        ],
        out_specs=pl.BlockSpec((TQ, ATTN_WIDTH), lambda b, j: (b * nq + j, 0)),
        scratch_shapes=[
            pltpu.VMEM((nq, TQ, TQ), I32),
            pltpu.VMEM((nq, TQ, TQ), F32),
            pltpu.VMEM((N_HEADS, LANES, TQ), BF16),
            pltpu.VMEM((N_HEADS, TQ), F32),
            pltpu.VMEM((N_HEADS, TQ), F32),
            pltpu.VMEM((ATTN_WIDTH, TQ), F32),
        ],
        compiler_params=_cparams(("parallel", "arbitrary")),
        name="attn",
    )(qvt, qvt, nat, sm_t, sm_t, sm_n, nat_meta, qvt_meta)


def _softplus(x):
    return jnp.maximum(x, 0.0) + jnp.log1p(jnp.exp(-jnp.abs(x)))


def _silu(x):
    return x * jax.nn.sigmoid(x)


def _conv_silu(cur, tail, w_ref, b_ref):
    stack = jnp.concatenate([tail.astype(F32), cur.astype(F32)], axis=0)
    n_tail = tail.shape[0]
    out = cur.astype(F32) * w_ref[CONV_WIDTH - 1:CONV_WIDTH, :]
    for k in range(1, CONV_WIDTH):
        shifted = pltpu.roll(stack, k, 0)[n_tail:, :]
        out = out + shifted * w_ref[CONV_WIDTH - 1 - k:CONV_WIDTH - k, :]
    return _silu(out + b_ref[...])


def _pad_transpose(rows):
    padded = jnp.concatenate([rows, jnp.zeros((LANES - rows.shape[0], LANES), rows.dtype)], axis=0)
    return padded.T


def _ssd_kernel(xs_ref, z_ref, b_ref, c_ref, dtt_ref,
                xs_m_ref, b_m_ref, c_m_ref, dtt_m_ref,
                cwx_ref, cbx_ref, cwb_ref, cbb_ref, cwc_ref, cbc_ref,
                dtb_ref, alog_ref, dsk_ref, ng_ref, exp_ref, tri_ref,
                o_ref, state_ref, *, n_chunks):
    a_neg = -jnp.exp(alog_ref[...])
    dt_bias = dtb_ref[...]
    expand = exp_ref[...]
    tri_u = tri_ref[...]
    li = lax.broadcasted_iota(I32, (CHUNK, CHUNK), 0)
    si = lax.broadcasted_iota(I32, (CHUNK, CHUNK), 1)
    tril = si <= li
    lane = lax.broadcasted_iota(I32, (CHUNK, LANES), 1)
    low_half = lane < SSM_HEAD_DIM
    neg_inf = jnp.float32(-jnp.inf)

    def chunk(x_raw, x_tail, b_raw, b_tail, c_raw, c_tail, dt_raw, z_raw, out_rows, is_meta):
        x = _conv_silu(x_raw, x_tail, cwx_ref, cbx_ref)
        bm = _conv_silu(b_raw, b_tail, cwb_ref, cbb_ref)
        cm = _conv_silu(c_raw, c_tail, cwc_ref, cbc_ref)
        dt_r = _softplus(dt_raw + dt_bias)
        if is_meta:
            tok = lax.broadcasted_iota(I32, dt_r.shape, 1)
            dt_r = jnp.where(tok >= META_PAD, dt_r, 0.0)
        adt_r = dt_r * a_neg
        cs_r = jnp.dot(adt_r, tri_u, precision=lax.Precision.HIGHEST,
                       preferred_element_type=F32)
        dt_c = _pad_transpose(dt_r)
        cs_c = _pad_transpose(cs_r)
        cs_last = cs_c[CHUNK - 1:CHUNK, :]
        dec_c = jnp.exp(cs_last - cs_c)
        ecs_c = jnp.exp(cs_c)
        dtx = jnp.dot(dt_c.astype(BF16), expand, preferred_element_type=F32)
        decx = jnp.dot((dt_c * dec_c).astype(BF16), expand, preferred_element_type=F32)
        ecsx = jnp.dot(ecs_c.astype(BF16), expand, preferred_element_type=F32)
        bm_bf, cm_bf = bm.astype(BF16), cm.astype(BF16)
        state = state_ref[...]
        new_part = lax.dot_general(bm_bf, (x * decx).astype(BF16), (((0,), (0,)), ((), ())),
                                   preferred_element_type=F32)
        if not is_meta:
            xdt = (x * dtx).astype(BF16)
            cb = lax.dot_general(cm_bf, bm_bf, (((1,), (1,)), ((), ())),
                                 preferred_element_type=F32)
            y_off = jnp.dot(cm_bf, state.astype(BF16), preferred_element_type=F32) * ecsx
            slabs = []
            for pp in range(GROUP_HEADS // 2):
                xslab = xdt[:, pp * LANES:(pp + 1) * LANES]
                halves = []
                for e in (2 * pp, 2 * pp + 1):
                    diff = cs_c[:, e:e + 1] - cs_r[e:e + 1, :]
                    lmat = jnp.exp(jnp.where(tril, diff, neg_inf))
                    g = (cb * lmat).astype(BF16)
                    halves.append(jnp.dot(g, xslab, preferred_element_type=F32))
                slabs.append(jnp.where(low_half, halves[0], halves[1]))
            y = jnp.concatenate(slabs, axis=1) + y_off + x * dsk_ref[...]
            y = y * _silu(z_raw.astype(F32))
            ms = jnp.mean(y * y, axis=-1, keepdims=True)
            o_ref[out_rows, :] = (y * lax.rsqrt(ms + RMS_EPS) * ng_ref[...]).astype(o_ref.dtype)
        state_ref[...] = state * ecsx[CHUNK - 1:CHUNK, :] + new_part

    tail_rows = 16
    state_ref[...] = jnp.zeros_like(state_ref)
    chunk(xs_m_ref[...], jnp.zeros((tail_rows, GROUP_CH), BF16),
          b_m_ref[...], jnp.zeros((tail_rows, SSM_STATE), BF16),
          c_m_ref[...], jnp.zeros((tail_rows, SSM_STATE), BF16),
          dtt_m_ref[...], None, None, True)
    chunk(xs_ref[0:CHUNK, :], xs_m_ref[CHUNK - tail_rows:CHUNK, :],
          b_ref[0:CHUNK, :], b_m_ref[CHUNK - tail_rows:CHUNK, :],
          c_ref[0:CHUNK, :], c_m_ref[CHUNK - tail_rows:CHUNK, :],
          dtt_ref[0], z_ref[0:CHUNK, :], slice(0, CHUNK), False)

    def body(ci, carry):
        start = pl.multiple_of(ci * CHUNK, CHUNK)
        tstart = pl.multiple_of(ci * CHUNK - tail_rows, tail_rows)
        rows = pl.ds(start, CHUNK)
        trows = pl.ds(tstart, tail_rows)
        chunk(xs_ref[rows, :], xs_ref[trows, :], b_ref[rows, :], b_ref[trows, :],
              c_ref[rows, :], c_ref[trows, :], dtt_ref[ci], z_ref[rows, :], rows, False)
        return carry

    lax.fori_loop(1, n_chunks, body, 0)


def _ssd(nat, nat_meta, dtt, dtt_meta, conv_w, conv_b, dt_bias, a_log, d_skip, ssm_norm_g,
         *, nb, seq):
    gch, st = GROUP_CH, SSM_STATE
    expand = np.zeros((LANES, gch), np.float32)
    for e in range(GROUP_HEADS):
        expand[e, e * SSM_HEAD_DIM:(e + 1) * SSM_HEAD_DIM] = 1.0
    tri = np.triu(np.ones((CHUNK, CHUNK), np.float32))
    cw = conv_w
    cb = conv_b.reshape(1, -1)
    dsk = jnp.repeat(d_skip, SSM_HEAD_DIM).reshape(1, SSM_INNER)
    ng = ssm_norm_g.reshape(1, SSM_INNER)
    dtb = dt_bias.reshape(SSM_GROUPS, GROUP_HEADS, 1)
    alog = a_log.reshape(SSM_GROUPS, GROUP_HEADS, 1)
    xs_blk, b_blk, c_blk = NAT_XS // gch, NAT_B // st, NAT_C // st
    z_blk = NAT_Z // gch
    cx_blk, cbb_blk, cc_blk = 0, SSM_INNER // st, (SSM_INNER + SSM_GROUPS * st) // st
    return pl.pallas_call(
        functools.partial(_ssd_kernel, n_chunks=seq // CHUNK),
        out_shape=jax.ShapeDtypeStruct((nb * seq, SSM_INNER), BF16),
        grid=(nb, SSM_GROUPS),
        in_specs=[
            pl.BlockSpec((seq, gch), lambda b, g: (b, xs_blk + g)),
            pl.BlockSpec((seq, gch), lambda b, g: (b, z_blk + g)),
            pl.BlockSpec((seq, st), lambda b, g: (b, b_blk + g)),
            pl.BlockSpec((seq, st), lambda b, g: (b, c_blk + g)),
            pl.BlockSpec((None, None, seq // CHUNK, GROUP_HEADS, CHUNK),
                         lambda b, g: (g, b, 0, 0, 0)),
            pl.BlockSpec((CHUNK, gch), lambda b, g: (0, xs_blk + g)),
            pl.BlockSpec((CHUNK, st), lambda b, g: (0, b_blk + g)),
            pl.BlockSpec((CHUNK, st), lambda b, g: (0, c_blk + g)),
            pl.BlockSpec((None, GROUP_HEADS, CHUNK), lambda b, g: (g, 0, 0)),
            pl.BlockSpec((CONV_WIDTH, gch), lambda b, g: (0, cx_blk + g)),
            pl.BlockSpec((1, gch), lambda b, g: (0, cx_blk + g)),
            pl.BlockSpec((CONV_WIDTH, st), lambda b, g: (0, cbb_blk + g)),
            pl.BlockSpec((1, st), lambda b, g: (0, cbb_blk + g)),
            pl.BlockSpec((CONV_WIDTH, st), lambda b, g: (0, cc_blk + g)),
            pl.BlockSpec((1, st), lambda b, g: (0, cc_blk + g)),
            pl.BlockSpec((None, GROUP_HEADS, 1), lambda b, g: (g, 0, 0)),
            pl.BlockSpec((None, GROUP_HEADS, 1), lambda b, g: (g, 0, 0)),
            pl.BlockSpec((1, gch), lambda b, g: (0, g)),
            pl.BlockSpec((1, gch), lambda b, g: (0, g)),
            pl.BlockSpec((LANES, gch), lambda b, g: (0, 0)),
            pl.BlockSpec((CHUNK, CHUNK), lambda b, g: (0, 0)),
        ],
        out_specs=pl.BlockSpec((seq, gch), lambda b, g: (b, g)),
        scratch_shapes=[pltpu.VMEM((SSM_STATE, gch), F32)],
        compiler_params=_cparams(("parallel", "arbitrary")),
        name="ssd",
    )(nat, nat, nat, nat, dtt, nat_meta, nat_meta, nat_meta, dtt_meta,
      cw, cb, cw, cb, cw, cb, dtb, alog, dsk, ng,
      jnp.asarray(expand, BF16), jnp.asarray(tri))


def _merge_kernel(attn_ref, ssm_ref, g0_ref, g1_ref, x_ref, woa_ref, wos_ref, wout_ref,
                  n2g_ref, wr_ref, br_ref, h1_ref, hn2_ref, ti_ref, tw_ref):
    a = jnp.dot(attn_ref[...], woa_ref[...], preferred_element_type=F32)
    s = jnp.dot(ssm_ref[...], wos_ref[...], preferred_element_type=F32)
    g0 = jax.nn.sigmoid(g0_ref[...].astype(F32))
    g1 = jax.nn.sigmoid(g1_ref[...].astype(F32))
    u = g0 * a + g1 * s
    h1 = x_ref[...] + jnp.dot(u.astype(BF16), wout_ref[...], preferred_element_type=F32)
    h1_ref[...] = h1
    hn2 = _rms_rows(h1, n2g_ref[...])
    hn2_ref[...] = hn2.astype(hn2_ref.dtype)
    logits = jnp.dot(hn2, wr_ref[...], precision=lax.Precision.HIGHEST,
                     preferred_element_type=F32) + br_ref[...]
    lane = lax.broadcasted_iota(I32, logits.shape, 1)
    work = logits
    ids, vals = [], []
    for _ in range(TOP_K):
        mx = jnp.max(work, axis=1, keepdims=True)
        idx = jnp.min(jnp.where(work == mx, lane, LANES), axis=1, keepdims=True)
        ids.append(idx)
        vals.append(mx)
        work = jnp.where(lane == idx, -jnp.inf, work)
    es = [jnp.exp(v - vals[0]) for v in vals]
    den = es[0] + es[1] + es[2] + es[3]
    ti = jnp.zeros(logits.shape, I32)
    tw = jnp.zeros(logits.shape, F32)
    for k in range(TOP_K):
        ti = jnp.where(lane == k, ids[k], ti)
        tw = jnp.where(lane == k, es[k] / den, tw)
    ti_ref[...] = ti
    tw_ref[...] = tw


def _merge(attn, ssm, nat, x, woa, wos, wout, n2g, wr, br, *, tm):
    n = x.shape[0]
    gblk = NAT_G // D_MODEL
    const = lambda shape: pl.BlockSpec(shape, lambda i: (0, 0))
    return pl.pallas_call(
        _merge_kernel,
        out_shape=(jax.ShapeDtypeStruct((n, D_MODEL), F32),
                   jax.ShapeDtypeStruct((n, D_MODEL), BF16),
                   jax.ShapeDtypeStruct((n, LANES), I32),
                   jax.ShapeDtypeStruct((n, LANES), F32)),
        grid=(n // tm,),
        in_specs=[
            pl.BlockSpec((tm, ATTN_WIDTH), lambda i: (i, 0)),
            pl.BlockSpec((tm, SSM_INNER), lambda i: (i, 0)),
            pl.BlockSpec((tm, D_MODEL), lambda i: (i, gblk)),
            pl.BlockSpec((tm, D_MODEL), lambda i: (i, gblk + 1)),
            pl.BlockSpec((tm, D_MODEL), lambda i: (i, 0)),
            const((ATTN_WIDTH, D_MODEL)), const((SSM_INNER, D_MODEL)), const((D_MODEL, D_MODEL)),
            const((1, D_MODEL)), const((D_MODEL, LANES)), const((1, LANES)),
        ],
        out_specs=(pl.BlockSpec((tm, D_MODEL), lambda i: (i, 0)),
                   pl.BlockSpec((tm, D_MODEL), lambda i: (i, 0)),
                   pl.BlockSpec((tm, LANES), lambda i: (i, 0)),
                   pl.BlockSpec((tm, LANES), lambda i: (i, 0))),
        compiler_params=_cparams(("parallel",)),
        name="merge",
    )(attn, ssm, nat, nat, x, woa, wos, wout, n2g, wr, br)


def _moe_kernel(te_ref, nv_ref, x_ref, wgu_ref, bgu_ref, wd_ref, bd_ref, o_ref):
    t = pl.program_id(0)

    @pl.when(t < nv_ref[0])
    def _():
        gu = jnp.dot(x_ref[...], wgu_ref[...], preferred_element_type=F32) + bgu_ref[...]
        gate = jnp.minimum(gu[:, :EXPERT_FF], SWIGLU_LIMIT)
        up = jnp.clip(gu[:, EXPERT_FF:], -SWIGLU_LIMIT, SWIGLU_LIMIT)
        act = (up + 1.0) * (gate * jax.nn.sigmoid(SWIGLU_ALPHA * gate))
        y = jnp.dot(act.astype(BF16), wd_ref[...], preferred_element_type=F32) + bd_ref[...]
        o_ref[...] = y.astype(o_ref.dtype)

    @pl.when(t >= nv_ref[0])
    def _():
        o_ref[...] = jnp.zeros_like(o_ref)


def _moe(tile_expert, n_valid, x_sorted, wgu, bgu, wd, bd):
    rows = x_sorted.shape[0]
    n_tiles = rows // MOE_TM
    grid_spec = pltpu.PrefetchScalarGridSpec(
        num_scalar_prefetch=2,
        grid=(n_tiles,),
        in_specs=[
            pl.BlockSpec((MOE_TM, D_MODEL), lambda t, te, nv: (t, 0)),
            pl.BlockSpec((None, D_MODEL, 2 * EXPERT_FF), lambda t, te, nv: (te[t], 0, 0)),
            pl.BlockSpec((None, 1, 2 * EXPERT_FF), lambda t, te, nv: (te[t], 0, 0)),
            pl.BlockSpec((None, EXPERT_FF, D_MODEL), lambda t, te, nv: (te[t], 0, 0)),
            pl.BlockSpec((None, 1, D_MODEL), lambda t, te, nv: (te[t], 0, 0)),
        ],
        out_specs=pl.BlockSpec((MOE_TM, D_MODEL), lambda t, te, nv: (t, 0)),
    )
    return pl.pallas_call(
        _moe_kernel,
        out_shape=jax.ShapeDtypeStruct((rows, D_MODEL), BF16),
        grid_spec=grid_spec,
        compiler_params=_cparams(("arbitrary",)),
        name="moe",
    )(tile_expert, n_valid, x_sorted, wgu, bgu, wd, bd)


def _final_kernel(h1_ref, y0_ref, y1_ref, y2_ref, y3_ref, tw_ref, fg_ref, o_ref):
    tw = tw_ref[...]
    h = h1_ref[...]
    for k, y_ref in enumerate((y0_ref, y1_ref, y2_ref, y3_ref)):
        h = h + tw[:, k:k + 1] * y_ref[...].astype(F32)
    o_ref[...] = _rms_rows(h, fg_ref[...])


def _final(h1, ys, tw, fg, *, tm):
    n = h1.shape[0]
    row = lambda w: pl.BlockSpec((tm, w), lambda i: (i, 0))
    return pl.pallas_call(
        _final_kernel,
        out_shape=jax.ShapeDtypeStruct((n, D_MODEL), F32),
        grid=(n // tm,),
        in_specs=[row(D_MODEL)] + [row(D_MODEL)] * TOP_K + [row(LANES),
                  pl.BlockSpec((1, D_MODEL), lambda i: (0, 0))],
        out_specs=row(D_MODEL),
        compiler_params=_cparams(("parallel",)),
        name="final",
    )(h1, *ys, tw, fg)


def _split_w_in(w_in):
    offs = np.cumsum((0,) + IN_SPLITS)
    names = ("q", "k", "v", "qi", "ki", "wi", "z", "xs", "b", "c", "dt", "gate")
    return {nm: w_in[:, offs[i]:offs[i + 1]] for i, nm in enumerate(names)}


def _expert_dispatch(top_i, n_rows_pad):
    n = top_i.shape[0]
    pairs = n * TOP_K
    e_flat = top_i.reshape(-1)
    order = jnp.argsort(e_flat, stable=True).astype(I32)
    e_sorted = e_flat[order]
    counts = jnp.zeros((N_EXPERTS,), I32).at[e_flat].add(1)
    tiles_per_e = (counts + MOE_TM - 1) // MOE_TM
    tile_end = jnp.cumsum(tiles_per_e)
    row_start = (tile_end - tiles_per_e) * MOE_TM
    grp_start = jnp.cumsum(counts) - counts
    dest = row_start[e_sorted] + (jnp.arange(pairs, dtype=I32) - grp_start[e_sorted])
    src_token = jnp.zeros((n_rows_pad,), I32).at[dest].set(order // TOP_K)
    pos = jnp.zeros((pairs,), I32).at[order].set(dest).reshape(n, TOP_K)
    n_tiles = n_rows_pad // MOE_TM
    n_valid = tile_end[-1].astype(I32)
    tile_ids = jnp.minimum(jnp.arange(n_tiles, dtype=I32), n_valid - 1)
    tile_expert = jnp.searchsorted(tile_end, tile_ids, side="right").astype(I32)
    tile_expert = jnp.minimum(tile_expert, N_EXPERTS - 1)
    return src_token, pos, tile_expert, n_valid.reshape(1)


def kernel(x, meta_tokens, norm1_g, w_in, conv_w, conv_b, dt_bias, a_log, d_skip, ssm_norm_g,
           w_o_attn, w_o_ssm, w_out, norm2_g, w_router, b_router, w_gate_up, b_gate_up,
           w_down, b_down, final_g):
    nb, seq, d = x.shape
    n = nb * seq
    assert d == D_MODEL and seq % TQ == 0 and w_in.shape[0] == 1
    xf = x.reshape(n, d)
    meta_pad = jnp.concatenate([jnp.zeros((META_PAD, d), x.dtype), meta_tokens.astype(x.dtype)], 0)
    g1 = norm1_g[0].reshape(1, d)

    w = _split_w_in(w_in[0])
    w_nat = jnp.concatenate([w["k"], w["z"], w["xs"], w["b"], w["c"], w["gate"]], 1).astype(BF16)
    w_qvt = jnp.concatenate([w["q"], w["v"]], 1).T.astype(BF16)
    zpad = lambda c: jnp.zeros((d, c), F32)
    w_sn = jnp.concatenate([w["ki"], zpad(SN_DT - IDX_DIM), w["dt"],
                            zpad(SN_W - SN_DT - SSM_HEADS)], 1).astype(BF16)
    w_st = jnp.concatenate([w["qi"], w["wi"], zpad(ST_W - ST_WI - IDX_HEADS)], 1).T.astype(BF16)

    tabs = _rope_tables(N_META + jnp.arange(seq, dtype=I32))
    tabs_t = tuple(t.T for t in tabs)
    pos_meta = jnp.maximum(jnp.arange(CHUNK, dtype=I32) - META_PAD, 0)
    tabs_m = _rope_tables(pos_meta)
    tabs_mt = tuple(t.T for t in tabs_m)

    tm_nat = 1024 if seq % 1024 == 0 else TQ
    nat = _proj_nat(xf, g1, w_nat, tabs, tm=tm_nat, tn=1024, out_dtype=BF16, rope_tiles=1,
                    rope_slabs=ATTN_WIDTH // LANES, seq_tiles=seq // tm_nat, name="proj_nat")
    sm_n = _proj_nat(xf, g1, w_sn, tabs, tm=tm_nat, tn=SN_W, out_dtype=F32, rope_tiles=1,
                     rope_slabs=1, seq_tiles=seq // tm_nat, name="proj_small_nat")
    qvt = _proj_tr(xf, g1, w_qvt, tabs_t, tm=TQ, nb=nb, out_dtype=BF16,
                   rope_slabs=ATTN_WIDTH // LANES, name="proj_qv_t")
    sm_t = _proj_tr(xf, g1, w_st, tabs_t, tm=TQ, nb=nb, out_dtype=F32,
                    rope_slabs=IDX_HEADS * IDX_DIM // LANES, name="proj_small_t")
    nat_m = _proj_nat(meta_pad, g1, w_nat, tabs_m, tm=CHUNK, tn=1024, out_dtype=BF16, rope_tiles=1,
                      rope_slabs=ATTN_WIDTH // LANES, seq_tiles=1, name="proj_nat_meta")
    sm_n_m = _proj_nat(meta_pad, g1, w_sn, tabs_m, tm=CHUNK, tn=SN_W, out_dtype=F32, rope_tiles=1,
                       rope_slabs=1, seq_tiles=1, name="proj_small_nat_meta")
    qvt_m = _proj_tr(meta_pad, g1, w_qvt, tabs_mt, tm=CHUNK, nb=1, out_dtype=BF16,
                     rope_slabs=ATTN_WIDTH // LANES, name="proj_qv_t_meta")

    attn = _attention(qvt, nat, sm_t, sm_n, nat_m, qvt_m, nb=nb, seq=seq)

    dt_raw = sm_n[:, SN_DT:SN_DT + SSM_HEADS].reshape(nb, seq // CHUNK, CHUNK, SSM_GROUPS,
                                                      GROUP_HEADS)
    dtt = jnp.transpose(dt_raw, (3, 0, 1, 4, 2))
    dt_raw_m = sm_n_m[:, SN_DT:SN_DT + SSM_HEADS].reshape(CHUNK, SSM_GROUPS, GROUP_HEADS)
    dtt_m = jnp.transpose(dt_raw_m, (1, 2, 0))
    ssm = _ssd(nat, nat_m, dtt, dtt_m, conv_w[0], conv_b[0], dt_bias[0], a_log[0], d_skip[0],
               ssm_norm_g[0], nb=nb, seq=seq)

    wr = jnp.concatenate([w_router[0], jnp.zeros((d, LANES - N_EXPERTS), F32)], 1)
    br = jnp.concatenate([b_router[0], jnp.full((LANES - N_EXPERTS,), -1e30, F32)]).reshape(1, LANES)
    h1, hn2, ti, tw = _merge(attn, ssm, nat, xf, w_o_attn[0].astype(BF16), w_o_ssm[0].astype(BF16),
                             w_out[0].astype(BF16), norm2_g[0].reshape(1, d), wr, br,
                             tm=512 if n % 512 == 0 else TQ)

    n_tiles = n * TOP_K // MOE_TM + N_EXPERTS
    src_token, pos, tile_expert, n_valid = _expert_dispatch(ti[:, :TOP_K], n_tiles * MOE_TM)
    x_sorted = jnp.take(hn2, src_token, axis=0)
    y_sorted = _moe(tile_expert, n_valid, x_sorted, w_gate_up[0].astype(BF16),
                    b_gate_up[0].reshape(N_EXPERTS, 1, -1), w_down[0].astype(BF16),
                    b_down[0].reshape(N_EXPERTS, 1, -1))
    ys = [jnp.take(y_sorted, pos[:, k], axis=0) for k in range(TOP_K)]

    out = _final(h1, ys, tw, final_g.reshape(1, d), tm=512 if n % 512 == 0 else TQ)
    return out.reshape(nb, seq, d)
```

```python
import functools
import math

import numpy as np
import jax
import jax.numpy as jnp
from jax import lax
from jax.experimental import pallas as pl
from jax.experimental.pallas import tpu as pltpu

F32 = jnp.float32
BF16 = jnp.bfloat16
I32 = jnp.int32

D_MODEL = 1024
N_META = 16
RMS_EPS = 1e-6
N_HEADS = 16
HEAD_DIM = 64
ATTN_WIDTH = N_HEADS * HEAD_DIM
ROT_DIM = HEAD_DIM // 4
ROPE_THETA = 500000.0
IDX_HEADS = 8
IDX_DIM = 64
TOPK_KEYS_MAX = 256
SSM_INNER = 2 * D_MODEL
SSM_HEAD_DIM = 64
SSM_HEADS = SSM_INNER // SSM_HEAD_DIM
SSM_GROUPS = 4
SSM_STATE = 128
CONV_WIDTH = 4
CHUNK = 128
N_EXPERTS = 32
TOP_K = 4
EXPERT_FF = D_MODEL
SWIGLU_LIMIT = 7.0
SWIGLU_ALPHA = 1.702
IN_SPLITS = (ATTN_WIDTH, ATTN_WIDTH, ATTN_WIDTH, IDX_HEADS * IDX_DIM, IDX_DIM, IDX_HEADS,
             SSM_INNER, SSM_INNER, SSM_GROUPS * SSM_STATE, SSM_GROUPS * SSM_STATE, SSM_HEADS,
             2 * D_MODEL)

LANES = 128
GROUP_CH = SSM_INNER // SSM_GROUPS
GROUP_HEADS = SSM_HEADS // SSM_GROUPS
META_PAD = CHUNK - N_META
TQ = 256
MOE_TM = 512
INT_MIN = -2147483648
VMEM_LIMIT = 56 * 1024 * 1024

NAT_K, NAT_Z, NAT_XS, NAT_B, NAT_C, NAT_G = 0, 1024, 3072, 5120, 5632, 6144
NAT_W = 8192
SN_KI, SN_DT, SN_W = 0, 128, 256
ST_QI, ST_WI, ST_W = 0, 512, 640


def _cparams(sem):
    return pltpu.CompilerParams(dimension_semantics=sem, vmem_limit_bytes=VMEM_LIMIT)


def _rms_rows(x, g):
    ms = jnp.mean(x * x, axis=-1, keepdims=True)
    return x * lax.rsqrt(ms + RMS_EPS) * g


def _proj_nat_kernel(x_ref, g_ref, w_ref, cos_ref, sa_ref, sb_ref, o_ref, hn_ref, *,
                     rope_tiles, rope_slabs):
    j = pl.program_id(1)

    @pl.when(j == 0)
    def _():
        hn_ref[...] = _rms_rows(x_ref[...], g_ref[...]).astype(hn_ref.dtype)

    acc = jnp.dot(hn_ref[...], w_ref[...], preferred_element_type=F32)
    n_slabs = acc.shape[1] // LANES

    def plain():
        o_ref[...] = acc.astype(o_ref.dtype)

    def roped():
        cos, sa, sb = cos_ref[...], sa_ref[...], sb_ref[...]
        for s in range(n_slabs):
            a = acc[:, s * LANES:(s + 1) * LANES]
            if s < rope_slabs:
                a = a * cos + pltpu.roll(a, 8, 1) * sa + pltpu.roll(a, LANES - 8, 1) * sb
            o_ref[:, s * LANES:(s + 1) * LANES] = a.astype(o_ref.dtype)

    if rope_tiles == 0:
        plain()
    else:
        pl.when(j < rope_tiles)(roped)
        pl.when(j >= rope_tiles)(plain)


def _proj_nat(x, g, w, tabs, *, tm, tn, out_dtype, rope_tiles, rope_slabs, seq_tiles, name):
    n, d = x.shape
    width = w.shape[1]
    cos, sa, sb = tabs
    tab_spec = pl.BlockSpec((tm, LANES), lambda i, j: (i % seq_tiles, 0))
    return pl.pallas_call(
        functools.partial(_proj_nat_kernel, rope_tiles=rope_tiles, rope_slabs=rope_slabs),
        out_shape=jax.ShapeDtypeStruct((n, width), out_dtype),
        grid=(n // tm, width // tn),
        in_specs=[
            pl.BlockSpec((tm, d), lambda i, j: (i, 0)),
            pl.BlockSpec((1, d), lambda i, j: (0, 0)),
            pl.BlockSpec((d, tn), lambda i, j: (0, j)),
            tab_spec, tab_spec, tab_spec,
        ],
        out_specs=pl.BlockSpec((tm, tn), lambda i, j: (i, j)),
        scratch_shapes=[pltpu.VMEM((tm, d), BF16)],
        compiler_params=_cparams(("parallel", "arbitrary")),
        name=name,
    )(x, g, w, cos, sa, sb)


def _proj_tr_kernel(x_ref, g_ref, wt_ref, cos_ref, sa_ref, sb_ref, o_ref, *, rope_slabs):
    hn = _rms_rows(x_ref[...], g_ref[...]).astype(BF16)
    acc = lax.dot_general(wt_ref[...], hn, (((1,), (1,)), ((), ())),
                          preferred_element_type=F32)
    cos, sa, sb = cos_ref[...], sa_ref[...], sb_ref[...]
    for s in range(acc.shape[0] // LANES):
        a = acc[s * LANES:(s + 1) * LANES, :]
        if s < rope_slabs:
            a = a * cos + pltpu.roll(a, 8, 0) * sa + pltpu.roll(a, LANES - 8, 0) * sb
        o_ref[s * LANES:(s + 1) * LANES, :] = a.astype(o_ref.dtype)


def _proj_tr(x, g, wt, tabs_t, *, tm, nb, out_dtype, rope_slabs, name):
    n, d = x.shape
    width = wt.shape[0]
    seq_tiles = n // nb // tm
    cos, sa, sb = tabs_t
    tab_spec = pl.BlockSpec((LANES, tm), lambda i: (0, i % seq_tiles))
    return pl.pallas_call(
        functools.partial(_proj_tr_kernel, rope_slabs=rope_slabs),
        out_shape=jax.ShapeDtypeStruct((nb, seq_tiles, width, tm), out_dtype),
        grid=(n // tm,),
        in_specs=[
            pl.BlockSpec((tm, d), lambda i: (i, 0)),
            pl.BlockSpec((1, d), lambda i: (0, 0)),
            pl.BlockSpec((width, d), lambda i: (0, 0)),
            tab_spec, tab_spec, tab_spec,
        ],
        out_specs=pl.BlockSpec((None, None, width, tm),
                               lambda i: (i // seq_tiles, i % seq_tiles, 0, 0)),
        compiler_params=_cparams(("parallel",)),
        name=name,
    )(x, g, wt, cos, sa, sb)


def _rope_tables(pos):
    half = ROT_DIM // 2
    inv_freq = jnp.exp(-math.log(ROPE_THETA) * jnp.arange(half, dtype=F32) / half)
    ang = pos.astype(F32)[:, None] * inv_freq[None, :]
    cos8, sin8 = jnp.cos(ang), jnp.sin(ang)
    t = pos.shape[0]
    cc = np.arange(LANES) % HEAD_DIM
    in_rot = jnp.asarray(cc < ROT_DIM)
    lo = jnp.asarray(cc < half)
    hi = jnp.asarray((cc >= half) & (cc < ROT_DIM))
    idx = jnp.asarray(cc % half)
    cos_l = jnp.take(cos8, idx, axis=1)
    sin_l = jnp.take(sin8, idx, axis=1)
    cos = jnp.where(in_rot[None, :], cos_l, 1.0)
    sa = jnp.where(hi[None, :], sin_l, 0.0)
    sb = jnp.where(lo[None, :], -sin_l, 0.0)
    del t
    return cos, sa, sb


def _attn_kernel(qt_ref, vt_ref, k_ref, qit_ref, wit_ref, ki_ref, kmeta_ref, vtmeta_ref,
                 o_ref, keys_ref, bias_ref, qm_ref, m_ref, l_ref, acc_ref, s_ref, *, ksel, nbits):
    j = pl.program_id(1)
    tq = TQ
    nch = j + 1
    int_min = jnp.int32(INT_MIN)
    neg_inf = jnp.float32(-jnp.inf)

    row128 = lax.broadcasted_iota(I32, (LANES, tq), 0)
    for h in range(N_HEADS):
        slab = qt_ref[(h // 2) * LANES:(h // 2 + 1) * LANES, :]
        keep = (row128 < HEAD_DIM) if h % 2 == 0 else (row128 >= HEAD_DIM)
        scaled = slab.astype(F32) * (HEAD_DIM ** -0.5 * math.log2(math.e))
        qm_ref[h] = jnp.where(keep, scaled, 0.0).astype(qm_ref.dtype)

    w_all = wit_ref[0:IDX_HEADS, :] * (IDX_HEADS ** -0.5 * IDX_DIM ** -0.5)
    qi_bf = [qit_ref[h * IDX_DIM:(h + 1) * IDX_DIM, :].astype(BF16) for h in range(IDX_HEADS)]
    qidx = j * tq + lax.broadcasted_iota(I32, (tq, tq), 1)
    krow = lax.broadcasted_iota(I32, (tq, tq), 0)

    def score_body(c, carry):
        start = pl.multiple_of(c * tq, tq)
        kic = ki_ref[pl.ds(start, tq), :][:, :IDX_DIM].astype(BF16)
        sc = jnp.zeros((tq, tq), F32)
        for h in range(IDX_HEADS):
            lg = jnp.dot(kic, qi_bf[h], preferred_element_type=F32)
            sc = sc + jnp.maximum(lg, 0.0) * w_all[h:h + 1, :]
        bits = pltpu.bitcast(sc, I32)
        skey = jnp.where(bits < 0, bits ^ jnp.int32(0x7FFFFFFF), bits)
        skey = jnp.where(c * tq + krow <= qidx, skey, int_min)
        keys_ref[c] = skey
        return carry

    lax.fori_loop(0, nch, score_body, 0)

    def count(pred):
        def body(c, cnt):
            hit = pred(keys_ref[c], c).astype(I32)
            return cnt + hit.reshape(tq // 8, 8, tq).sum(axis=0)
        cnt8 = lax.fori_loop(0, nch, body, jnp.zeros((8, tq), I32))
        return jnp.sum(cnt8, axis=0, keepdims=True)

    def bit_body(it, t_b):
        cand_b = t_b | lax.shift_left(jnp.int32(1), 31 - it)
        cand = cand_b ^ int_min
        cnt = count(lambda kc, c: kc >= cand)
        return jnp.where(cnt >= ksel, cand_b, t_b)

    t_b = lax.fori_loop(0, 32, bit_body, jnp.zeros((1, tq), I32))
    thr = t_b ^ int_min
    n_gt = count(lambda kc, c: kc > thr)
    n_eq = count(lambda kc, c: kc == thr)
    need = ksel - n_gt
    fix = (n_eq > need) & (thr != int_min)
    thr_eff = jnp.maximum(thr, int_min + 1)

    def bias_body(c, carry):
        bias_ref[c] = jnp.where(keys_ref[c] >= thr_eff, 0.0, neg_inf)
        return carry

    lax.fori_loop(0, nch, bias_body, 0)

    @pl.when(jnp.max(fix.astype(I32)) > 0)
    def _():
        def idx_body(it, v):
            cand = v | lax.shift_left(jnp.int32(1), nbits - 1 - it)
            cnt = count(lambda kc, c: jnp.where(c * tq + krow < cand, kc, int_min) == thr)
            return jnp.where(cnt <= need - 1, cand, v)

        last = lax.fori_loop(0, nbits, idx_body, jnp.zeros((1, tq), I32))

        def fix_body(c, carry):
            kc = keys_ref[c]
            tie_lim = jnp.where(fix, last, jnp.int32(2 ** 30))
            tie_ok = jnp.where(c * tq + krow <= tie_lim, thr_eff, thr_eff + 1)
            bias_ref[c] = jnp.where(kc >= tie_ok, 0.0, neg_inf)
            return carry

        lax.fori_loop(0, nch, fix_body, 0)

    def chunk_step(kslab_of, vt_of, bias_c, n_keys, first):
        alphas = []
        for h in range(N_HEADS):
            s = jnp.dot(kslab_of(h), qm_ref[h], preferred_element_type=F32) + bias_c
            s_ref[h, 0:n_keys, :] = s
            smax = jnp.max(s, axis=0, keepdims=True)
            if first:
                m_ref[h:h + 1, :] = smax
            else:
                m_old = m_ref[h:h + 1, :]
                m_new = jnp.maximum(m_old, smax)
                alphas.append(jnp.exp2(m_old - m_new))
                m_ref[h:h + 1, :] = m_new
        ones = jnp.ones((16, n_keys), BF16)
        for h in range(N_HEADS):
            p = jnp.exp2(s_ref[h, 0:n_keys, :] - m_ref[h:h + 1, :]).astype(BF16)
            res = jnp.dot(jnp.concatenate([vt_of(h), ones], axis=0), p,
                          preferred_element_type=F32)
            pv, psum = res[:HEAD_DIM, :], res[HEAD_DIM:HEAD_DIM + 1, :]
            rows = slice(h * HEAD_DIM, (h + 1) * HEAD_DIM)
            if first:
                l_ref[h:h + 1, :] = psum
                acc_ref[rows, :] = pv
            else:
                l_ref[h:h + 1, :] = l_ref[h:h + 1, :] * alphas[h] + psum
                acc_ref[rows, :] = acc_ref[rows, :] * alphas[h] + pv

    meta_row = lax.broadcasted_iota(I32, (CHUNK, tq), 0)
    bias_meta = jnp.where(meta_row >= META_PAD, 0.0, neg_inf)
    chunk_step(lambda h: kmeta_ref[:, (h // 2) * LANES:(h // 2 + 1) * LANES],
               lambda h: vtmeta_ref[h * HEAD_DIM:(h + 1) * HEAD_DIM, :], bias_meta, CHUNK, True)

    def flash_body(c, carry):
        start = pl.multiple_of(c * tq, tq)
        chunk_step(lambda h: k_ref[pl.ds(start, tq), (h // 2) * LANES:(h // 2 + 1) * LANES],
                   lambda h: vt_ref[c, h * HEAD_DIM:(h + 1) * HEAD_DIM, :], bias_ref[c], tq, False)
        return carry

    lax.fori_loop(0, nch, flash_body, 0)

    for h in range(N_HEADS):
        rows = slice(h * HEAD_DIM, (h + 1) * HEAD_DIM)
        acc_ref[rows, :] = acc_ref[rows, :] * (1.0 / l_ref[h:h + 1, :])
    o_ref[...] = acc_ref[...].T.astype(o_ref.dtype)


def _attention(qvt, nat, sm_t, sm_n, nat_meta, qvt_meta, *, nb, seq):
    nq = seq // TQ
    ksel = min(TOPK_KEYS_MAX, seq // 4)
    nbits = max(1, int(math.ceil(math.log2(seq))))
    kernel = functools.partial(_attn_kernel, ksel=ksel, nbits=nbits)
    return pl.pallas_call(
        kernel,
        out_shape=jax.ShapeDtypeStruct((nb * seq, ATTN_WIDTH), BF16),
        grid=(nb, nq),
        in_specs=[
            pl.BlockSpec((None, None, ATTN_WIDTH, TQ), lambda b, j: (b, j, 0, 0)),
            pl.BlockSpec((None, nq, ATTN_WIDTH, TQ), lambda b, j: (b, 0, 1, 0)),
            pl.BlockSpec((seq, ATTN_WIDTH), lambda b, j: (b, NAT_K // ATTN_WIDTH)),
            pl.BlockSpec((None, None, IDX_HEADS * IDX_DIM, TQ), lambda b, j: (b, j, 0, 0)),
            pl.BlockSpec((None, None, LANES, TQ), lambda b, j: (b, j, ST_WI // LANES, 0)),
            pl.BlockSpec((seq, LANES), lambda b, j: (b, SN_KI // LANES)),
            pl.BlockSpec((CHUNK, ATTN_WIDTH), lambda b, j: (0, NAT_K // ATTN_WIDTH)),
            pl.BlockSpec((None, None, ATTN_WIDTH, CHUNK), lambda b, j: (0, 0, 1, 0)),
        ],
        out_specs=pl.BlockSpec((TQ, ATTN_WIDTH), lambda b, j: (b * nq + j, 0)),
        scratch_shapes=[
            pltpu.VMEM((nq, TQ, TQ), I32),
            pltpu.VMEM((nq, TQ, TQ), F32),
            pltpu.VMEM((N_HEADS, LANES, TQ), BF16),
            pltpu.VMEM((N_HEADS, TQ), F32),
            pltpu.VMEM((N_HEADS, TQ), F32),
            pltpu.VMEM((ATTN_WIDTH, TQ), F32),
            pltpu.VMEM((N_HEADS, TQ, TQ), F32),
        ],
        compiler_params=_cparams(("parallel", "arbitrary")),
        name="attn",
    )(qvt, qvt, nat, sm_t, sm_t, sm_n, nat_meta, qvt_meta)


def _softplus(x):
    return jnp.maximum(x, 0.0) + jnp.log1p(jnp.exp(-jnp.abs(x)))


def _silu(x):
    return x * jax.nn.sigmoid(x)


def _conv_silu(cur, tail, w_ref, b_ref):
    stack = jnp.concatenate([tail.astype(F32), cur.astype(F32)], axis=0)
    n_tail = tail.shape[0]
    out = cur.astype(F32) * w_ref[CONV_WIDTH - 1:CONV_WIDTH, :]
    for k in range(1, CONV_WIDTH):
        shifted = pltpu.roll(stack, k, 0)[n_tail:, :]
        out = out + shifted * w_ref[CONV_WIDTH - 1 - k:CONV_WIDTH - k, :]
    return _silu(out + b_ref[...])


def _pad_transpose(rows):
    padded = jnp.concatenate([rows, jnp.zeros((LANES - rows.shape[0], LANES), rows.dtype)], axis=0)
    return padded.T


def _ssd_kernel(xs_ref, z_ref, b_ref, c_ref, dtt_ref,
                xs_m_ref, b_m_ref, c_m_ref, dtt_m_ref,
                cwx_ref, cbx_ref, cwb_ref, cbb_ref, cwc_ref, cbc_ref,
                dtb_ref, alog_ref, dsk_ref, ng_ref, exp_ref, tri_ref,
                o_ref, state_ref, *, n_chunks):
    a_neg = -jnp.exp(alog_ref[...])
    dt_bias = dtb_ref[...]
    expand = exp_ref[...]
    tri_u = tri_ref[...]
    li = lax.broadcasted_iota(I32, (CHUNK, CHUNK), 0)
    si = lax.broadcasted_iota(I32, (CHUNK, CHUNK), 1)
    tril = si <= li
    lane = lax.broadcasted_iota(I32, (CHUNK, LANES), 1)
    low_half = lane < SSM_HEAD_DIM
    neg_inf = jnp.float32(-jnp.inf)

    def chunk(x_raw, x_tail, b_raw, b_tail, c_raw, c_tail, dt_raw, z_raw, out_rows, is_meta):
        x = _conv_silu(x_raw, x_tail, cwx_ref, cbx_ref)
        bm = _conv_silu(b_raw, b_tail, cwb_ref, cbb_ref)
        cm = _conv_silu(c_raw, c_tail, cwc_ref, cbc_ref)
        dt_r = _softplus(dt_raw + dt_bias)
        if is_meta:
            tok = lax.broadcasted_iota(I32, dt_r.shape, 1)
            dt_r = jnp.where(tok >= META_PAD, dt_r, 0.0)
        adt_r = dt_r * a_neg
        cs_r = jnp.dot(adt_r, tri_u, precision=lax.Precision.HIGHEST,
                       preferred_element_type=F32)
        dt_c = _pad_transpose(dt_r)
        cs_c = _pad_transpose(cs_r)
        cs_last = cs_c[CHUNK - 1:CHUNK, :]
        dec_c = jnp.exp(cs_last - cs_c)
        ecs_c = jnp.exp(cs_c)
        dtx = jnp.dot(dt_c.astype(BF16), expand, preferred_element_type=F32)
        decx = jnp.dot((dt_c * dec_c).astype(BF16), expand, preferred_element_type=F32)
        ecsx = jnp.dot(ecs_c.astype(BF16), expand, preferred_element_type=F32)
        bm_bf, cm_bf = bm.astype(BF16), cm.astype(BF16)
        state = state_ref[...]
        new_part = lax.dot_general(bm_bf, (x * decx).astype(BF16), (((0,), (0,)), ((), ())),
                                   preferred_element_type=F32)
        if not is_meta:
            xdt = (x * dtx).astype(BF16)
            cb = lax.dot_general(cm_bf, bm_bf, (((1,), (1,)), ((), ())),
                                 preferred_element_type=F32)
            y_off = jnp.dot(cm_bf, state.astype(BF16), preferred_element_type=F32) * ecsx
            slabs = []
            for pp in range(GROUP_HEADS // 2):
                xslab = xdt[:, pp * LANES:(pp + 1) * LANES]
                halves = []
                for e in (2 * pp, 2 * pp + 1):
                    diff = cs_c[:, e:e + 1] - cs_r[e:e + 1, :]
                    lmat = jnp.exp(jnp.where(tril, diff, neg_inf))
                    g = (cb * lmat).astype(BF16)
                    halves.append(jnp.dot(g, xslab, preferred_element_type=F32))
                slabs.append(jnp.where(low_half, halves[0], halves[1]))
            y = jnp.concatenate(slabs, axis=1) + y_off + x * dsk_ref[...]
            y = y * _silu(z_raw.astype(F32))
            ms = jnp.mean(y * y, axis=-1, keepdims=True)
            o_ref[out_rows, :] = (y * lax.rsqrt(ms + RMS_EPS) * ng_ref[...]).astype(o_ref.dtype)
        state_ref[...] = state * ecsx[CHUNK - 1:CHUNK, :] + new_part

    tail_rows = 16
    state_ref[...] = jnp.zeros_like(state_ref)
    chunk(xs_m_ref[...], jnp.zeros((tail_rows, GROUP_CH), BF16),
          b_m_ref[...], jnp.zeros((tail_rows, SSM_STATE), BF16),
          c_m_ref[...], jnp.zeros((tail_rows, SSM_STATE), BF16),
          dtt_m_ref[...], None, None, True)
    chunk(xs_ref[0:CHUNK, :], xs_m_ref[CHUNK - tail_rows:CHUNK, :],
          b_ref[0:CHUNK, :], b_m_ref[CHUNK - tail_rows:CHUNK, :],
          c_ref[0:CHUNK, :], c_m_ref[CHUNK - tail_rows:CHUNK, :],
          dtt_ref[0], z_ref[0:CHUNK, :], slice(0, CHUNK), False)

    def body(ci, carry):
        start = pl.multiple_of(ci * CHUNK, CHUNK)
        tstart = pl.multiple_of(ci * CHUNK - tail_rows, tail_rows)
        rows = pl.ds(start, CHUNK)
        trows = pl.ds(tstart, tail_rows)
        chunk(xs_ref[rows, :], xs_ref[trows, :], b_ref[rows, :], b_ref[trows, :],
              c_ref[rows, :], c_ref[trows, :], dtt_ref[ci], z_ref[rows, :], rows, False)
        return carry

    lax.fori_loop(1, n_chunks, body, 0)


def _ssd(nat, nat_meta, dtt, dtt_meta, conv_w, conv_b, dt_bias, a_log, d_skip, ssm_norm_g,
         *, nb, seq):
    gch, st = GROUP_CH, SSM_STATE
    expand = np.zeros((LANES, gch), np.float32)
    for e in range(GROUP_HEADS):
        expand[e, e * SSM_HEAD_DIM:(e + 1) * SSM_HEAD_DIM] = 1.0
    tri = np.triu(np.ones((CHUNK, CHUNK), np.float32))
    cw = conv_w
    cb = conv_b.reshape(1, -1)
    dsk = jnp.repeat(d_skip, SSM_HEAD_DIM).reshape(1, SSM_INNER)
    ng = ssm_norm_g.reshape(1, SSM_INNER)
    dtb = dt_bias.reshape(SSM_GROUPS, GROUP_HEADS, 1)
    alog = a_log.reshape(SSM_GROUPS, GROUP_HEADS, 1)
    xs_blk, b_blk, c_blk = NAT_XS // gch, NAT_B // st, NAT_C // st
    z_blk = NAT_Z // gch
    cx_blk, cbb_blk, cc_blk = 0, SSM_INNER // st, (SSM_INNER + SSM_GROUPS * st) // st
    return pl.pallas_call(
        functools.partial(_ssd_kernel, n_chunks=seq // CHUNK),
        out_shape=jax.ShapeDtypeStruct((nb * seq, SSM_INNER), BF16),
        grid=(nb, SSM_GROUPS),
        in_specs=[
            pl.BlockSpec((seq, gch), lambda b, g: (b, xs_blk + g)),
            pl.BlockSpec((seq, gch), lambda b, g: (b, z_blk + g)),
            pl.BlockSpec((seq, st), lambda b, g: (b, b_blk + g)),
            pl.BlockSpec((seq, st), lambda b, g: (b, c_blk + g)),
            pl.BlockSpec((None, None, seq // CHUNK, GROUP_HEADS, CHUNK),
                         lambda b, g: (g, b, 0, 0, 0)),
            pl.BlockSpec((CHUNK, gch), lambda b, g: (0, xs_blk + g)),
            pl.BlockSpec((CHUNK, st), lambda b, g: (0, b_blk + g)),
            pl.BlockSpec((CHUNK, st), lambda b, g: (0, c_blk + g)),
            pl.BlockSpec((None, GROUP_HEADS, CHUNK), lambda b, g: (g, 0, 0)),
            pl.BlockSpec((CONV_WIDTH, gch), lambda b, g: (0, cx_blk + g)),
            pl.BlockSpec((1, gch), lambda b, g: (0, cx_blk + g)),
            pl.BlockSpec((CONV_WIDTH, st), lambda b, g: (0, cbb_blk + g)),
            pl.BlockSpec((1, st), lambda b, g: (0, cbb_blk + g)),
            pl.BlockSpec((CONV_WIDTH, st), lambda b, g: (0, cc_blk + g)),
            pl.BlockSpec((1, st), lambda b, g: (0, cc_blk + g)),
            pl.BlockSpec((None, GROUP_HEADS, 1), lambda b, g: (g, 0, 0)),
            pl.BlockSpec((None, GROUP_HEADS, 1), lambda b, g: (g, 0, 0)),
            pl.BlockSpec((1, gch), lambda b, g: (0, g)),
            pl.BlockSpec((1, gch), lambda b, g: (0, g)),
            pl.BlockSpec((LANES, gch), lambda b, g: (0, 0)),
            pl.BlockSpec((CHUNK, CHUNK), lambda b, g: (0, 0)),
        ],
        out_specs=pl.BlockSpec((seq, gch), lambda b, g: (b, g)),
        scratch_shapes=[pltpu.VMEM((SSM_STATE, gch), F32)],
        compiler_params=_cparams(("parallel", "arbitrary")),
        name="ssd",
    )(nat, nat, nat, nat, dtt, nat_meta, nat_meta, nat_meta, dtt_meta,
      cw, cb, cw, cb, cw, cb, dtb, alog, dsk, ng,
      jnp.asarray(expand, BF16), jnp.asarray(tri))


def _merge_kernel(attn_ref, ssm_ref, g0_ref, g1_ref, x_ref, woa_ref, wos_ref, wout_ref,
                  n2g_ref, wr_ref, br_ref, h1_ref, hn2_ref, ti_ref, tw_ref):
    a = jnp.dot(attn_ref[...], woa_ref[...], preferred_element_type=F32)
    s = jnp.dot(ssm_ref[...], wos_ref[...], preferred_element_type=F32)
    g0 = jax.nn.sigmoid(g0_ref[...].astype(F32))
    g1 = jax.nn.sigmoid(g1_ref[...].astype(F32))
    u = g0 * a + g1 * s
    h1 = x_ref[...] + jnp.dot(u.astype(BF16), wout_ref[...], preferred_element_type=F32)
    h1_ref[...] = h1
    hn2 = _rms_rows(h1, n2g_ref[...])
    hn2_ref[...] = hn2.astype(hn2_ref.dtype)
    logits = jnp.dot(hn2, wr_ref[...], precision=lax.Precision.HIGHEST,
                     preferred_element_type=F32) + br_ref[...]
    lane = lax.broadcasted_iota(I32, logits.shape, 1)
    work = logits
    ids, vals = [], []
    for _ in range(TOP_K):
        mx = jnp.max(work, axis=1, keepdims=True)
        idx = jnp.min(jnp.where(work == mx, lane, LANES), axis=1, keepdims=True)
        ids.append(idx)
        vals.append(mx)
        work = jnp.where(lane == idx, -jnp.inf, work)
    es = [jnp.exp(v - vals[0]) for v in vals]
    den = es[0] + es[1] + es[2] + es[3]
    ti = jnp.zeros(logits.shape, I32)
    tw = jnp.zeros(logits.shape, F32)
    for k in range(TOP_K):
        ti = jnp.where(lane == k, ids[k], ti)
        tw = jnp.where(lane == k, es[k] / den, tw)
    ti_ref[...] = ti
    tw_ref[...] = tw


def _merge(attn, ssm, nat, x, woa, wos, wout, n2g, wr, br, *, tm):
    n = x.shape[0]
    gblk = NAT_G // D_MODEL
    const = lambda shape: pl.BlockSpec(shape, lambda i: (0, 0))
    return pl.pallas_call(
        _merge_kernel,
        out_shape=(jax.ShapeDtypeStruct((n, D_MODEL), F32),
                   jax.ShapeDtypeStruct((n, D_MODEL), BF16),
                   jax.ShapeDtypeStruct((n, LANES), I32),
                   jax.ShapeDtypeStruct((n, LANES), F32)),
        grid=(n // tm,),
        in_specs=[
            pl.BlockSpec((tm, ATTN_WIDTH), lambda i: (i, 0)),
            pl.BlockSpec((tm, SSM_INNER), lambda i: (i, 0)),
            pl.BlockSpec((tm, D_MODEL), lambda i: (i, gblk)),
            pl.BlockSpec((tm, D_MODEL), lambda i: (i, gblk + 1)),
            pl.BlockSpec((tm, D_MODEL), lambda i: (i, 0)),
            const((ATTN_WIDTH, D_MODEL)), const((SSM_INNER, D_MODEL)), const((D_MODEL, D_MODEL)),
            const((1, D_MODEL)), const((D_MODEL, LANES)), const((1, LANES)),
        ],
        out_specs=(pl.BlockSpec((tm, D_MODEL), lambda i: (i, 0)),
                   pl.BlockSpec((tm, D_MODEL), lambda i: (i, 0)),
                   pl.BlockSpec((tm, LANES), lambda i: (i, 0)),
                   pl.BlockSpec((tm, LANES), lambda i: (i, 0))),
        compiler_params=_cparams(("parallel",)),
        name="merge",
    )(attn, ssm, nat, nat, x, woa, wos, wout, n2g, wr, br)


def _moe_kernel(te_ref, nv_ref, x_ref, wgu_ref, bgu_ref, wd_ref, bd_ref, o_ref,
                wgu_bf_ref, wd_bf_ref):
    t = pl.program_id(0)
    live = t < nv_ref[0]

    @pl.when(live & ((t == 0) | (te_ref[t] != te_ref[jnp.maximum(t - 1, 0)])))
    def _():
        wgu_bf_ref[...] = wgu_ref[...].astype(BF16)
        wd_bf_ref[...] = wd_ref[...].astype(BF16)

    @pl.when(live)
    def _():
        gu = jnp.dot(x_ref[...], wgu_bf_ref[...], preferred_element_type=F32) + bgu_ref[...]
        gate = jnp.minimum(gu[:, :EXPERT_FF], SWIGLU_LIMIT)
        up = jnp.clip(gu[:, EXPERT_FF:], -SWIGLU_LIMIT, SWIGLU_LIMIT)
        act = (up + 1.0) * (gate * jax.nn.sigmoid(SWIGLU_ALPHA * gate))
        y = jnp.dot(act.astype(BF16), wd_bf_ref[...], preferred_element_type=F32) + bd_ref[...]
        o_ref[...] = y.astype(o_ref.dtype)

    @pl.when(t >= nv_ref[0])
    def _():
        o_ref[...] = jnp.zeros_like(o_ref)


def _moe(tile_expert, n_valid, x_sorted, wgu, bgu, wd, bd):
    rows = x_sorted.shape[0]
    n_tiles = rows // MOE_TM
    grid_spec = pltpu.PrefetchScalarGridSpec(
        num_scalar_prefetch=2,
        grid=(n_tiles,),
        in_specs=[
            pl.BlockSpec((MOE_TM, D_MODEL), lambda t, te, nv: (t, 0)),
            pl.BlockSpec((None, D_MODEL, 2 * EXPERT_FF), lambda t, te, nv: (te[t], 0, 0)),
            pl.BlockSpec((None, 1, 2 * EXPERT_FF), lambda t, te, nv: (te[t], 0, 0)),
            pl.BlockSpec((None, EXPERT_FF, D_MODEL), lambda t, te, nv: (te[t], 0, 0)),
            pl.BlockSpec((None, 1, D_MODEL), lambda t, te, nv: (te[t], 0, 0)),
        ],
        out_specs=pl.BlockSpec((MOE_TM, D_MODEL), lambda t, te, nv: (t, 0)),
        scratch_shapes=[pltpu.VMEM((D_MODEL, 2 * EXPERT_FF), BF16),
                        pltpu.VMEM((EXPERT_FF, D_MODEL), BF16)],
    )
    return pl.pallas_call(
        _moe_kernel,
        out_shape=jax.ShapeDtypeStruct((rows, D_MODEL), BF16),
        grid_spec=grid_spec,
        compiler_params=_cparams(("arbitrary",)),
        name="moe",
    )(tile_expert, n_valid, x_sorted, wgu, bgu, wd, bd)


def _final_kernel(h1_ref, y0_ref, y1_ref, y2_ref, y3_ref, tw_ref, fg_ref, o_ref):
    tw = tw_ref[...]
    h = h1_ref[...]
    for k, y_ref in enumerate((y0_ref, y1_ref, y2_ref, y3_ref)):
        h = h + tw[:, k:k + 1] * y_ref[...].astype(F32)
    o_ref[...] = _rms_rows(h, fg_ref[...])


def _final(h1, ys, tw, fg, *, tm):
    n = h1.shape[0]
    row = lambda w: pl.BlockSpec((tm, w), lambda i: (i, 0))
    return pl.pallas_call(
        _final_kernel,
        out_shape=jax.ShapeDtypeStruct((n, D_MODEL), F32),
        grid=(n // tm,),
        in_specs=[row(D_MODEL)] + [row(D_MODEL)] * TOP_K + [row(LANES),
                  pl.BlockSpec((1, D_MODEL), lambda i: (0, 0))],
        out_specs=row(D_MODEL),
        compiler_params=_cparams(("parallel",)),
        name="final",
    )(h1, *ys, tw, fg)


def _split_w_in(w_in):
    offs = np.cumsum((0,) + IN_SPLITS)
    names = ("q", "k", "v", "qi", "ki", "wi", "z", "xs", "b", "c", "dt", "gate")
    return {nm: w_in[:, offs[i]:offs[i + 1]] for i, nm in enumerate(names)}


def _expert_dispatch(top_i, n_rows_pad):
    n = top_i.shape[0]
    pairs = n * TOP_K
    e_flat = top_i.reshape(-1)
    iota = jnp.arange(pairs, dtype=I32)
    e_sorted, order = lax.sort_key_val(e_flat, iota)
    bounds = jnp.searchsorted(e_sorted, jnp.arange(N_EXPERTS + 1, dtype=I32)).astype(I32)
    grp_start, counts = bounds[:-1], bounds[1:] - bounds[:-1]
    tiles_per_e = (counts + MOE_TM - 1) // MOE_TM
    tile_end = jnp.cumsum(tiles_per_e)
    row_start = (tile_end - tiles_per_e) * MOE_TM
    offset = row_start - grp_start
    n_tiles = n_rows_pad // MOE_TM
    n_valid = tile_end[-1].astype(I32)
    tile_ids = jnp.minimum(jnp.arange(n_tiles, dtype=I32), n_valid - 1)
    tile_expert = jnp.searchsorted(tile_end, tile_ids, side="right").astype(I32)
    tile_expert = jnp.minimum(tile_expert, N_EXPERTS - 1)
    rows = jnp.arange(n_rows_pad, dtype=I32)
    e_row = jnp.repeat(tile_expert, MOE_TM)
    s_row = rows - offset[e_row]
    live = (s_row >= grp_start[e_row]) & (s_row < bounds[1:][e_row])
    src_token = jnp.where(live, order[jnp.clip(s_row, 0, pairs - 1)] // TOP_K, 0)
    _, dest_by_pair = lax.sort_key_val(order, iota + offset[e_sorted])
    return src_token, dest_by_pair.reshape(n, TOP_K), tile_expert, n_valid.reshape(1)


def kernel(x, meta_tokens, norm1_g, w_in, conv_w, conv_b, dt_bias, a_log, d_skip, ssm_norm_g,
           w_o_attn, w_o_ssm, w_out, norm2_g, w_router, b_router, w_gate_up, b_gate_up,
           w_down, b_down, final_g):
    nb, seq, d = x.shape
    n = nb * seq
    assert d == D_MODEL and seq % TQ == 0 and w_in.shape[0] == 1
    xf = x.reshape(n, d)
    meta_pad = jnp.concatenate([jnp.zeros((META_PAD, d), x.dtype), meta_tokens.astype(x.dtype)], 0)
    g1 = norm1_g[0].reshape(1, d)

    w = _split_w_in(w_in[0])
    w_nat = jnp.concatenate([w["k"], w["z"], w["xs"], w["b"], w["c"], w["gate"]], 1).astype(BF16)
    w_qvt = jnp.concatenate([w["q"], w["v"]], 1).T.astype(BF16)
    zpad = lambda c: jnp.zeros((d, c), F32)
    w_sn = jnp.concatenate([w["ki"], zpad(SN_DT - IDX_DIM), w["dt"],
                            zpad(SN_W - SN_DT - SSM_HEADS)], 1).astype(BF16)
    w_st = jnp.concatenate([w["qi"], w["wi"], zpad(ST_W - ST_WI - IDX_HEADS)], 1).T.astype(BF16)

    tabs = _rope_tables(N_META + jnp.arange(seq, dtype=I32))
    tabs_t = tuple(t.T for t in tabs)
    pos_meta = jnp.maximum(jnp.arange(CHUNK, dtype=I32) - META_PAD, 0)
    tabs_m = _rope_tables(pos_meta)
    tabs_mt = tuple(t.T for t in tabs_m)

    tm_nat = 1024 if seq % 1024 == 0 else TQ
    nat = _proj_nat(xf, g1, w_nat, tabs, tm=tm_nat, tn=1024, out_dtype=BF16, rope_tiles=1,
                    rope_slabs=ATTN_WIDTH // LANES, seq_tiles=seq // tm_nat, name="proj_nat")
    sm_n = _proj_nat(xf, g1, w_sn, tabs, tm=tm_nat, tn=SN_W, out_dtype=F32, rope_tiles=1,
                     rope_slabs=1, seq_tiles=seq // tm_nat, name="proj_small_nat")
    qvt = _proj_tr(xf, g1, w_qvt, tabs_t, tm=TQ, nb=nb, out_dtype=BF16,
                   rope_slabs=ATTN_WIDTH // LANES, name="proj_qv_t")
    sm_t = _proj_tr(xf, g1, w_st, tabs_t, tm=TQ, nb=nb, out_dtype=F32,
                    rope_slabs=IDX_HEADS * IDX_DIM // LANES, name="proj_small_t")
    nat_m = _proj_nat(meta_pad, g1, w_nat, tabs_m, tm=CHUNK, tn=1024, out_dtype=BF16, rope_tiles=1,
                      rope_slabs=ATTN_WIDTH // LANES, seq_tiles=1, name="proj_nat_meta")
    sm_n_m = _proj_nat(meta_pad, g1, w_sn, tabs_m, tm=CHUNK, tn=SN_W, out_dtype=F32, rope_tiles=1,
                       rope_slabs=1, seq_tiles=1, name="proj_small_nat_meta")
    qvt_m = _proj_tr(meta_pad, g1, w_qvt, tabs_mt, tm=CHUNK, nb=1, out_dtype=BF16,
                     rope_slabs=ATTN_WIDTH // LANES, name="proj_qv_t_meta")

    attn = _attention(qvt, nat, sm_t, sm_n, nat_m, qvt_m, nb=nb, seq=seq)

    dt_raw = sm_n[:, SN_DT:SN_DT + SSM_HEADS].reshape(nb, seq // CHUNK, CHUNK, SSM_GROUPS,
                                                      GROUP_HEADS)
    dtt = jnp.transpose(dt_raw, (3, 0, 1, 4, 2))
    dt_raw_m = sm_n_m[:, SN_DT:SN_DT + SSM_HEADS].reshape(CHUNK, SSM_GROUPS, GROUP_HEADS)
    dtt_m = jnp.transpose(dt_raw_m, (1, 2, 0))
    ssm = _ssd(nat, nat_m, dtt, dtt_m, conv_w[0], conv_b[0], dt_bias[0], a_log[0], d_skip[0],
               ssm_norm_g[0], nb=nb, seq=seq)

    wr = jnp.concatenate([w_router[0], jnp.zeros((d, LANES - N_EXPERTS), F32)], 1)
    br = jnp.concatenate([b_router[0], jnp.full((LANES - N_EXPERTS,), -1e30, F32)]).reshape(1, LANES)
    h1, hn2, ti, tw = _merge(attn, ssm, nat, xf, w_o_attn[0].astype(BF16), w_o_ssm[0].astype(BF16),
                             w_out[0].astype(BF16), norm2_g[0].reshape(1, d), wr, br,
                             tm=512 if n % 512 == 0 else TQ)

    n_tiles = n * TOP_K // MOE_TM + N_EXPERTS
    src_token, pos, tile_expert, n_valid = _expert_dispatch(ti[:, :TOP_K], n_tiles * MOE_TM)
    x_sorted = jnp.take(hn2, src_token, axis=0)
    y_sorted = _moe(tile_expert, n_valid, x_sorted, w_gate_up[0],
                    b_gate_up[0].reshape(N_EXPERTS, 1, -1), w_down[0],
                    b_down[0].reshape(N_EXPERTS, 1, -1))
    ys = [jnp.take(y_sorted, pos[:, k], axis=0) for k in range(TOP_K)]

    out = _final(h1, ys, tw, final_g.reshape(1, d), tm=512 if n % 512 == 0 else TQ)
    return out.reshape(nb, seq, d)
```

```python
import functools
import math

import numpy as np
import jax
import jax.numpy as jnp
from jax import lax
from jax.experimental import pallas as pl
from jax.experimental.pallas import tpu as pltpu

F32 = jnp.float32
BF16 = jnp.bfloat16
I32 = jnp.int32
I16 = jnp.int16

D_MODEL = 1024
N_META = 16
RMS_EPS = 1e-6
N_HEADS = 16
HEAD_DIM = 64
ATTN_WIDTH = N_HEADS * HEAD_DIM
ROT_DIM = HEAD_DIM // 4
ROPE_THETA = 500000.0
IDX_HEADS = 8
IDX_DIM = 64
TOPK_KEYS_MAX = 256
SSM_INNER = 2 * D_MODEL
SSM_HEAD_DIM = 64
SSM_HEADS = SSM_INNER // SSM_HEAD_DIM
SSM_GROUPS = 4
SSM_STATE = 128
CONV_WIDTH = 4
CHUNK = 128
N_EXPERTS = 32
TOP_K = 4
EXPERT_FF = D_MODEL
SWIGLU_LIMIT = 7.0
SWIGLU_ALPHA = 1.702
IN_SPLITS = (ATTN_WIDTH, ATTN_WIDTH, ATTN_WIDTH, IDX_HEADS * IDX_DIM, IDX_DIM, IDX_HEADS,
             SSM_INNER, SSM_INNER, SSM_GROUPS * SSM_STATE, SSM_GROUPS * SSM_STATE, SSM_HEADS,
             2 * D_MODEL)

LANES = 128
GROUP_CH = SSM_INNER // SSM_GROUPS
GROUP_HEADS = SSM_HEADS // SSM_GROUPS
META_PAD = CHUNK - N_META
TQ = 256
MOE_TM = 512
INT_MIN = -2147483648
VMEM_LIMIT = 56 * 1024 * 1024

NAT_K, NAT_Z, NAT_XS, NAT_B, NAT_C, NAT_G = 0, 1024, 3072, 5120, 5632, 6144
NAT_W = 8192
SN_KI, SN_DT, SN_W = 0, 128, 256
ST_QI, ST_WI, ST_W = 0, 512, 640


def _cparams(sem):
    return pltpu.CompilerParams(dimension_semantics=sem, vmem_limit_bytes=VMEM_LIMIT)


def _rms_rows(x, g):
    ms = jnp.mean(x * x, axis=-1, keepdims=True)
    return x * lax.rsqrt(ms + RMS_EPS) * g


def _proj_nat_kernel(x_ref, g_ref, w_ref, wsm_ref, cos_ref, sa_ref, sb_ref, o_ref, osm_ref,
                     hn_ref, *, rope_tiles):
    j = pl.program_id(1)

    def rope(a):
        return (a * cos_ref[...] + pltpu.roll(a, 8, 1) * sa_ref[...]
                + pltpu.roll(a, LANES - 8, 1) * sb_ref[...])

    @pl.when(j == 0)
    def _():
        hn = _rms_rows(x_ref[...], g_ref[...]).astype(hn_ref.dtype)
        hn_ref[...] = hn
        small = jnp.dot(hn, wsm_ref[...], preferred_element_type=F32)
        osm_ref[:, 0:LANES] = rope(small[:, 0:LANES])
        osm_ref[:, LANES:] = small[:, LANES:]

    acc = jnp.dot(hn_ref[...], w_ref[...], preferred_element_type=F32)

    @pl.when(j < rope_tiles)
    def _():
        for s in range(acc.shape[1] // LANES):
            cols = slice(s * LANES, (s + 1) * LANES)
            o_ref[:, cols] = rope(acc[:, cols]).astype(o_ref.dtype)

    @pl.when(j >= rope_tiles)
    def _():
        o_ref[...] = acc.astype(o_ref.dtype)


def _proj_nat(x, g, w, w_small, tabs, *, tm, tn, rope_tiles, seq_tiles, name):
    n, d = x.shape
    width, width_sm = w.shape[1], w_small.shape[1]
    cos, sa, sb = tabs
    tab_spec = pl.BlockSpec((tm, LANES), lambda i, j: (i % seq_tiles, 0))
    return pl.pallas_call(
        functools.partial(_proj_nat_kernel, rope_tiles=rope_tiles),
        out_shape=(jax.ShapeDtypeStruct((n, width), BF16),
                   jax.ShapeDtypeStruct((n, width_sm), F32)),
        grid=(n // tm, width // tn),
        in_specs=[
            pl.BlockSpec((tm, d), lambda i, j: (i, 0)),
            pl.BlockSpec((1, d), lambda i, j: (0, 0)),
            pl.BlockSpec((d, tn), lambda i, j: (0, j)),
            pl.BlockSpec((d, width_sm), lambda i, j: (0, 0)),
            tab_spec, tab_spec, tab_spec,
        ],
        out_specs=(pl.BlockSpec((tm, tn), lambda i, j: (i, j)),
                   pl.BlockSpec((tm, width_sm), lambda i, j: (i, 0))),
        scratch_shapes=[pltpu.VMEM((tm, d), BF16)],
        compiler_params=_cparams(("parallel", "arbitrary")),
        name=name,
    )(x, g, w, w_small, cos, sa, sb)


def _proj_tr_kernel(x_ref, g_ref, wt_ref, cos_ref, sa_ref, sb_ref, o_ref, osm_ref, *,
                    rope_slabs):
    hn = _rms_rows(x_ref[...], g_ref[...]).astype(BF16)
    acc = lax.dot_general(wt_ref[...], hn, (((1,), (1,)), ((), ())),
                          preferred_element_type=F32)
    cos, sa, sb = cos_ref[...], sa_ref[...], sb_ref[...]
    n_main = o_ref.shape[0] // LANES
    for s in range(acc.shape[0] // LANES):
        a = acc[s * LANES:(s + 1) * LANES, :]
        if s in rope_slabs:
            a = a * cos + pltpu.roll(a, 8, 0) * sa + pltpu.roll(a, LANES - 8, 0) * sb
        if s < n_main:
            o_ref[s * LANES:(s + 1) * LANES, :] = a.astype(o_ref.dtype)
        else:
            osm_ref[(s - n_main) * LANES:(s - n_main + 1) * LANES, :] = a


def _proj_tr(x, g, wt, tabs_t, *, tm, nb, width_main, rope_slabs, name):
    n, d = x.shape
    width = wt.shape[0]
    width_sm = width - width_main
    seq_tiles = n // nb // tm
    cos, sa, sb = tabs_t
    tab_spec = pl.BlockSpec((LANES, tm), lambda i: (0, i % seq_tiles))
    out_map = lambda i: (i // seq_tiles, i % seq_tiles, 0, 0)
    return pl.pallas_call(
        functools.partial(_proj_tr_kernel, rope_slabs=rope_slabs),
        out_shape=(jax.ShapeDtypeStruct((nb, seq_tiles, width_main, tm), BF16),
                   jax.ShapeDtypeStruct((nb, seq_tiles, width_sm, tm), F32)),
        grid=(n // tm,),
        in_specs=[
            pl.BlockSpec((tm, d), lambda i: (i, 0)),
            pl.BlockSpec((1, d), lambda i: (0, 0)),
            pl.BlockSpec((width, d), lambda i: (0, 0)),
            tab_spec, tab_spec, tab_spec,
        ],
        out_specs=(pl.BlockSpec((None, None, width_main, tm), out_map),
                   pl.BlockSpec((None, None, width_sm, tm), out_map)),
        compiler_params=_cparams(("parallel",)),
        name=name,
    )(x, g, wt, cos, sa, sb)


def _rope_tables(pos):
    half = ROT_DIM // 2
    inv_freq = jnp.exp(-math.log(ROPE_THETA) * jnp.arange(half, dtype=F32) / half)
    ang = pos.astype(F32)[:, None] * inv_freq[None, :]
    cos8, sin8 = jnp.cos(ang), jnp.sin(ang)
    t = pos.shape[0]
    cc = np.arange(LANES) % HEAD_DIM
    in_rot = jnp.asarray(cc < ROT_DIM)
    lo = jnp.asarray(cc < half)
    hi = jnp.asarray((cc >= half) & (cc < ROT_DIM))
    idx = jnp.asarray(cc % half)
    cos_l = jnp.take(cos8, idx, axis=1)
    sin_l = jnp.take(sin8, idx, axis=1)
    cos = jnp.where(in_rot[None, :], cos_l, 1.0)
    sa = jnp.where(hi[None, :], sin_l, 0.0)
    sb = jnp.where(lo[None, :], -sin_l, 0.0)
    del t
    return cos, sa, sb


def _attn_kernel(qt_ref, vt_ref, k_ref, qit_ref, wit_ref, ki_ref, kmeta_ref, vtmeta_ref,
                 o_ref, keys_ref, hi_ref, lo_ref, bias_ref, qm_ref, m_ref, l_ref, acc_ref, s_ref,
                 *, ksel, nbits):
    j = pl.program_id(1)
    tq = TQ
    nch = j + 1
    int_min = jnp.int32(INT_MIN)
    neg_inf = jnp.float32(-jnp.inf)

    row128 = lax.broadcasted_iota(I32, (LANES, tq), 0)
    for h in range(N_HEADS):
        slab = qt_ref[(h // 2) * LANES:(h // 2 + 1) * LANES, :]
        keep = (row128 < HEAD_DIM) if h % 2 == 0 else (row128 >= HEAD_DIM)
        scaled = slab.astype(F32) * (HEAD_DIM ** -0.5 * math.log2(math.e))
        qm_ref[h] = jnp.where(keep, scaled, 0.0).astype(qm_ref.dtype)

    w_all = wit_ref[0:IDX_HEADS, :] * (IDX_HEADS ** -0.5 * IDX_DIM ** -0.5)
    qi_bf = [qit_ref[h * IDX_DIM:(h + 1) * IDX_DIM, :].astype(BF16) for h in range(IDX_HEADS)]
    qidx = j * tq + lax.broadcasted_iota(I32, (tq, tq), 1)
    krow = lax.broadcasted_iota(I32, (tq, tq), 0)

    def score_body(c, carry):
        start = pl.multiple_of(c * tq, tq)
        kic = ki_ref[pl.ds(start, tq), :][:, :IDX_DIM].astype(BF16)
        sc = jnp.zeros((tq, tq), F32)
        for h in range(IDX_HEADS):
            lg = jnp.dot(kic, qi_bf[h], preferred_element_type=F32)
            sc = sc + jnp.maximum(lg, 0.0) * w_all[h:h + 1, :]
        bits = pltpu.bitcast(sc, I32)
        skey = jnp.where(bits < 0, bits ^ jnp.int32(0x7FFFFFFF), bits)
        skey = jnp.where(c * tq + krow <= qidx, skey, int_min)
        keys_ref[c] = skey
        hi_ref[c] = (skey >> 16).astype(I16)
        lo_ref[c] = ((skey & 0xFFFF) - 32768).astype(I16)
        return carry

    lax.fori_loop(0, nch, score_body, 0)

    @pl.when(nch % 2 == 1)
    def _():
        hi_ref[nch] = jnp.full((tq, tq), -32768, I16)
        lo_ref[nch] = jnp.full((tq, tq), -32768, I16)

    npair = (nch + 1) // 2

    def count(pred):
        def body(c, cnt):
            hit = pred(keys_ref[c], c).astype(I32)
            return cnt + hit.reshape(tq // 8, 8, tq).sum(axis=0)
        cnt8 = lax.fori_loop(0, nch, body, jnp.zeros((8, tq), I32))
        return jnp.sum(cnt8, axis=0, keepdims=True)

    def count16(ref, cand):
        cand16 = cand.astype(I16)
        one, zero = jnp.int16(1), jnp.int16(0)

        def body(i, cnt):
            for c in (2 * i, 2 * i + 1):
                hit = pltpu.bitcast(jnp.where(ref[c] >= cand16, one, zero), I32)
                cnt = cnt + hit.reshape(tq // 16, 8, tq).sum(axis=0)
            return cnt

        cnt8 = lax.fori_loop(0, npair, body, jnp.zeros((8, tq), I32))
        both = jnp.sum(cnt8, axis=0, keepdims=True)
        return (both & 0xFFFF) + lax.shift_right_logical(both, 16)

    def search16(ref, want):
        def bit_body(it, t_b):
            cand_b = t_b | lax.shift_left(jnp.int32(1), 15 - it)
            cnt = count16(ref, cand_b - 32768)
            return jnp.where(cnt >= want, cand_b, t_b)
        return lax.fori_loop(0, 16, bit_body, jnp.zeros((1, tq), I32))

    hi_b = search16(hi_ref, ksel)
    thr_hi = hi_b - 32768
    n_above = count16(hi_ref, jnp.minimum(thr_hi + 1, 32767))
    n_above = jnp.where(thr_hi >= 32767, 0, n_above)
    thr_hi16 = thr_hi.astype(I16)

    def bucket_body(c, carry):
        lo_ref[c] = jnp.where(hi_ref[c] == thr_hi16, lo_ref[c], jnp.int16(-32768))
        return carry

    lax.fori_loop(0, 2 * npair, bucket_body, 0)
    lo_b = search16(lo_ref, ksel - n_above)
    thr = lax.shift_left(thr_hi, 16) | lo_b
    n_gt = count(lambda kc, c: kc > thr)
    n_eq = count(lambda kc, c: kc == thr)
    need = ksel - n_gt
    fix = (n_eq > need) & (thr != int_min)
    thr_eff = jnp.maximum(thr, int_min + 1)

    def bias_body(c, carry):
        bias_ref[c] = jnp.where(keys_ref[c] >= thr_eff, 0.0, neg_inf)
        return carry

    lax.fori_loop(0, nch, bias_body, 0)

    @pl.when(jnp.max(fix.astype(I32)) > 0)
    def _():
        def idx_body(it, v):
            cand = v | lax.shift_left(jnp.int32(1), nbits - 1 - it)
            cnt = count(lambda kc, c: jnp.where(c * tq + krow < cand, kc, int_min) == thr)
            return jnp.where(cnt <= need - 1, cand, v)

        last = lax.fori_loop(0, nbits, idx_body, jnp.zeros((1, tq), I32))

        def fix_body(c, carry):
            kc = keys_ref[c]
            tie_lim = jnp.where(fix, last, jnp.int32(2 ** 30))
            tie_ok = jnp.where(c * tq + krow <= tie_lim, thr_eff, thr_eff + 1)
            bias_ref[c] = jnp.where(kc >= tie_ok, 0.0, neg_inf)
            return carry

        lax.fori_loop(0, nch, fix_body, 0)

    def chunk_step(kslab_of, vt_of, bias_c, n_keys, first):
        alphas = []
        for h in range(N_HEADS):
            s = jnp.dot(kslab_of(h), qm_ref[h], preferred_element_type=F32) + bias_c
            s_ref[h, 0:n_keys, :] = s
            smax = jnp.max(s, axis=0, keepdims=True)
            if first:
                m_ref[h:h + 1, :] = smax
            else:
                m_old = m_ref[h:h + 1, :]
                m_new = jnp.maximum(m_old, smax)
                alphas.append(jnp.exp2(m_old - m_new))
                m_ref[h:h + 1, :] = m_new
        ones = jnp.ones((16, n_keys), BF16)
        for h in range(N_HEADS):
            p = jnp.exp2(s_ref[h, 0:n_keys, :] - m_ref[h:h + 1, :]).astype(BF16)
            res = jnp.dot(jnp.concatenate([vt_of(h), ones], axis=0), p,
                          preferred_element_type=F32)
            pv, psum = res[:HEAD_DIM, :], res[HEAD_DIM:HEAD_DIM + 1, :]
            rows = slice(h * HEAD_DIM, (h + 1) * HEAD_DIM)
            if first:
                l_ref[h:h + 1, :] = psum
                acc_ref[rows, :] = pv
            else:
                l_ref[h:h + 1, :] = l_ref[h:h + 1, :] * alphas[h] + psum
                acc_ref[rows, :] = acc_ref[rows, :] * alphas[h] + pv

    meta_row = lax.broadcasted_iota(I32, (CHUNK, tq), 0)
    bias_meta = jnp.where(meta_row >= META_PAD, 0.0, neg_inf)
    chunk_step(lambda h: kmeta_ref[:, (h // 2) * LANES:(h // 2 + 1) * LANES],
               lambda h: vtmeta_ref[h * HEAD_DIM:(h + 1) * HEAD_DIM, :], bias_meta, CHUNK, True)

    def flash_body(c, carry):
        start = pl.multiple_of(c * tq, tq)
        chunk_step(lambda h: k_ref[pl.ds(start, tq), (h // 2) * LANES:(h // 2 + 1) * LANES],
                   lambda h: vt_ref[c, h * HEAD_DIM:(h + 1) * HEAD_DIM, :], bias_ref[c], tq, False)
        return carry

    lax.fori_loop(0, nch, flash_body, 0)

    for h in range(N_HEADS):
        rows = slice(h * HEAD_DIM, (h + 1) * HEAD_DIM)
        acc_ref[rows, :] = acc_ref[rows, :] * (1.0 / l_ref[h:h + 1, :])
    o_ref[...] = acc_ref[...].T.astype(o_ref.dtype)


def _attention(qvt, nat, sm_t, sm_n, nat_meta, qvt_meta, *, nb, seq):
    nq = seq // TQ
    ksel = min(TOPK_KEYS_MAX, seq // 4)
    nbits = max(1, int(math.ceil(math.log2(seq))))
    kernel = functools.partial(_attn_kernel, ksel=ksel, nbits=nbits)
    return pl.pallas_call(
        kernel,
        out_shape=jax.ShapeDtypeStruct((nb * seq, ATTN_WIDTH), BF16),
        grid=(nb, nq),
        in_specs=[
            pl.BlockSpec((None, None, ATTN_WIDTH, TQ), lambda b, j: (b, j, 0, 0)),
            pl.BlockSpec((None, nq, ATTN_WIDTH, TQ), lambda b, j: (b, 0, 1, 0)),
            pl.BlockSpec((seq, ATTN_WIDTH), lambda b, j: (b, NAT_K // ATTN_WIDTH)),
            pl.BlockSpec((None, None, IDX_HEADS * IDX_DIM, TQ), lambda b, j: (b, j, 0, 0)),
            pl.BlockSpec((None, None, LANES, TQ), lambda b, j: (b, j, ST_WI // LANES, 0)),
            pl.BlockSpec((seq, LANES), lambda b, j: (b, SN_KI // LANES)),
            pl.BlockSpec((CHUNK, ATTN_WIDTH), lambda b, j: (0, NAT_K // ATTN_WIDTH)),
            pl.BlockSpec((None, None, ATTN_WIDTH, CHUNK), lambda b, j: (0, 0, 1, 0)),
        ],
        out_specs=pl.BlockSpec((TQ, ATTN_WIDTH), lambda b, j: (b * nq + j, 0)),
        scratch_shapes=[
            pltpu.VMEM((nq, TQ, TQ), I32),
            pltpu.VMEM((nq + nq % 2, TQ, TQ), I16),
            pltpu.VMEM((nq + nq % 2, TQ, TQ), I16),
            pltpu.VMEM((nq, TQ, TQ), F32),
            pltpu.VMEM((N_HEADS, LANES, TQ), BF16),
            pltpu.VMEM((N_HEADS, TQ), F32),
            pltpu.VMEM((N_HEADS, TQ), F32),
            pltpu.VMEM((ATTN_WIDTH, TQ), F32),
            pltpu.VMEM((N_HEADS, TQ, TQ), F32),
        ],
        compiler_params=_cparams(("parallel", "arbitrary")),
        name="attn",
    )(qvt, qvt, nat, sm_t, sm_t, sm_n, nat_meta, qvt_meta)


def _softplus(x):
    return jnp.maximum(x, 0.0) + jnp.log1p(jnp.exp(-jnp.abs(x)))


def _sigmoid(x):
    return 0.5 + 0.5 * jnp.tanh(0.5 * x)


def _silu(x):
    h = 0.5 * x
    return h + h * jnp.tanh(h)


def _pad_transpose(rows):
    padded = jnp.concatenate([rows, jnp.zeros((LANES - rows.shape[0], LANES), rows.dtype)], axis=0)
    return padded.T


STAGE_TAIL = 16
SSD_GROUPS_PER_STEP = 2


def _ssd_kernel(xs_ref, z_ref, b_ref, c_ref, dtt_ref,
                xs_m_ref, b_m_ref, c_m_ref, dtt_m_ref,
                cw_ref, cb_ref, dtb_ref, alog_ref, dsk_ref, ng_ref, exp_ref, tri_ref, shift_ref,
                o_ref, state_ref, *, n_chunks):
    gs = range(SSD_GROUPS_PER_STEP)
    a_neg = [-jnp.exp(alog_ref[g]) for g in gs]
    dt_bias = [dtb_ref[g] for g in gs]
    expand = exp_ref[...]
    tri_u = tri_ref[...]
    li = lax.broadcasted_iota(I32, (CHUNK, CHUNK), 0)
    si = lax.broadcasted_iota(I32, (CHUNK, CHUNK), 1)
    tril = si <= li
    lane = lax.broadcasted_iota(I32, (CHUNK, LANES), 1)
    low_half = lane < SSM_HEAD_DIM
    neg_inf = jnp.float32(-jnp.inf)

    xb, xc = GROUP_CH, GROUP_CH + SSM_STATE

    widths = (GROUP_CH, SSM_STATE, SSM_STATE)
    gcols = lambda g: slice(g * GROUP_CH, (g + 1) * GROUP_CH)

    def pieces(refs, rows, g):
        return [r[rows, g * w:(g + 1) * w] for r, w in zip(refs, widths)]

    def chunk(src, rows, tail_src, trows, dt_of, out_rows, is_meta):
        stacks = []
        for g in gs:
            tail = ([jnp.zeros((STAGE_TAIL, w), BF16) for w in widths] if tail_src is None
                    else pieces(tail_src, trows, g))
            stacks.append(jnp.concatenate(
                [jnp.concatenate(tail, axis=1), jnp.concatenate(pieces(src, rows, g), axis=1)],
                axis=0))
        shifted = [jnp.dot(shift_ref[...], stacks[g], preferred_element_type=F32) for g in gs]
        x, bm_bf, cm_bf = [], [], []
        for g in gs:
            conv = cb_ref[g]
            for k in range(CONV_WIDTH):
                conv = conv + (shifted[g][k * CHUNK:(k + 1) * CHUNK, :]
                               * cw_ref[g, CONV_WIDTH - 1 - k:CONV_WIDTH - k, :])
            act = _silu(conv)
            x.append(act[:, 0:xb])
            bm_bf.append(act[:, xb:xc].astype(BF16))
            cm_bf.append(act[:, xc:].astype(BF16))
        dt_r = []
        for g in gs:
            d = _softplus(dt_of(g) + dt_bias[g])
            if is_meta:
                d = jnp.where(lax.broadcasted_iota(I32, d.shape, 1) >= META_PAD, d, 0.0)
            dt_r.append(d)
        cs_r = [jnp.dot(dt_r[g] * a_neg[g], tri_u, precision=lax.Precision.HIGHEST,
                        preferred_element_type=F32) for g in gs]
        dt_c = [_pad_transpose(dt_r[g]) for g in gs]
        cs_c = [_pad_transpose(cs_r[g]) for g in gs]
        dec_c = [jnp.exp(cs_c[g][CHUNK - 1:CHUNK, :] - cs_c[g]) for g in gs]
        ecs_c = [jnp.exp(cs_c[g]) for g in gs]
        dtx = [jnp.dot(dt_c[g].astype(BF16), expand, preferred_element_type=F32) for g in gs]
        decx = [jnp.dot((dt_c[g] * dec_c[g]).astype(BF16), expand, preferred_element_type=F32)
                for g in gs]
        ecsx = [jnp.dot(ecs_c[g].astype(BF16), expand, preferred_element_type=F32) for g in gs]
        state = [state_ref[g] for g in gs]
        new_part = [lax.dot_general(bm_bf[g], (x[g] * decx[g]).astype(BF16),
                                    (((0,), (0,)), ((), ())), preferred_element_type=F32)
                    for g in gs]
        if not is_meta:
            xdt = [(x[g] * dtx[g]).astype(BF16) for g in gs]
            cb = [lax.dot_general(cm_bf[g], bm_bf[g], (((1,), (1,)), ((), ())),
                                  preferred_element_type=F32) for g in gs]
            y_off = [jnp.dot(cm_bf[g], state[g].astype(BF16), preferred_element_type=F32)
                     * ecsx[g] for g in gs]
            slabs = [[] for _ in gs]
            for pp in range(GROUP_HEADS // 2):
                halves = [[] for _ in gs]
                for e in (2 * pp, 2 * pp + 1):
                    for g in gs:
                        diff = cs_c[g][:, e:e + 1] - cs_r[g][e:e + 1, :]
                        lmat = jnp.exp(jnp.where(tril, diff, neg_inf))
                        halves[g].append(jnp.dot((cb[g] * lmat).astype(BF16),
                                                 xdt[g][:, pp * LANES:(pp + 1) * LANES],
                                                 preferred_element_type=F32))
                for g in gs:
                    slabs[g].append(jnp.where(low_half, halves[g][0], halves[g][1]))
            for g in gs:
                y = jnp.concatenate(slabs[g], axis=1) + y_off[g] + x[g] * dsk_ref[:, gcols(g)]
                y = y * _silu(z_ref[rows, gcols(g)].astype(F32))
                ms = jnp.mean(y * y, axis=-1, keepdims=True)
                o_ref[out_rows, gcols(g)] = (y * lax.rsqrt(ms + RMS_EPS)
                                             * ng_ref[:, gcols(g)]).astype(o_ref.dtype)
        for g in gs:
            state_ref[g] = state[g] * ecsx[g][CHUNK - 1:CHUNK, :] + new_part[g]

    state_ref[...] = jnp.zeros_like(state_ref)
    real, meta = (xs_ref, b_ref, c_ref), (xs_m_ref, b_m_ref, c_m_ref)
    all_rows = slice(0, CHUNK)
    chunk(meta, all_rows, None, None, lambda g: dtt_m_ref[g], None, True)
    chunk(real, all_rows, meta, slice(CHUNK - STAGE_TAIL, CHUNK), lambda g: dtt_ref[g, 0],
          all_rows, False)

    def body(ci, carry):
        rows = pl.ds(pl.multiple_of(ci * CHUNK, CHUNK), CHUNK)
        trows = pl.ds(pl.multiple_of(ci * CHUNK - STAGE_TAIL, STAGE_TAIL), STAGE_TAIL)
        chunk(real, rows, real, trows, lambda g: dtt_ref[g, ci], rows, False)
        return carry

    lax.fori_loop(1, n_chunks, body, 0)


def _ssd(nat, nat_meta, dtt, dtt_meta, conv_w, conv_b, dt_bias, a_log, d_skip, ssm_norm_g,
         *, nb, seq):
    gch, st = GROUP_CH, SSM_STATE
    expand = np.zeros((LANES, gch), np.float32)
    for e in range(GROUP_HEADS):
        expand[e, e * SSM_HEAD_DIM:(e + 1) * SSM_HEAD_DIM] = 1.0
    tri = np.triu(np.ones((CHUNK, CHUNK), np.float32))
    def group_cols(p):
        px = p[:, :SSM_INNER].reshape(-1, SSM_GROUPS, gch)
        pb = p[:, SSM_INNER:SSM_INNER + SSM_GROUPS * st].reshape(-1, SSM_GROUPS, st)
        pc = p[:, SSM_INNER + SSM_GROUPS * st:].reshape(-1, SSM_GROUPS, st)
        return jnp.transpose(jnp.concatenate([px, pb, pc], axis=2), (1, 0, 2))

    cw = group_cols(conv_w)
    cb = group_cols(conv_b.reshape(1, -1))
    stage_w = gch + 2 * st
    shift = np.zeros((CONV_WIDTH * CHUNK, STAGE_TAIL + CHUNK), np.float32)
    for k in range(CONV_WIDTH):
        shift[k * CHUNK + np.arange(CHUNK), STAGE_TAIL + np.arange(CHUNK) - k] = 1.0
    dsk = jnp.repeat(d_skip, SSM_HEAD_DIM).reshape(1, SSM_INNER)
    ng = ssm_norm_g.reshape(1, SSM_INNER)
    dtb = dt_bias.reshape(SSM_GROUPS, GROUP_HEADS, 1)
    alog = a_log.reshape(SSM_GROUPS, GROUP_HEADS, 1)
    gps = SSD_GROUPS_PER_STEP
    xs_blk, z_blk = NAT_XS // (gps * gch), NAT_Z // (gps * gch)
    b_blk, c_blk = NAT_B // (gps * st), NAT_C // (gps * st)
    return pl.pallas_call(
        functools.partial(_ssd_kernel, n_chunks=seq // CHUNK),
        out_shape=jax.ShapeDtypeStruct((nb * seq, SSM_INNER), BF16),
        grid=(nb, SSM_GROUPS // gps),
        in_specs=[
            pl.BlockSpec((seq, gps * gch), lambda b, g: (b, xs_blk + g)),
            pl.BlockSpec((seq, gps * gch), lambda b, g: (b, z_blk + g)),
            pl.BlockSpec((seq, gps * st), lambda b, g: (b, b_blk + g)),
            pl.BlockSpec((seq, gps * st), lambda b, g: (b, c_blk + g)),
            pl.BlockSpec((gps, None, seq // CHUNK, GROUP_HEADS, CHUNK),
                         lambda b, g: (g, b, 0, 0, 0)),
            pl.BlockSpec((CHUNK, gps * gch), lambda b, g: (0, xs_blk + g)),
            pl.BlockSpec((CHUNK, gps * st), lambda b, g: (0, b_blk + g)),
            pl.BlockSpec((CHUNK, gps * st), lambda b, g: (0, c_blk + g)),
            pl.BlockSpec((gps, GROUP_HEADS, CHUNK), lambda b, g: (g, 0, 0)),
            pl.BlockSpec((gps, CONV_WIDTH, stage_w), lambda b, g: (g, 0, 0)),
            pl.BlockSpec((gps, 1, stage_w), lambda b, g: (g, 0, 0)),
            pl.BlockSpec((gps, GROUP_HEADS, 1), lambda b, g: (g, 0, 0)),
            pl.BlockSpec((gps, GROUP_HEADS, 1), lambda b, g: (g, 0, 0)),
            pl.BlockSpec((1, gps * gch), lambda b, g: (0, g)),
            pl.BlockSpec((1, gps * gch), lambda b, g: (0, g)),
            pl.BlockSpec((LANES, gch), lambda b, g: (0, 0)),
            pl.BlockSpec((CHUNK, CHUNK), lambda b, g: (0, 0)),
            pl.BlockSpec((CONV_WIDTH * CHUNK, STAGE_TAIL + CHUNK), lambda b, g: (0, 0)),
        ],
        out_specs=pl.BlockSpec((seq, gps * gch), lambda b, g: (b, g)),
        scratch_shapes=[pltpu.VMEM((gps, SSM_STATE, gch), F32)],
        compiler_params=_cparams(("parallel", "arbitrary")),
        name="ssd",
    )(nat, nat, nat, nat, dtt, nat_meta, nat_meta, nat_meta, dtt_meta,
      cw, cb, dtb, alog, dsk, ng,
      jnp.asarray(expand, BF16), jnp.asarray(tri), jnp.asarray(shift, BF16))


def _merge_kernel(attn_ref, ssm_ref, g0_ref, g1_ref, x_ref, woa_ref, wos_ref, wout_ref,
                  n2g_ref, wrh_ref, wrl_ref, br_ref, h1_ref, hn2_ref, ti_ref, tw_ref):
    a = jnp.dot(attn_ref[...], woa_ref[...], preferred_element_type=F32)
    s = jnp.dot(ssm_ref[...], wos_ref[...], preferred_element_type=F32)
    g0 = _sigmoid(g0_ref[...].astype(F32))
    g1 = _sigmoid(g1_ref[...].astype(F32))
    u = g0 * a + g1 * s
    h1 = x_ref[...] + jnp.dot(u.astype(BF16), wout_ref[...], preferred_element_type=F32)
    h1_ref[...] = h1
    hn2 = _rms_rows(h1, n2g_ref[...])
    hn2_hi = hn2.astype(BF16)
    hn2_ref[...] = hn2_hi
    hn2_lo = (hn2 - hn2_hi.astype(F32)).astype(BF16)
    logits = (jnp.dot(hn2_hi, wrh_ref[...], preferred_element_type=F32)
              + jnp.dot(hn2_lo, wrh_ref[...], preferred_element_type=F32)
              + jnp.dot(hn2_hi, wrl_ref[...], preferred_element_type=F32)
              + br_ref[...])
    lane = lax.broadcasted_iota(I32, logits.shape, 1)
    work = logits
    ids, vals = [], []
    for _ in range(TOP_K):
        mx = jnp.max(work, axis=1, keepdims=True)
        idx = jnp.min(jnp.where(work == mx, lane, LANES), axis=1, keepdims=True)
        ids.append(idx)
        vals.append(mx)
        work = jnp.where(lane == idx, -jnp.inf, work)
    es = [jnp.exp(v - vals[0]) for v in vals]
    den = es[0] + es[1] + es[2] + es[3]
    ti = jnp.zeros(logits.shape, I32)
    tw = jnp.zeros(logits.shape, F32)
    for k in range(TOP_K):
        ti = jnp.where(lane == k, ids[k], ti)
        tw = jnp.where(lane == k, es[k] / den, tw)
    ti_ref[...] = ti
    tw_ref[...] = tw


def _merge(attn, ssm, nat, x, woa, wos, wout, n2g, wr_hi, wr_lo, br, *, tm):
    n = x.shape[0]
    gblk = NAT_G // D_MODEL
    const = lambda shape: pl.BlockSpec(shape, lambda i: (0, 0))
    return pl.pallas_call(
        _merge_kernel,
        out_shape=(jax.ShapeDtypeStruct((n, D_MODEL), F32),
                   jax.ShapeDtypeStruct((n, D_MODEL), BF16),
                   jax.ShapeDtypeStruct((n, LANES), I32),
                   jax.ShapeDtypeStruct((n, LANES), F32)),
        grid=(n // tm,),
        in_specs=[
            pl.BlockSpec((tm, ATTN_WIDTH), lambda i: (i, 0)),
            pl.BlockSpec((tm, SSM_INNER), lambda i: (i, 0)),
            pl.BlockSpec((tm, D_MODEL), lambda i: (i, gblk)),
            pl.BlockSpec((tm, D_MODEL), lambda i: (i, gblk + 1)),
            pl.BlockSpec((tm, D_MODEL), lambda i: (i, 0)),
            const((ATTN_WIDTH, D_MODEL)), const((SSM_INNER, D_MODEL)), const((D_MODEL, D_MODEL)),
            const((1, D_MODEL)), const((D_MODEL, LANES)), const((D_MODEL, LANES)),
            const((1, LANES)),
        ],
        out_specs=(pl.BlockSpec((tm, D_MODEL), lambda i: (i, 0)),
                   pl.BlockSpec((tm, D_MODEL), lambda i: (i, 0)),
                   pl.BlockSpec((tm, LANES), lambda i: (i, 0)),
                   pl.BlockSpec((tm, LANES), lambda i: (i, 0))),
        compiler_params=_cparams(("parallel",)),
        name="merge",
    )(attn, ssm, nat, nat, x, woa, wos, wout, n2g, wr_hi, wr_lo, br)


def _moe_kernel(te_ref, nv_ref, x_ref, wgu_ref, bgu_ref, wd_ref, bd_ref, o_ref,
                wgu_bf_ref, wd_bf_ref):
    t = pl.program_id(0)
    live = t < nv_ref[0]

    @pl.when(live & ((t == 0) | (te_ref[t] != te_ref[jnp.maximum(t - 1, 0)])))
    def _():
        wgu_bf_ref[...] = wgu_ref[...].astype(BF16)
        wd_bf_ref[...] = wd_ref[...].astype(BF16)

    @pl.when(live)
    def _():
        gu = jnp.dot(x_ref[...], wgu_bf_ref[...], preferred_element_type=F32) + bgu_ref[...]
        gate = jnp.minimum(gu[:, :EXPERT_FF], SWIGLU_LIMIT)
        up = jnp.clip(gu[:, EXPERT_FF:], -SWIGLU_LIMIT, SWIGLU_LIMIT)
        act = (up + 1.0) * (gate * _sigmoid(SWIGLU_ALPHA * gate))
        y = jnp.dot(act.astype(BF16), wd_bf_ref[...], preferred_element_type=F32) + bd_ref[...]
        o_ref[...] = y.astype(o_ref.dtype)

    @pl.when(t >= nv_ref[0])
    def _():
        o_ref[...] = jnp.zeros_like(o_ref)


def _moe(tile_expert, n_valid, x_sorted, wgu, bgu, wd, bd):
    rows = x_sorted.shape[0]
    n_tiles = rows // MOE_TM
    grid_spec = pltpu.PrefetchScalarGridSpec(
        num_scalar_prefetch=2,
        grid=(n_tiles,),
        in_specs=[
            pl.BlockSpec((MOE_TM, D_MODEL), lambda t, te, nv: (t, 0)),
            pl.BlockSpec((None, D_MODEL, 2 * EXPERT_FF), lambda t, te, nv: (te[t], 0, 0)),
            pl.BlockSpec((None, 1, 2 * EXPERT_FF), lambda t, te, nv: (te[t], 0, 0)),
            pl.BlockSpec((None, EXPERT_FF, D_MODEL), lambda t, te, nv: (te[t], 0, 0)),
            pl.BlockSpec((None, 1, D_MODEL), lambda t, te, nv: (te[t], 0, 0)),
        ],
        out_specs=pl.BlockSpec((MOE_TM, D_MODEL), lambda t, te, nv: (t, 0)),
        scratch_shapes=[pltpu.VMEM((D_MODEL, 2 * EXPERT_FF), BF16),
                        pltpu.VMEM((EXPERT_FF, D_MODEL), BF16)],
    )
    return pl.pallas_call(
        _moe_kernel,
        out_shape=jax.ShapeDtypeStruct((rows, D_MODEL), BF16),
        grid_spec=grid_spec,
        compiler_params=_cparams(("arbitrary",)),
        name="moe",
    )(tile_expert, n_valid, x_sorted, wgu, bgu, wd, bd)


def _final_kernel(h1_ref, y_ref, tw_ref, fg_ref, o_ref):
    tw = tw_ref[...]
    h = h1_ref[...]
    for k in range(TOP_K):
        h = h + tw[:, k:k + 1] * y_ref[:, k * D_MODEL:(k + 1) * D_MODEL].astype(F32)
    o_ref[...] = _rms_rows(h, fg_ref[...])


def _final(h1, y_tok, tw, fg, *, tm):
    n = h1.shape[0]
    row = lambda w: pl.BlockSpec((tm, w), lambda i: (i, 0))
    return pl.pallas_call(
        _final_kernel,
        out_shape=jax.ShapeDtypeStruct((n, D_MODEL), F32),
        grid=(n // tm,),
        in_specs=[row(D_MODEL), row(TOP_K * D_MODEL), row(LANES),
                  pl.BlockSpec((1, D_MODEL), lambda i: (0, 0))],
        out_specs=row(D_MODEL),
        compiler_params=_cparams(("parallel",)),
        name="final",
    )(h1, y_tok, tw, fg)


def _split_w_in(w_in):
    offs = np.cumsum((0,) + IN_SPLITS)
    names = ("q", "k", "v", "qi", "ki", "wi", "z", "xs", "b", "c", "dt", "gate")
    return {nm: w_in[:, offs[i]:offs[i + 1]] for i, nm in enumerate(names)}


def _expert_dispatch(top_i, n_rows_pad):
    n = top_i.shape[0]
    pairs = n * TOP_K
    e_flat = top_i.reshape(-1)
    iota = jnp.arange(pairs, dtype=I32)
    e_sorted, order = lax.sort_key_val(e_flat, iota)
    bounds = jnp.searchsorted(e_sorted, jnp.arange(N_EXPERTS + 1, dtype=I32),
                              method="compare_all").astype(I32)
    grp_start, counts = bounds[:-1], bounds[1:] - bounds[:-1]
    tiles_per_e = (counts + MOE_TM - 1) // MOE_TM
    tile_end = jnp.cumsum(tiles_per_e)
    row_start = (tile_end - tiles_per_e) * MOE_TM
    offset = row_start - grp_start
    n_tiles = n_rows_pad // MOE_TM
    n_valid = tile_end[-1].astype(I32)
    tile_ids = jnp.minimum(jnp.arange(n_tiles, dtype=I32), n_valid - 1)
    tile_expert = jnp.searchsorted(tile_end, tile_ids, side="right",
                                   method="compare_all").astype(I32)
    tile_expert = jnp.minimum(tile_expert, N_EXPERTS - 1)
    rows = jnp.arange(n_rows_pad, dtype=I32)
    e_row = jnp.repeat(tile_expert, MOE_TM)
    s_row = rows - offset[e_row]
    live = (s_row >= grp_start[e_row]) & (s_row < bounds[1:][e_row])
    src_token = jnp.where(live, order[jnp.clip(s_row, 0, pairs - 1)] // TOP_K, rows % n)
    _, dest_by_pair = lax.sort_key_val(order, iota + offset[e_sorted])
    return src_token, dest_by_pair.reshape(n, TOP_K), tile_expert, n_valid.reshape(1)


def kernel(x, meta_tokens, norm1_g, w_in, conv_w, conv_b, dt_bias, a_log, d_skip, ssm_norm_g,
           w_o_attn, w_o_ssm, w_out, norm2_g, w_router, b_router, w_gate_up, b_gate_up,
           w_down, b_down, final_g):
    nb, seq, d = x.shape
    n = nb * seq
    assert d == D_MODEL and seq % TQ == 0 and w_in.shape[0] == 1
    xf = x.reshape(n, d)
    meta_pad = jnp.concatenate([jnp.zeros((META_PAD, d), x.dtype), meta_tokens.astype(x.dtype)], 0)
    g1 = norm1_g[0].reshape(1, d)

    w = _split_w_in(w_in[0])
    w_nat = jnp.concatenate([w["k"], w["z"], w["xs"], w["b"], w["c"], w["gate"]], 1).astype(BF16)
    w_qvt = jnp.concatenate([w["q"], w["v"]], 1).T.astype(BF16)
    zpad = lambda c: jnp.zeros((d, c), F32)
    w_sn = jnp.concatenate([w["ki"], zpad(SN_DT - IDX_DIM), w["dt"],
                            zpad(SN_W - SN_DT - SSM_HEADS)], 1).astype(BF16)
    w_st = jnp.concatenate([w["qi"], w["wi"], zpad(ST_W - ST_WI - IDX_HEADS)], 1).T.astype(BF16)

    tabs = _rope_tables(N_META + jnp.arange(seq, dtype=I32))
    tabs_t = tuple(t.T for t in tabs)
    pos_meta = jnp.maximum(jnp.arange(CHUNK, dtype=I32) - META_PAD, 0)
    tabs_m = _rope_tables(pos_meta)
    tabs_mt = tuple(t.T for t in tabs_m)

    tm_nat = 1024 if seq % 1024 == 0 else TQ
    w_t = jnp.concatenate([w_qvt, w_st], axis=0)
    q_slabs = ATTN_WIDTH // LANES
    tr_rope = tuple(range(q_slabs)) + tuple(
        2 * q_slabs + s for s in range(IDX_HEADS * IDX_DIM // LANES))
    nat, sm_n = _proj_nat(xf, g1, w_nat, w_sn, tabs, tm=tm_nat, tn=1024, rope_tiles=1,
                          seq_tiles=seq // tm_nat, name="proj_nat")
    qvt, sm_t = _proj_tr(xf, g1, w_t, tabs_t, tm=TQ, nb=nb, width_main=2 * ATTN_WIDTH,
                         rope_slabs=tr_rope, name="proj_t")
    nat_m, sm_n_m = _proj_nat(meta_pad, g1, w_nat, w_sn, tabs_m, tm=CHUNK, tn=1024, rope_tiles=1,
                              seq_tiles=1, name="proj_nat_meta")
    qvt_m, _ = _proj_tr(meta_pad, g1, w_t, tabs_mt, tm=CHUNK, nb=1, width_main=2 * ATTN_WIDTH,
                        rope_slabs=tr_rope, name="proj_t_meta")

    attn = _attention(qvt, nat, sm_t, sm_n, nat_m, qvt_m, nb=nb, seq=seq)

    dt_raw = sm_n[:, SN_DT:SN_DT + SSM_HEADS].reshape(nb, seq // CHUNK, CHUNK, SSM_GROUPS,
                                                      GROUP_HEADS)
    dtt = jnp.transpose(dt_raw, (3, 0, 1, 4, 2))
    dt_raw_m = sm_n_m[:, SN_DT:SN_DT + SSM_HEADS].reshape(CHUNK, SSM_GROUPS, GROUP_HEADS)
    dtt_m = jnp.transpose(dt_raw_m, (1, 2, 0))
    ssm = _ssd(nat, nat_m, dtt, dtt_m, conv_w[0], conv_b[0], dt_bias[0], a_log[0], d_skip[0],
               ssm_norm_g[0], nb=nb, seq=seq)

    wr = jnp.concatenate([w_router[0], jnp.zeros((d, LANES - N_EXPERTS), F32)], 1)
    br = jnp.concatenate([b_router[0], jnp.full((LANES - N_EXPERTS,), -1e30, F32)]).reshape(1, LANES)
    wr_hi = wr.astype(BF16)
    wr_lo = (wr - wr_hi.astype(F32)).astype(BF16)
    h1, hn2, ti, tw = _merge(attn, ssm, nat, xf, w_o_attn[0].astype(BF16), w_o_ssm[0].astype(BF16),
                             w_out[0].astype(BF16), norm2_g[0].reshape(1, d), wr_hi, wr_lo, br,
                             tm=512 if n % 512 == 0 else TQ)

    n_tiles = n * TOP_K // MOE_TM + N_EXPERTS
    src_token, pos, tile_expert, n_valid = _expert_dispatch(ti[:, :TOP_K], n_tiles * MOE_TM)
    x_sorted = hn2.at[src_token].get(mode="promise_in_bounds")
    y_sorted = _moe(tile_expert, n_valid, x_sorted, w_gate_up[0],
                    b_gate_up[0].reshape(N_EXPERTS, 1, -1), w_down[0],
                    b_down[0].reshape(N_EXPERTS, 1, -1))
    y_tok = y_sorted.at[pos.reshape(-1)].get(mode="promise_in_bounds").reshape(n, TOP_K * d)

    out = _final(h1, y_tok, tw, final_g.reshape(1, d), tm=512 if n % 512 == 0 else TQ)
    return out.reshape(nb, seq, d)
```

```python
import functools
import math

import numpy as np
import jax
import jax.numpy as jnp
from jax import lax
from jax.experimental import pallas as pl
from jax.experimental.pallas import tpu as pltpu

F32 = jnp.float32
BF16 = jnp.bfloat16
I32 = jnp.int32
I16 = jnp.int16

D_MODEL = 1024
N_META = 16
RMS_EPS = 1e-6
N_HEADS = 16
HEAD_DIM = 64
ATTN_WIDTH = N_HEADS * HEAD_DIM
ROT_DIM = HEAD_DIM // 4
ROPE_THETA = 500000.0
IDX_HEADS = 8
IDX_DIM = 64
TOPK_KEYS_MAX = 256
SSM_INNER = 2 * D_MODEL
SSM_HEAD_DIM = 64
SSM_HEADS = SSM_INNER // SSM_HEAD_DIM
SSM_GROUPS = 4
SSM_STATE = 128
CONV_WIDTH = 4
CHUNK = 128
N_EXPERTS = 32
TOP_K = 4
EXPERT_FF = D_MODEL
SWIGLU_LIMIT = 7.0
SWIGLU_ALPHA = 1.702
IN_SPLITS = (ATTN_WIDTH, ATTN_WIDTH, ATTN_WIDTH, IDX_HEADS * IDX_DIM, IDX_DIM, IDX_HEADS,
             SSM_INNER, SSM_INNER, SSM_GROUPS * SSM_STATE, SSM_GROUPS * SSM_STATE, SSM_HEADS,
             2 * D_MODEL)

LANES = 128
GROUP_CH = SSM_INNER // SSM_GROUPS
GROUP_HEADS = SSM_HEADS // SSM_GROUPS
META_PAD = CHUNK - N_META
TQ = 256
FLASH_LAG = 8
MOE_TM = 512
INT_MIN = -2147483648
VMEM_LIMIT = 56 * 1024 * 1024

NAT_K, NAT_Z, NAT_XS, NAT_B, NAT_C, NAT_G = 0, 1024, 3072, 5120, 5632, 6144
NAT_W = 8192
SN_KI, SN_DT, SN_W = 0, 128, 256
ST_QI, ST_WI, ST_W = 0, 512, 640


def _cparams(sem):
    return pltpu.CompilerParams(dimension_semantics=sem, vmem_limit_bytes=VMEM_LIMIT)


def _rms_rows(x, g):
    ms = jnp.mean(x * x, axis=-1, keepdims=True)
    return x * lax.rsqrt(ms + RMS_EPS) * g


def _proj_nat_kernel(x_ref, g_ref, w_ref, wsm_ref, cos_ref, sa_ref, sb_ref, o_ref, osm_ref,
                     hn_ref, *, rope_tiles):
    j = pl.program_id(1)

    def rope(a):
        return (a * cos_ref[...] + pltpu.roll(a, 8, 1) * sa_ref[...]
                + pltpu.roll(a, LANES - 8, 1) * sb_ref[...])

    @pl.when(j == 0)
    def _():
        hn = _rms_rows(x_ref[...], g_ref[...]).astype(hn_ref.dtype)
        hn_ref[...] = hn
        small = jnp.dot(hn, wsm_ref[...], preferred_element_type=F32)
        osm_ref[:, 0:LANES] = rope(small[:, 0:LANES])
        osm_ref[:, LANES:] = small[:, LANES:]

    acc = jnp.dot(hn_ref[...], w_ref[...], preferred_element_type=F32)

    @pl.when(j < rope_tiles)
    def _():
        for s in range(acc.shape[1] // LANES):
            cols = slice(s * LANES, (s + 1) * LANES)
            o_ref[:, cols] = rope(acc[:, cols]).astype(o_ref.dtype)

    @pl.when(j >= rope_tiles)
    def _():
        o_ref[...] = acc.astype(o_ref.dtype)


def _proj_nat(x, g, w, w_small, tabs, *, tm, tn, rope_tiles, seq_tiles, name):
    n, d = x.shape
    width, width_sm = w.shape[1], w_small.shape[1]
    cos, sa, sb = tabs
    tab_spec = pl.BlockSpec((tm, LANES), lambda i, j: (i % seq_tiles, 0))
    return pl.pallas_call(
        functools.partial(_proj_nat_kernel, rope_tiles=rope_tiles),
        out_shape=(jax.ShapeDtypeStruct((n, width), BF16),
                   jax.ShapeDtypeStruct((n, width_sm), F32)),
        grid=(n // tm, width // tn),
        in_specs=[
            pl.BlockSpec((tm, d), lambda i, j: (i, 0)),
            pl.BlockSpec((1, d), lambda i, j: (0, 0)),
            pl.BlockSpec((d, tn), lambda i, j: (0, j)),
            pl.BlockSpec((d, width_sm), lambda i, j: (0, 0)),
            tab_spec, tab_spec, tab_spec,
        ],
        out_specs=(pl.BlockSpec((tm, tn), lambda i, j: (i, j)),
                   pl.BlockSpec((tm, width_sm), lambda i, j: (i, 0))),
        scratch_shapes=[pltpu.VMEM((tm, d), BF16)],
        compiler_params=_cparams(("parallel", "arbitrary")),
        name=name,
    )(x, g, w, w_small, cos, sa, sb)


def _proj_tr_kernel(x_ref, g_ref, wt_ref, cos_ref, sa_ref, sb_ref, o_ref, osm_ref, *,
                    rope_slabs):
    hn = _rms_rows(x_ref[...], g_ref[...]).astype(BF16)
    acc = lax.dot_general(wt_ref[...], hn, (((1,), (1,)), ((), ())),
                          preferred_element_type=F32)
    cos, sa, sb = cos_ref[...], sa_ref[...], sb_ref[...]
    n_main = o_ref.shape[0] // LANES
    for s in range(acc.shape[0] // LANES):
        a = acc[s * LANES:(s + 1) * LANES, :]
        if s in rope_slabs:
            a = a * cos + pltpu.roll(a, 8, 0) * sa + pltpu.roll(a, LANES - 8, 0) * sb
        if s < n_main:
            o_ref[s * LANES:(s + 1) * LANES, :] = a.astype(o_ref.dtype)
        else:
            osm_ref[(s - n_main) * LANES:(s - n_main + 1) * LANES, :] = a


def _proj_tr(x, g, wt, tabs_t, *, tm, nb, width_main, rope_slabs, name):
    n, d = x.shape
    width = wt.shape[0]
    width_sm = width - width_main
    seq_tiles = n // nb // tm
    cos, sa, sb = tabs_t
    tab_spec = pl.BlockSpec((LANES, tm), lambda i: (0, i % seq_tiles))
    out_map = lambda i: (i // seq_tiles, i % seq_tiles, 0, 0)
    return pl.pallas_call(
        functools.partial(_proj_tr_kernel, rope_slabs=rope_slabs),
        out_shape=(jax.ShapeDtypeStruct((nb, seq_tiles, width_main, tm), BF16),
                   jax.ShapeDtypeStruct((nb, seq_tiles, width_sm, tm), F32)),
        grid=(n // tm,),
        in_specs=[
            pl.BlockSpec((tm, d), lambda i: (i, 0)),
            pl.BlockSpec((1, d), lambda i: (0, 0)),
            pl.BlockSpec((width, d), lambda i: (0, 0)),
            tab_spec, tab_spec, tab_spec,
        ],
        out_specs=(pl.BlockSpec((None, None, width_main, tm), out_map),
                   pl.BlockSpec((None, None, width_sm, tm), out_map)),
        compiler_params=_cparams(("parallel",)),
        name=name,
    )(x, g, wt, cos, sa, sb)


def _rope_tables(pos):
    half = ROT_DIM // 2
    inv_freq = jnp.exp(-math.log(ROPE_THETA) * jnp.arange(half, dtype=F32) / half)
    ang = pos.astype(F32)[:, None] * inv_freq[None, :]
    cos8, sin8 = jnp.cos(ang), jnp.sin(ang)
    t = pos.shape[0]
    cc = np.arange(LANES) % HEAD_DIM
    in_rot = jnp.asarray(cc < ROT_DIM)
    lo = jnp.asarray(cc < half)
    hi = jnp.asarray((cc >= half) & (cc < ROT_DIM))
    idx = jnp.asarray(cc % half)
    cos_l = jnp.take(cos8, idx, axis=1)
    sin_l = jnp.take(sin8, idx, axis=1)
    cos = jnp.where(in_rot[None, :], cos_l, 1.0)
    sa = jnp.where(hi[None, :], sin_l, 0.0)
    sb = jnp.where(lo[None, :], -sin_l, 0.0)
    del t
    return cos, sa, sb


def _attn_kernel(qt_ref, vt_ref, k_ref, qit_ref, wit_ref, ki_ref, kmeta_ref, vtmeta_ref,
                 o_ref, keys_ref, hi_ref, lo_ref, bias_ref, qm_ref, m_ref, l_ref, acc_ref, s_ref,
                 *, ksel, nbits):
    j = pl.program_id(1)
    tq = TQ
    nch = j + 1
    int_min = jnp.int32(INT_MIN)
    neg_inf = jnp.float32(-jnp.inf)

    row128 = lax.broadcasted_iota(I32, (LANES, tq), 0)
    for h in range(N_HEADS):
        slab = qt_ref[(h // 2) * LANES:(h // 2 + 1) * LANES, :]
        keep = (row128 < HEAD_DIM) if h % 2 == 0 else (row128 >= HEAD_DIM)
        scaled = slab.astype(F32) * (HEAD_DIM ** -0.5 * math.log2(math.e))
        qm_ref[h] = jnp.where(keep, scaled, 0.0).astype(qm_ref.dtype)

    w_all = wit_ref[0:IDX_HEADS, :] * (IDX_HEADS ** -0.5 * IDX_DIM ** -0.5)
    qi_bf = [qit_ref[h * IDX_DIM:(h + 1) * IDX_DIM, :].astype(BF16) for h in range(IDX_HEADS)]
    qidx = j * tq + lax.broadcasted_iota(I32, (tq, tq), 1)
    krow = lax.broadcasted_iota(I32, (tq, tq), 0)

    def score_body(c, carry):
        start = pl.multiple_of(c * tq, tq)
        kic = ki_ref[pl.ds(start, tq), :][:, :IDX_DIM].astype(BF16)
        sc = jnp.zeros((tq, tq), F32)
        for h in range(IDX_HEADS):
            lg = jnp.dot(kic, qi_bf[h], preferred_element_type=F32)
            sc = sc + jnp.maximum(lg, 0.0) * w_all[h:h + 1, :]
        bits = pltpu.bitcast(sc, I32)
        skey = jnp.where(bits < 0, bits ^ jnp.int32(0x7FFFFFFF), bits)
        skey = jnp.where(c * tq + krow <= qidx, skey, int_min)
        keys_ref[c] = skey
        hi_ref[c] = (skey >> 16).astype(I16)
        lo_ref[c] = ((skey & 0xFFFF) - 32768).astype(I16)
        return carry

    lax.fori_loop(0, nch, score_body, 0)

    @pl.when(nch % 2 == 1)
    def _():
        hi_ref[nch] = jnp.full((tq, tq), -32768, I16)
        lo_ref[nch] = jnp.full((tq, tq), -32768, I16)

    npair = (nch + 1) // 2

    def count(pred):
        def body(c, cnt):
            hit = pred(keys_ref[c], c).astype(I32)
            return cnt + hit.reshape(tq // 8, 8, tq).sum(axis=0)
        cnt8 = lax.fori_loop(0, nch, body, jnp.zeros((8, tq), I32))
        return jnp.sum(cnt8, axis=0, keepdims=True)

    def count16(ref, cand):
        cand16 = cand.astype(I16)
        one, zero = jnp.int16(1), jnp.int16(0)

        def body(i, cnt):
            for c in (2 * i, 2 * i + 1):
                hit = pltpu.bitcast(jnp.where(ref[c] >= cand16, one, zero), I32)
                cnt = cnt + hit.reshape(tq // 16, 8, tq).sum(axis=0)
            return cnt

        cnt8 = lax.fori_loop(0, npair, body, jnp.zeros((8, tq), I32))
        both = jnp.sum(cnt8, axis=0, keepdims=True)
        return (both & 0xFFFF) + lax.shift_right_logical(both, 16)

    def search16(ref, want):
        def bit_body(it, t_b):
            cand_b = t_b | lax.shift_left(jnp.int32(1), 15 - it)
            cnt = count16(ref, cand_b - 32768)
            return jnp.where(cnt >= want, cand_b, t_b)
        return lax.fori_loop(0, 16, bit_body, jnp.zeros((1, tq), I32))

    hi_b = search16(hi_ref, ksel)
    thr_hi = hi_b - 32768
    n_above = count16(hi_ref, jnp.minimum(thr_hi + 1, 32767))
    n_above = jnp.where(thr_hi >= 32767, 0, n_above)
    thr_hi16 = thr_hi.astype(I16)

    def bucket_body(c, carry):
        lo_ref[c] = jnp.where(hi_ref[c] == thr_hi16, lo_ref[c], jnp.int16(-32768))
        return carry

    lax.fori_loop(0, 2 * npair, bucket_body, 0)
    lo_b = search16(lo_ref, ksel - n_above)
    thr = lax.shift_left(thr_hi, 16) | lo_b
    n_gt = count(lambda kc, c: kc > thr)
    n_eq = count(lambda kc, c: kc == thr)
    need = ksel - n_gt
    fix = (n_eq > need) & (thr != int_min)
    thr_eff = jnp.maximum(thr, int_min + 1)

    def bias_body(c, carry):
        bias_ref[c] = jnp.where(keys_ref[c] >= thr_eff, 0.0, neg_inf)
        return carry

    lax.fori_loop(0, nch, bias_body, 0)

    @pl.when(jnp.max(fix.astype(I32)) > 0)
    def _():
        def idx_body(it, v):
            cand = v | lax.shift_left(jnp.int32(1), nbits - 1 - it)
            cnt = count(lambda kc, c: jnp.where(c * tq + krow < cand, kc, int_min) == thr)
            return jnp.where(cnt <= need - 1, cand, v)

        last = lax.fori_loop(0, nbits, idx_body, jnp.zeros((1, tq), I32))

        def fix_body(c, carry):
            kc = keys_ref[c]
            tie_lim = jnp.where(fix, last, jnp.int32(2 ** 30))
            tie_ok = jnp.where(c * tq + krow <= tie_lim, thr_eff, thr_eff + 1)
            bias_ref[c] = jnp.where(kc >= tie_ok, 0.0, neg_inf)
            return carry

        lax.fori_loop(0, nch, fix_body, 0)

    def chunk_step(kslab_of, vt_of, bias_c, n_keys, first):
        alphas = []
        ones = jnp.ones((16, n_keys), BF16)

        def scores(h):
            s = jnp.dot(kslab_of(h), qm_ref[h], preferred_element_type=F32) + bias_c
            s_ref[h, 0:n_keys, :] = s
            smax = jnp.max(s, axis=0, keepdims=True)
            if first:
                m_ref[h:h + 1, :] = smax
            else:
                m_old = m_ref[h:h + 1, :]
                m_new = jnp.maximum(m_old, smax)
                alphas.append(jnp.exp2(m_old - m_new))
                m_ref[h:h + 1, :] = m_new

        def values(h):
            p = jnp.exp2(s_ref[h, 0:n_keys, :] - m_ref[h:h + 1, :]).astype(BF16)
            res = jnp.dot(jnp.concatenate([vt_of(h), ones], axis=0), p,
                          preferred_element_type=F32)
            pv, psum = res[:HEAD_DIM, :], res[HEAD_DIM:HEAD_DIM + 1, :]
            rows = slice(h * HEAD_DIM, (h + 1) * HEAD_DIM)
            if first:
                l_ref[h:h + 1, :] = psum
                acc_ref[rows, :] = pv
            else:
                l_ref[h:h + 1, :] = l_ref[h:h + 1, :] * alphas[h] + psum
                acc_ref[rows, :] = acc_ref[rows, :] * alphas[h] + pv

        for h in range(N_HEADS + FLASH_LAG):
            if h < N_HEADS:
                scores(h)
            if h >= FLASH_LAG:
                values(h - FLASH_LAG)

    meta_row = lax.broadcasted_iota(I32, (CHUNK, tq), 0)
    bias_meta = jnp.where(meta_row >= META_PAD, 0.0, neg_inf)
    chunk_step(lambda h: kmeta_ref[:, (h // 2) * LANES:(h // 2 + 1) * LANES],
               lambda h: vtmeta_ref[h * HEAD_DIM:(h + 1) * HEAD_DIM, :], bias_meta, CHUNK, True)

    def flash_body(c, carry):
        start = pl.multiple_of(c * tq, tq)
        chunk_step(lambda h: k_ref[pl.ds(start, tq), (h // 2) * LANES:(h // 2 + 1) * LANES],
                   lambda h: vt_ref[c, h * HEAD_DIM:(h + 1) * HEAD_DIM, :], bias_ref[c], tq, False)
        return carry

    lax.fori_loop(0, nch, flash_body, 0)

    for h in range(N_HEADS):
        rows = slice(h * HEAD_DIM, (h + 1) * HEAD_DIM)
        acc_ref[rows, :] = acc_ref[rows, :] * (1.0 / l_ref[h:h + 1, :])
    o_ref[...] = acc_ref[...].T.astype(o_ref.dtype)


def _attention(qvt, nat, sm_t, sm_n, nat_meta, qvt_meta, *, nb, seq):
    nq = seq // TQ
    ksel = min(TOPK_KEYS_MAX, seq // 4)
    nbits = max(1, int(math.ceil(math.log2(seq))))
    kernel = functools.partial(_attn_kernel, ksel=ksel, nbits=nbits)
    return pl.pallas_call(
        kernel,
        out_shape=jax.ShapeDtypeStruct((nb * seq, ATTN_WIDTH), BF16),
        grid=(nb, nq),
        in_specs=[
            pl.BlockSpec((None, None, ATTN_WIDTH, TQ), lambda b, j: (b, j, 0, 0)),
            pl.BlockSpec((None, nq, ATTN_WIDTH, TQ), lambda b, j: (b, 0, 1, 0)),
            pl.BlockSpec((seq, ATTN_WIDTH), lambda b, j: (b, NAT_K // ATTN_WIDTH)),
            pl.BlockSpec((None, None, IDX_HEADS * IDX_DIM, TQ), lambda b, j: (b, j, 0, 0)),
            pl.BlockSpec((None, None, LANES, TQ), lambda b, j: (b, j, ST_WI // LANES, 0)),
            pl.BlockSpec((seq, LANES), lambda b, j: (b, SN_KI // LANES)),
            pl.BlockSpec((CHUNK, ATTN_WIDTH), lambda b, j: (0, NAT_K // ATTN_WIDTH)),
            pl.BlockSpec((None, None, ATTN_WIDTH, CHUNK), lambda b, j: (0, 0, 1, 0)),
        ],
        out_specs=pl.BlockSpec((TQ, ATTN_WIDTH), lambda b, j: (b * nq + j, 0)),
        scratch_shapes=[
            pltpu.VMEM((nq, TQ, TQ), I32),
            pltpu.VMEM((nq + nq % 2, TQ, TQ), I16),
            pltpu.VMEM((nq + nq % 2, TQ, TQ), I16),
            pltpu.VMEM((nq, TQ, TQ), F32),
            pltpu.VMEM((N_HEADS, LANES, TQ), BF16),
            pltpu.VMEM((N_HEADS, TQ), F32),
            pltpu.VMEM((N_HEADS, TQ), F32),
            pltpu.VMEM((ATTN_WIDTH, TQ), F32),
            pltpu.VMEM((N_HEADS, TQ, TQ), F32),
        ],
        compiler_params=_cparams(("parallel", "arbitrary")),
        name="attn",
    )(qvt, qvt, nat, sm_t, sm_t, sm_n, nat_meta, qvt_meta)


def _softplus(x):
    return jnp.maximum(x, 0.0) + jnp.log1p(jnp.exp(-jnp.abs(x)))


def _sigmoid(x):
    return 0.5 + 0.5 * jnp.tanh(0.5 * x)


def _silu(x):
    h = 0.5 * x
    return h + h * jnp.tanh(h)


def _pad_transpose(rows):
    padded = jnp.concatenate([rows, jnp.zeros((LANES - rows.shape[0], LANES), rows.dtype)], axis=0)
    return padded.T


STAGE_TAIL = 16
SSD_GROUPS_PER_STEP = 2


def _ssd_kernel(xs_ref, z_ref, b_ref, c_ref, dtt_ref,
                xs_m_ref, b_m_ref, c_m_ref, dtt_m_ref,
                cw_ref, cb_ref, dtb_ref, alog_ref, dsk_ref, ng_ref, exp_ref, tri_ref, shift_ref,
                o_ref, state_ref, *, n_chunks):
    gs = range(SSD_GROUPS_PER_STEP)
    a_neg = [-jnp.exp(alog_ref[g]) for g in gs]
    dt_bias = [dtb_ref[g] for g in gs]
    expand = exp_ref[...]
    tri_u = tri_ref[...]
    li = lax.broadcasted_iota(I32, (CHUNK, CHUNK), 0)
    si = lax.broadcasted_iota(I32, (CHUNK, CHUNK), 1)
    tril = si <= li
    lane = lax.broadcasted_iota(I32, (CHUNK, LANES), 1)
    low_half = lane < SSM_HEAD_DIM
    neg_inf = jnp.float32(-jnp.inf)

    xb, xc = GROUP_CH, GROUP_CH + SSM_STATE

    widths = (GROUP_CH, SSM_STATE, SSM_STATE)
    gcols = lambda g: slice(g * GROUP_CH, (g + 1) * GROUP_CH)

    def pieces(refs, rows, g):
        return [r[rows, g * w:(g + 1) * w] for r, w in zip(refs, widths)]

    def chunk(src, rows, tail_src, trows, dt_of, out_rows, is_meta):
        stacks = []
        for g in gs:
            tail = ([jnp.zeros((STAGE_TAIL, w), BF16) for w in widths] if tail_src is None
                    else pieces(tail_src, trows, g))
            stacks.append(jnp.concatenate(
                [jnp.concatenate(tail, axis=1), jnp.concatenate(pieces(src, rows, g), axis=1)],
                axis=0))
        shifted = [jnp.dot(shift_ref[...], stacks[g], preferred_element_type=F32) for g in gs]
        x, bm_bf, cm_bf = [], [], []
        for g in gs:
            conv = cb_ref[g]
            for k in range(CONV_WIDTH):
                conv = conv + (shifted[g][k * CHUNK:(k + 1) * CHUNK, :]
                               * cw_ref[g, CONV_WIDTH - 1 - k:CONV_WIDTH - k, :])
            act = _silu(conv)
            x.append(act[:, 0:xb])
            bm_bf.append(act[:, xb:xc].astype(BF16))
            cm_bf.append(act[:, xc:].astype(BF16))
        dt_r = []
        for g in gs:
            d = _softplus(dt_of(g) + dt_bias[g])
            if is_meta:
                d = jnp.where(lax.broadcasted_iota(I32, d.shape, 1) >= META_PAD, d, 0.0)
            dt_r.append(d)
        cs_r = [jnp.dot(dt_r[g] * a_neg[g], tri_u, precision=lax.Precision.HIGHEST,
                        preferred_element_type=F32) for g in gs]
        dt_c = [_pad_transpose(dt_r[g]) for g in gs]
        cs_c = [_pad_transpose(cs_r[g]) for g in gs]
        dec_c = [jnp.exp(cs_c[g][CHUNK - 1:CHUNK, :] - cs_c[g]) for g in gs]
        ecs_c = [jnp.exp(cs_c[g]) for g in gs]
        dtx = [jnp.dot(dt_c[g].astype(BF16), expand, preferred_element_type=F32) for g in gs]
        decx = [jnp.dot((dt_c[g] * dec_c[g]).astype(BF16), expand, preferred_element_type=F32)
                for g in gs]
        ecsx = [jnp.dot(ecs_c[g].astype(BF16), expand, preferred_element_type=F32) for g in gs]
        state = [state_ref[g] for g in gs]
        new_part = [lax.dot_general(bm_bf[g], (x[g] * decx[g]).astype(BF16),
                                    (((0,), (0,)), ((), ())), preferred_element_type=F32)
                    for g in gs]
        if not is_meta:
            xdt = [(x[g] * dtx[g]).astype(BF16) for g in gs]
            cb = [lax.dot_general(cm_bf[g], bm_bf[g], (((1,), (1,)), ((), ())),
                                  preferred_element_type=F32) for g in gs]
            y_off = [jnp.dot(cm_bf[g], state[g].astype(BF16), preferred_element_type=F32)
                     * ecsx[g] for g in gs]
            slabs = [[] for _ in gs]
            for pp in range(GROUP_HEADS // 2):
                halves = [[] for _ in gs]
                for e in (2 * pp, 2 * pp + 1):
                    for g in gs:
                        diff = cs_c[g][:, e:e + 1] - cs_r[g][e:e + 1, :]
                        lmat = jnp.exp(jnp.where(tril, diff, neg_inf))
                        halves[g].append(jnp.dot((cb[g] * lmat).astype(BF16),
                                                 xdt[g][:, pp * LANES:(pp + 1) * LANES],
                                                 preferred_element_type=F32))
                for g in gs:
                    slabs[g].append(jnp.where(low_half, halves[g][0], halves[g][1]))
            for g in gs:
                y = jnp.concatenate(slabs[g], axis=1) + y_off[g] + x[g] * dsk_ref[:, gcols(g)]
                y = y * _silu(z_ref[rows, gcols(g)].astype(F32))
                ms = jnp.mean(y * y, axis=-1, keepdims=True)
                o_ref[out_rows, gcols(g)] = (y * lax.rsqrt(ms + RMS_EPS)
                                             * ng_ref[:, gcols(g)]).astype(o_ref.dtype)
        for g in gs:
            state_ref[g] = state[g] * ecsx[g][CHUNK - 1:CHUNK, :] + new_part[g]

    state_ref[...] = jnp.zeros_like(state_ref)
    real, meta = (xs_ref, b_ref, c_ref), (xs_m_ref, b_m_ref, c_m_ref)
    all_rows = slice(0, CHUNK)
    chunk(meta, all_rows, None, None, lambda g: dtt_m_ref[g], None, True)
    chunk(real, all_rows, meta, slice(CHUNK - STAGE_TAIL, CHUNK), lambda g: dtt_ref[g, 0],
          all_rows, False)

    def body(ci, carry):
        rows = pl.ds(pl.multiple_of(ci * CHUNK, CHUNK), CHUNK)
        trows = pl.ds(pl.multiple_of(ci * CHUNK - STAGE_TAIL, STAGE_TAIL), STAGE_TAIL)
        chunk(real, rows, real, trows, lambda g: dtt_ref[g, ci], rows, False)
        return carry

    lax.fori_loop(1, n_chunks, body, 0)


def _ssd(nat, nat_meta, dtt, dtt_meta, conv_w, conv_b, dt_bias, a_log, d_skip, ssm_norm_g,
         *, nb, seq):
    gch, st = GROUP_CH, SSM_STATE
    expand = np.zeros((LANES, gch), np.float32)
    for e in range(GROUP_HEADS):
        expand[e, e * SSM_HEAD_DIM:(e + 1) * SSM_HEAD_DIM] = 1.0
    tri = np.triu(np.ones((CHUNK, CHUNK), np.float32))
    def group_cols(p):
        px = p[:, :SSM_INNER].reshape(-1, SSM_GROUPS, gch)
        pb = p[:, SSM_INNER:SSM_INNER + SSM_GROUPS * st].reshape(-1, SSM_GROUPS, st)
        pc = p[:, SSM_INNER + SSM_GROUPS * st:].reshape(-1, SSM_GROUPS, st)
        return jnp.transpose(jnp.concatenate([px, pb, pc], axis=2), (1, 0, 2))

    cw = group_cols(conv_w)
    cb = group_cols(conv_b.reshape(1, -1))
    stage_w = gch + 2 * st
    shift = np.zeros((CONV_WIDTH * CHUNK, STAGE_TAIL + CHUNK), np.float32)
    for k in range(CONV_WIDTH):
        shift[k * CHUNK + np.arange(CHUNK), STAGE_TAIL + np.arange(CHUNK) - k] = 1.0
    dsk = jnp.repeat(d_skip, SSM_HEAD_DIM).reshape(1, SSM_INNER)
    ng = ssm_norm_g.reshape(1, SSM_INNER)
    dtb = dt_bias.reshape(SSM_GROUPS, GROUP_HEADS, 1)
    alog = a_log.reshape(SSM_GROUPS, GROUP_HEADS, 1)
    gps = SSD_GROUPS_PER_STEP
    xs_blk, z_blk = NAT_XS // (gps * gch), NAT_Z // (gps * gch)
    b_blk, c_blk = NAT_B // (gps * st), NAT_C // (gps * st)
    return pl.pallas_call(
        functools.partial(_ssd_kernel, n_chunks=seq // CHUNK),
        out_shape=jax.ShapeDtypeStruct((nb * seq, SSM_INNER), BF16),
        grid=(nb, SSM_GROUPS // gps),
        in_specs=[
            pl.BlockSpec((seq, gps * gch), lambda b, g: (b, xs_blk + g)),
            pl.BlockSpec((seq, gps * gch), lambda b, g: (b, z_blk + g)),
            pl.BlockSpec((seq, gps * st), lambda b, g: (b, b_blk + g)),
            pl.BlockSpec((seq, gps * st), lambda b, g: (b, c_blk + g)),
            pl.BlockSpec((gps, None, seq // CHUNK, GROUP_HEADS, CHUNK),
                         lambda b, g: (g, b, 0, 0, 0)),
            pl.BlockSpec((CHUNK, gps * gch), lambda b, g: (0, xs_blk + g)),
            pl.BlockSpec((CHUNK, gps * st), lambda b, g: (0, b_blk + g)),
            pl.BlockSpec((CHUNK, gps * st), lambda b, g: (0, c_blk + g)),
            pl.BlockSpec((gps, GROUP_HEADS, CHUNK), lambda b, g: (g, 0, 0)),
            pl.BlockSpec((gps, CONV_WIDTH, stage_w), lambda b, g: (g, 0, 0)),
            pl.BlockSpec((gps, 1, stage_w), lambda b, g: (g, 0, 0)),
            pl.BlockSpec((gps, GROUP_HEADS, 1), lambda b, g: (g, 0, 0)),
            pl.BlockSpec((gps, GROUP_HEADS, 1), lambda b, g: (g, 0, 0)),
            pl.BlockSpec((1, gps * gch), lambda b, g: (0, g)),
            pl.BlockSpec((1, gps * gch), lambda b, g: (0, g)),
            pl.BlockSpec((LANES, gch), lambda b, g: (0, 0)),
            pl.BlockSpec((CHUNK, CHUNK), lambda b, g: (0, 0)),
            pl.BlockSpec((CONV_WIDTH * CHUNK, STAGE_TAIL + CHUNK), lambda b, g: (0, 0)),
        ],
        out_specs=pl.BlockSpec((seq, gps * gch), lambda b, g: (b, g)),
        scratch_shapes=[pltpu.VMEM((gps, SSM_STATE, gch), F32)],
        compiler_params=_cparams(("parallel", "arbitrary")),
        name="ssd",
    )(nat, nat, nat, nat, dtt, nat_meta, nat_meta, nat_meta, dtt_meta,
      cw, cb, dtb, alog, dsk, ng,
      jnp.asarray(expand, BF16), jnp.asarray(tri), jnp.asarray(shift, BF16))


def _merge_kernel(attn_ref, ssm_ref, g0_ref, g1_ref, x_ref, woa_ref, wos_ref, wout_ref,
                  n2g_ref, wrh_ref, wrl_ref, br_ref, ltri_ref,
                  h1_ref, hn2_ref, ti_ref, tw_ref, hist_ref):
    a = jnp.dot(attn_ref[...], woa_ref[...], preferred_element_type=F32)
    s = jnp.dot(ssm_ref[...], wos_ref[...], preferred_element_type=F32)
    g0 = _sigmoid(g0_ref[...].astype(F32))
    g1 = _sigmoid(g1_ref[...].astype(F32))
    u = g0 * a + g1 * s
    h1 = x_ref[...] + jnp.dot(u.astype(BF16), wout_ref[...], preferred_element_type=F32)
    h1_ref[...] = h1
    hn2 = _rms_rows(h1, n2g_ref[...])
    hn2_hi = hn2.astype(BF16)
    hn2_ref[...] = hn2_hi
    hn2_lo = (hn2 - hn2_hi.astype(F32)).astype(BF16)
    logits = (jnp.dot(hn2_hi, wrh_ref[...], preferred_element_type=F32)
              + jnp.dot(hn2_lo, wrh_ref[...], preferred_element_type=F32)
              + jnp.dot(hn2_hi, wrl_ref[...], preferred_element_type=F32)
              + br_ref[...])
    lane = lax.broadcasted_iota(I32, logits.shape, 1)
    work = logits
    ids, vals = [], []
    for _ in range(TOP_K):
        mx = jnp.max(work, axis=1, keepdims=True)
        idx = jnp.min(jnp.where(work == mx, lane, LANES), axis=1, keepdims=True)
        ids.append(idx)
        vals.append(mx)
        work = jnp.where(lane == idx, -jnp.inf, work)
    es = [jnp.exp(v - vals[0]) for v in vals]
    den = es[0] + es[1] + es[2] + es[3]
    routed = jnp.zeros(logits.shape, F32)
    for k in range(TOP_K):
        routed = jnp.where(lane == ids[k], 1.0, routed)
    before = jnp.dot(ltri_ref[...], routed.astype(BF16), preferred_element_type=F32)
    hist_ref[...] = jnp.broadcast_to(jnp.sum(routed, axis=0, keepdims=True),
                                     hist_ref.shape).astype(I32)
    ti = jnp.zeros(logits.shape, I32)
    tw = jnp.zeros(logits.shape, F32)
    for k in range(TOP_K):
        rank = jnp.sum(jnp.where(lane == ids[k], before, 0.0), axis=1, keepdims=True)
        ti = jnp.where(lane == k, ids[k], ti)
        ti = jnp.where(lane == TOP_K + k, rank.astype(I32), ti)
        tw = jnp.where(lane == k, es[k] / den, tw)
    ti_ref[...] = ti
    tw_ref[...] = tw


def _merge(attn, ssm, nat, x, woa, wos, wout, n2g, wr_hi, wr_lo, br, *, tm):
    n = x.shape[0]
    gblk = NAT_G // D_MODEL
    const = lambda shape: pl.BlockSpec(shape, lambda i: (0, 0))
    ltri = jnp.asarray(np.tril(np.ones((tm, tm), np.float32), -1), BF16)
    return pl.pallas_call(
        _merge_kernel,
        out_shape=(jax.ShapeDtypeStruct((n, D_MODEL), F32),
                   jax.ShapeDtypeStruct((n, D_MODEL), BF16),
                   jax.ShapeDtypeStruct((n, LANES), I32),
                   jax.ShapeDtypeStruct((n, LANES), F32),
                   jax.ShapeDtypeStruct((n // tm * 8, LANES), I32)),
        grid=(n // tm,),
        in_specs=[
            pl.BlockSpec((tm, ATTN_WIDTH), lambda i: (i, 0)),
            pl.BlockSpec((tm, SSM_INNER), lambda i: (i, 0)),
            pl.BlockSpec((tm, D_MODEL), lambda i: (i, gblk)),
            pl.BlockSpec((tm, D_MODEL), lambda i: (i, gblk + 1)),
            pl.BlockSpec((tm, D_MODEL), lambda i: (i, 0)),
            const((ATTN_WIDTH, D_MODEL)), const((SSM_INNER, D_MODEL)), const((D_MODEL, D_MODEL)),
            const((1, D_MODEL)), const((D_MODEL, LANES)), const((D_MODEL, LANES)),
            const((1, LANES)), const((tm, tm)),
        ],
        out_specs=(pl.BlockSpec((tm, D_MODEL), lambda i: (i, 0)),
                   pl.BlockSpec((tm, D_MODEL), lambda i: (i, 0)),
                   pl.BlockSpec((tm, LANES), lambda i: (i, 0)),
                   pl.BlockSpec((tm, LANES), lambda i: (i, 0)),
                   pl.BlockSpec((8, LANES), lambda i: (i, 0))),
        compiler_params=_cparams(("parallel",)),
        name="merge",
    )(attn, ssm, nat, nat, x, woa, wos, wout, n2g, wr_hi, wr_lo, br, ltri)


def _moe_kernel(te_ref, nv_ref, x_ref, wgu_ref, bgu_ref, wd_ref, bd_ref, o_ref,
                wgu_bf_ref, wd_bf_ref):
    t = pl.program_id(0)
    live = t < nv_ref[0]

    @pl.when(live & ((t == 0) | (te_ref[t] != te_ref[jnp.maximum(t - 1, 0)])))
    def _():
        wgu_bf_ref[...] = wgu_ref[...].astype(BF16)
        wd_bf_ref[...] = wd_ref[...].astype(BF16)

    @pl.when(live)
    def _():
        gu = jnp.dot(x_ref[...], wgu_bf_ref[...], preferred_element_type=F32) + bgu_ref[...]
        gate = jnp.minimum(gu[:, :EXPERT_FF], SWIGLU_LIMIT)
        up = jnp.clip(gu[:, EXPERT_FF:], -SWIGLU_LIMIT, SWIGLU_LIMIT)
        act = (up + 1.0) * (gate * _sigmoid(SWIGLU_ALPHA * gate))
        y = jnp.dot(act.astype(BF16), wd_bf_ref[...], preferred_element_type=F32) + bd_ref[...]
        o_ref[...] = y.astype(o_ref.dtype)

    @pl.when(t >= nv_ref[0])
    def _():
        o_ref[...] = jnp.zeros_like(o_ref)


def _moe(tile_expert, n_valid, x_sorted, wgu, bgu, wd, bd):
    rows = x_sorted.shape[0]
    n_tiles = rows // MOE_TM
    grid_spec = pltpu.PrefetchScalarGridSpec(
        num_scalar_prefetch=2,
        grid=(n_tiles,),
        in_specs=[
            pl.BlockSpec((MOE_TM, D_MODEL), lambda t, te, nv: (t, 0)),
            pl.BlockSpec((None, D_MODEL, 2 * EXPERT_FF), lambda t, te, nv: (te[t], 0, 0)),
            pl.BlockSpec((None, 1, 2 * EXPERT_FF), lambda t, te, nv: (te[t], 0, 0)),
            pl.BlockSpec((None, EXPERT_FF, D_MODEL), lambda t, te, nv: (te[t], 0, 0)),
            pl.BlockSpec((None, 1, D_MODEL), lambda t, te, nv: (te[t], 0, 0)),
        ],
        out_specs=pl.BlockSpec((MOE_TM, D_MODEL), lambda t, te, nv: (t, 0)),
        scratch_shapes=[pltpu.VMEM((D_MODEL, 2 * EXPERT_FF), BF16),
                        pltpu.VMEM((EXPERT_FF, D_MODEL), BF16)],
    )
    return pl.pallas_call(
        _moe_kernel,
        out_shape=jax.ShapeDtypeStruct((rows, D_MODEL), BF16),
        grid_spec=grid_spec,
        compiler_params=_cparams(("arbitrary",)),
        name="moe",
    )(tile_expert, n_valid, x_sorted, wgu, bgu, wd, bd)


def _final_kernel(h1_ref, y0_ref, y1_ref, y2_ref, y3_ref, tw_ref, fg_ref, o_ref):
    tw = tw_ref[...]
    h = h1_ref[...]
    for k, y_ref in enumerate((y0_ref, y1_ref, y2_ref, y3_ref)):
        h = h + tw[:, k:k + 1] * y_ref[...].astype(F32)
    o_ref[...] = _rms_rows(h, fg_ref[...])


def _final(h1, y_tok, tw, fg, *, tm):
    n = h1.shape[0]
    steps = n // tm
    row = lambda w: pl.BlockSpec((tm, w), lambda i: (i, 0))
    y_spec = lambda k: pl.BlockSpec((tm, D_MODEL), lambda i: (k * steps + i, 0))
    return pl.pallas_call(
        _final_kernel,
        out_shape=jax.ShapeDtypeStruct((n, D_MODEL), F32),
        grid=(steps,),
        in_specs=[row(D_MODEL)] + [y_spec(k) for k in range(TOP_K)]
                 + [row(LANES), pl.BlockSpec((1, D_MODEL), lambda i: (0, 0))],
        out_specs=row(D_MODEL),
        compiler_params=_cparams(("parallel",)),
        name="final",
    )(h1, *([y_tok] * TOP_K), tw, fg)


def _split_w_in(w_in):
    offs = np.cumsum((0,) + IN_SPLITS)
    names = ("q", "k", "v", "qi", "ki", "wi", "z", "xs", "b", "c", "dt", "gate")
    return {nm: w_in[:, offs[i]:offs[i + 1]] for i, nm in enumerate(names)}


def _expert_dispatch(top_i, rank, hist, n_rows_pad):
    n = top_i.shape[0]
    pairs = n * TOP_K
    counts = jnp.sum(hist, axis=0)
    grp_start = jnp.cumsum(counts) - counts
    tiles_per_e = (counts + MOE_TM - 1) // MOE_TM
    tile_end = jnp.cumsum(tiles_per_e)
    row_start = (tile_end - tiles_per_e) * MOE_TM
    offset = row_start - grp_start
    base = (row_start[None, :] + jnp.cumsum(hist, axis=0) - hist).reshape(-1)
    tile_of_token = jnp.arange(n, dtype=I32) // (n // hist.shape[0])
    dest = base[tile_of_token[:, None] * N_EXPERTS + top_i] + rank
    token_of_pair = jnp.broadcast_to(jnp.arange(n, dtype=I32)[:, None], (n, TOP_K))
    _, order_tok = lax.sort_key_val(dest.reshape(-1), token_of_pair.reshape(-1))
    n_tiles = n_rows_pad // MOE_TM
    n_valid = tile_end[-1].astype(I32)
    tile_ids = jnp.minimum(jnp.arange(n_tiles, dtype=I32), n_valid - 1)
    tile_expert = jnp.searchsorted(tile_end, tile_ids, side="right",
                                   method="compare_all").astype(I32)
    tile_expert = jnp.minimum(tile_expert, N_EXPERTS - 1)
    rows = jnp.arange(n_rows_pad, dtype=I32)
    e_row = jnp.repeat(tile_expert, MOE_TM)
    s_row = rows - offset[e_row]
    live = (s_row >= grp_start[e_row]) & (s_row < (grp_start + counts)[e_row])
    src_token = jnp.where(live, order_tok[jnp.clip(s_row, 0, pairs - 1)], rows % n)
    return src_token, dest, tile_expert, n_valid.reshape(1)


def kernel(x, meta_tokens, norm1_g, w_in, conv_w, conv_b, dt_bias, a_log, d_skip, ssm_norm_g,
           w_o_attn, w_o_ssm, w_out, norm2_g, w_router, b_router, w_gate_up, b_gate_up,
           w_down, b_down, final_g):
    nb, seq, d = x.shape
    n = nb * seq
    assert d == D_MODEL and seq % TQ == 0 and w_in.shape[0] == 1
    xf = x.reshape(n, d)
    meta_pad = jnp.concatenate([jnp.zeros((META_PAD, d), x.dtype), meta_tokens.astype(x.dtype)], 0)
    g1 = norm1_g[0].reshape(1, d)

    w = _split_w_in(w_in[0])
    w_nat = jnp.concatenate([w["k"], w["z"], w["xs"], w["b"], w["c"], w["gate"]], 1).astype(BF16)
    w_qvt = jnp.concatenate([w["q"], w["v"]], 1).T.astype(BF16)
    zpad = lambda c: jnp.zeros((d, c), F32)
    w_sn = jnp.concatenate([w["ki"], zpad(SN_DT - IDX_DIM), w["dt"],
                            zpad(SN_W - SN_DT - SSM_HEADS)], 1).astype(BF16)
    w_st = jnp.concatenate([w["qi"], w["wi"], zpad(ST_W - ST_WI - IDX_HEADS)], 1).T.astype(BF16)

    tabs = _rope_tables(N_META + jnp.arange(seq, dtype=I32))
    tabs_t = tuple(t.T for t in tabs)
    pos_meta = jnp.maximum(jnp.arange(CHUNK, dtype=I32) - META_PAD, 0)
    tabs_m = _rope_tables(pos_meta)
    tabs_mt = tuple(t.T for t in tabs_m)

    tm_nat = 1024 if seq % 1024 == 0 else TQ
    w_t = jnp.concatenate([w_qvt, w_st], axis=0)
    q_slabs = ATTN_WIDTH // LANES
    tr_rope = tuple(range(q_slabs)) + tuple(
        2 * q_slabs + s for s in range(IDX_HEADS * IDX_DIM // LANES))
    nat, sm_n = _proj_nat(xf, g1, w_nat, w_sn, tabs, tm=tm_nat, tn=1024, rope_tiles=1,
                          seq_tiles=seq // tm_nat, name="proj_nat")
    qvt, sm_t = _proj_tr(xf, g1, w_t, tabs_t, tm=TQ, nb=nb, width_main=2 * ATTN_WIDTH,
                         rope_slabs=tr_rope, name="proj_t")
    nat_m, sm_n_m = _proj_nat(meta_pad, g1, w_nat, w_sn, tabs_m, tm=CHUNK, tn=1024, rope_tiles=1,
                              seq_tiles=1, name="proj_nat_meta")
    qvt_m, _ = _proj_tr(meta_pad, g1, w_t, tabs_mt, tm=CHUNK, nb=1, width_main=2 * ATTN_WIDTH,
                        rope_slabs=tr_rope, name="proj_t_meta")

    attn = _attention(qvt, nat, sm_t, sm_n, nat_m, qvt_m, nb=nb, seq=seq)

    dt_raw = sm_n[:, SN_DT:SN_DT + SSM_HEADS].reshape(nb, seq // CHUNK, CHUNK, SSM_GROUPS,
                                                      GROUP_HEADS)
    dtt = jnp.transpose(dt_raw, (3, 0, 1, 4, 2))
    dt_raw_m = sm_n_m[:, SN_DT:SN_DT + SSM_HEADS].reshape(CHUNK, SSM_GROUPS, GROUP_HEADS)
    dtt_m = jnp.transpose(dt_raw_m, (1, 2, 0))
    ssm = _ssd(nat, nat_m, dtt, dtt_m, conv_w[0], conv_b[0], dt_bias[0], a_log[0], d_skip[0],
               ssm_norm_g[0], nb=nb, seq=seq)

    wr = jnp.concatenate([w_router[0], jnp.zeros((d, LANES - N_EXPERTS), F32)], 1)
    br = jnp.concatenate([b_router[0], jnp.full((LANES - N_EXPERTS,), -1e30, F32)]).reshape(1, LANES)
    wr_hi = wr.astype(BF16)
    wr_lo = (wr - wr_hi.astype(F32)).astype(BF16)
    tm_merge = 512 if n % 512 == 0 else TQ
    h1, hn2, ti, tw, hist = _merge(attn, ssm, nat, xf, w_o_attn[0].astype(BF16),
                                   w_o_ssm[0].astype(BF16), w_out[0].astype(BF16),
                                   norm2_g[0].reshape(1, d), wr_hi, wr_lo, br, tm=tm_merge)

    n_tiles = n * TOP_K // MOE_TM + N_EXPERTS
    src_token, pos, tile_expert, n_valid = _expert_dispatch(
        ti[:, :TOP_K], ti[:, TOP_K:2 * TOP_K], hist[::8, :N_EXPERTS], n_tiles * MOE_TM)
    x_sorted = hn2.at[src_token].get(mode="promise_in_bounds")
    y_sorted = _moe(tile_expert, n_valid, x_sorted, w_gate_up[0],
                    b_gate_up[0].reshape(N_EXPERTS, 1, -1), w_down[0],
                    b_down[0].reshape(N_EXPERTS, 1, -1))
    y_tok = y_sorted.at[pos.T.reshape(-1)].get(mode="promise_in_bounds")

    out = _final(h1, y_tok, tw, final_g.reshape(1, d), tm=512 if n % 512 == 0 else TQ)
    return out.reshape(nb, seq, d)
```

```python
import functools
import math

import numpy as np
import jax
import jax.numpy as jnp
from jax import lax
from jax.experimental import pallas as pl
from jax.experimental.pallas import tpu as pltpu

F32 = jnp.float32
BF16 = jnp.bfloat16
I32 = jnp.int32
I16 = jnp.int16

D_MODEL = 1024
N_META = 16
RMS_EPS = 1e-6
N_HEADS = 16
HEAD_DIM = 64
ATTN_WIDTH = N_HEADS * HEAD_DIM
ROT_DIM = HEAD_DIM // 4
ROPE_THETA = 500000.0
IDX_HEADS = 8
IDX_DIM = 64
TOPK_KEYS_MAX = 256
SSM_INNER = 2 * D_MODEL
SSM_HEAD_DIM = 64
SSM_HEADS = SSM_INNER // SSM_HEAD_DIM
SSM_GROUPS = 4
SSM_STATE = 128
CONV_WIDTH = 4
CHUNK = 128
N_EXPERTS = 32
TOP_K = 4
EXPERT_FF = D_MODEL
SWIGLU_LIMIT = 7.0
SWIGLU_ALPHA = 1.702
IN_SPLITS = (ATTN_WIDTH, ATTN_WIDTH, ATTN_WIDTH, IDX_HEADS * IDX_DIM, IDX_DIM, IDX_HEADS,
             SSM_INNER, SSM_INNER, SSM_GROUPS * SSM_STATE, SSM_GROUPS * SSM_STATE, SSM_HEADS,
             2 * D_MODEL)

LANES = 128
GROUP_CH = SSM_INNER // SSM_GROUPS
GROUP_HEADS = SSM_HEADS // SSM_GROUPS
META_PAD = CHUNK - N_META
TQ = 256
FLASH_LAG = 8
MOE_TM = 512
INT_MIN = -2147483648
VMEM_LIMIT = 56 * 1024 * 1024

NAT_K, NAT_Z, NAT_XS, NAT_B, NAT_C, NAT_G = 0, 1024, 3072, 5120, 5632, 6144
NAT_W = 8192
SN_KI, SN_DT, SN_W = 0, 128, 256
ST_QI, ST_WI, ST_W = 0, 512, 640


def _cparams(sem):
    return pltpu.CompilerParams(dimension_semantics=sem, vmem_limit_bytes=VMEM_LIMIT)


def _rms_rows(x, g):
    ms = jnp.mean(x * x, axis=-1, keepdims=True)
    return x * lax.rsqrt(ms + RMS_EPS) * g


def _proj_nat_kernel(x_ref, g_ref, w_ref, wsm_ref, cos_ref, sa_ref, sb_ref, o_ref, osm_ref,
                     hn_ref, *, rope_tiles):
    j = pl.program_id(1)

    def rope(a):
        return (a * cos_ref[...] + pltpu.roll(a, 8, 1) * sa_ref[...]
                + pltpu.roll(a, LANES - 8, 1) * sb_ref[...])

    @pl.when(j == 0)
    def _():
        hn = _rms_rows(x_ref[...], g_ref[...]).astype(hn_ref.dtype)
        hn_ref[...] = hn
        small = jnp.dot(hn, wsm_ref[...], preferred_element_type=F32)
        osm_ref[:, 0:LANES] = rope(small[:, 0:LANES])
        osm_ref[:, LANES:] = small[:, LANES:]

    acc = jnp.dot(hn_ref[...], w_ref[...], preferred_element_type=F32)

    @pl.when(j < rope_tiles)
    def _():
        for s in range(acc.shape[1] // LANES):
            cols = slice(s * LANES, (s + 1) * LANES)
            o_ref[:, cols] = rope(acc[:, cols]).astype(o_ref.dtype)

    @pl.when(j >= rope_tiles)
    def _():
        o_ref[...] = acc.astype(o_ref.dtype)


def _proj_nat(x, g, w, w_small, tabs, *, tm, tn, rope_tiles, seq_tiles, name):
    n, d = x.shape
    width, width_sm = w.shape[1], w_small.shape[1]
    cos, sa, sb = tabs
    tab_spec = pl.BlockSpec((tm, LANES), lambda i, j: (i % seq_tiles, 0))
    return pl.pallas_call(
        functools.partial(_proj_nat_kernel, rope_tiles=rope_tiles),
        out_shape=(jax.ShapeDtypeStruct((n, width), BF16),
                   jax.ShapeDtypeStruct((n, width_sm), F32)),
        grid=(n // tm, width // tn),
        in_specs=[
            pl.BlockSpec((tm, d), lambda i, j: (i, 0)),
            pl.BlockSpec((1, d), lambda i, j: (0, 0)),
            pl.BlockSpec((d, tn), lambda i, j: (0, j)),
            pl.BlockSpec((d, width_sm), lambda i, j: (0, 0)),
            tab_spec, tab_spec, tab_spec,
        ],
        out_specs=(pl.BlockSpec((tm, tn), lambda i, j: (i, j)),
                   pl.BlockSpec((tm, width_sm), lambda i, j: (i, 0))),
        scratch_shapes=[pltpu.VMEM((tm, d), BF16)],
        compiler_params=_cparams(("parallel", "arbitrary")),
        name=name,
    )(x, g, w, w_small, cos, sa, sb)


def _proj_tr_kernel(x_ref, g_ref, wt_ref, cos_ref, sa_ref, sb_ref, o_ref, osm_ref, *,
                    rope_slabs):
    hn = _rms_rows(x_ref[...], g_ref[...]).astype(BF16)
    acc = lax.dot_general(wt_ref[...], hn, (((1,), (1,)), ((), ())),
                          preferred_element_type=F32)
    cos, sa, sb = cos_ref[...], sa_ref[...], sb_ref[...]
    n_main = o_ref.shape[0] // LANES
    for s in range(acc.shape[0] // LANES):
        a = acc[s * LANES:(s + 1) * LANES, :]
        if s in rope_slabs:
            a = a * cos + pltpu.roll(a, 8, 0) * sa + pltpu.roll(a, LANES - 8, 0) * sb
        if s < n_main:
            o_ref[s * LANES:(s + 1) * LANES, :] = a.astype(o_ref.dtype)
        else:
            osm_ref[(s - n_main) * LANES:(s - n_main + 1) * LANES, :] = a


def _proj_tr(x, g, wt, tabs_t, *, tm, nb, width_main, rope_slabs, name):
    n, d = x.shape
    width = wt.shape[0]
    width_sm = width - width_main
    seq_tiles = n // nb // tm
    cos, sa, sb = tabs_t
    tab_spec = pl.BlockSpec((LANES, tm), lambda i: (0, i % seq_tiles))
    out_map = lambda i: (i // seq_tiles, i % seq_tiles, 0, 0)
    return pl.pallas_call(
        functools.partial(_proj_tr_kernel, rope_slabs=rope_slabs),
        out_shape=(jax.ShapeDtypeStruct((nb, seq_tiles, width_main, tm), BF16),
                   jax.ShapeDtypeStruct((nb, seq_tiles, width_sm, tm), F32)),
        grid=(n // tm,),
        in_specs=[
            pl.BlockSpec((tm, d), lambda i: (i, 0)),
            pl.BlockSpec((1, d), lambda i: (0, 0)),
            pl.BlockSpec((width, d), lambda i: (0, 0)),
            tab_spec, tab_spec, tab_spec,
        ],
        out_specs=(pl.BlockSpec((None, None, width_main, tm), out_map),
                   pl.BlockSpec((None, None, width_sm, tm), out_map)),
        compiler_params=_cparams(("parallel",)),
        name=name,
    )(x, g, wt, cos, sa, sb)


def _rope_tables(pos):
    half = ROT_DIM // 2
    inv_freq = jnp.exp(-math.log(ROPE_THETA) * jnp.arange(half, dtype=F32) / half)
    ang = pos.astype(F32)[:, None] * inv_freq[None, :]
    cos8, sin8 = jnp.cos(ang), jnp.sin(ang)
    t = pos.shape[0]
    cc = np.arange(LANES) % HEAD_DIM
    in_rot = jnp.asarray(cc < ROT_DIM)
    lo = jnp.asarray(cc < half)
    hi = jnp.asarray((cc >= half) & (cc < ROT_DIM))
    idx = jnp.asarray(cc % half)
    cos_l = jnp.take(cos8, idx, axis=1)
    sin_l = jnp.take(sin8, idx, axis=1)
    cos = jnp.where(in_rot[None, :], cos_l, 1.0)
    sa = jnp.where(hi[None, :], sin_l, 0.0)
    sb = jnp.where(lo[None, :], -sin_l, 0.0)
    del t
    return cos, sa, sb


def _attn_kernel(qt_ref, vt_ref, k_ref, qit_ref, wit_ref, ki_ref, kmeta_ref, vtmeta_ref,
                 o_ref, keys_ref, hi_ref, lo_ref, bias_ref, qm_ref, m_ref, l_ref, acc_ref, s_ref,
                 *, ksel, nbits):
    j = pl.program_id(1)
    tq = TQ
    nch = j + 1
    int_min = jnp.int32(INT_MIN)
    neg_inf = jnp.float32(-jnp.inf)

    row128 = lax.broadcasted_iota(I32, (LANES, tq), 0)
    for h in range(N_HEADS):
        slab = qt_ref[(h // 2) * LANES:(h // 2 + 1) * LANES, :]
        keep = (row128 < HEAD_DIM) if h % 2 == 0 else (row128 >= HEAD_DIM)
        scaled = slab.astype(F32) * (HEAD_DIM ** -0.5 * math.log2(math.e))
        qm_ref[h] = jnp.where(keep, scaled, 0.0).astype(qm_ref.dtype)

    w_all = wit_ref[0:IDX_HEADS, :] * (IDX_HEADS ** -0.5 * IDX_DIM ** -0.5)
    qi_bf = [qit_ref[h * IDX_DIM:(h + 1) * IDX_DIM, :].astype(BF16) for h in range(IDX_HEADS)]
    qidx = j * tq + lax.broadcasted_iota(I32, (tq, tq), 1)
    krow = lax.broadcasted_iota(I32, (tq, tq), 0)

    def score_body(c, carry):
        start = pl.multiple_of(c * tq, tq)
        kic = ki_ref[pl.ds(start, tq), :][:, :IDX_DIM].astype(BF16)
        sc = jnp.zeros((tq, tq), F32)
        for h in range(IDX_HEADS):
            lg = jnp.dot(kic, qi_bf[h], preferred_element_type=F32)
            sc = sc + jnp.maximum(lg, 0.0) * w_all[h:h + 1, :]
        bits = pltpu.bitcast(sc, I32)
        skey = jnp.where(bits < 0, bits ^ jnp.int32(0x7FFFFFFF), bits)
        skey = jnp.where(c * tq + krow <= qidx, skey, int_min)
        keys_ref[c] = skey
        hi_ref[c] = (skey >> 16).astype(I16)
        lo_ref[c] = ((skey & 0xFFFF) - 32768).astype(I16)
        return carry

    lax.fori_loop(0, nch, score_body, 0)

    @pl.when(nch % 2 == 1)
    def _():
        hi_ref[nch] = jnp.full((tq, tq), -32768, I16)
        lo_ref[nch] = jnp.full((tq, tq), -32768, I16)

    npair = (nch + 1) // 2

    def count(pred):
        def body(c, cnt):
            hit = pred(keys_ref[c], c).astype(I32)
            return cnt + hit.reshape(tq // 8, 8, tq).sum(axis=0)
        cnt8 = lax.fori_loop(0, nch, body, jnp.zeros((8, tq), I32))
        return jnp.sum(cnt8, axis=0, keepdims=True)

    def count16(ref, cand):
        cand16 = cand.astype(I16)
        one, zero = jnp.int16(1), jnp.int16(0)

        def body(i, cnt):
            for c in (2 * i, 2 * i + 1):
                hit = pltpu.bitcast(jnp.where(ref[c] >= cand16, one, zero), I32)
                cnt = cnt + hit.reshape(tq // 16, 8, tq).sum(axis=0)
            return cnt

        cnt8 = lax.fori_loop(0, npair, body, jnp.zeros((8, tq), I32))
        both = jnp.sum(cnt8, axis=0, keepdims=True)
        return (both & 0xFFFF) + lax.shift_right_logical(both, 16)

    def search16(ref, want):
        def bit_body(it, t_b):
            cand_b = t_b | lax.shift_left(jnp.int32(1), 15 - it)
            cnt = count16(ref, cand_b - 32768)
            return jnp.where(cnt >= want, cand_b, t_b)
        return lax.fori_loop(0, 16, bit_body, jnp.zeros((1, tq), I32))

    hi_b = search16(hi_ref, ksel)
    thr_hi = hi_b - 32768
    n_above = count16(hi_ref, jnp.minimum(thr_hi + 1, 32767))
    n_above = jnp.where(thr_hi >= 32767, 0, n_above)
    thr_hi16 = thr_hi.astype(I16)

    def bucket_body(c, carry):
        lo_ref[c] = jnp.where(hi_ref[c] == thr_hi16, lo_ref[c], jnp.int16(-32768))
        return carry

    lax.fori_loop(0, 2 * npair, bucket_body, 0)
    lo_b = search16(lo_ref, ksel - n_above)
    thr = lax.shift_left(thr_hi, 16) | lo_b
    n_gt = count(lambda kc, c: kc > thr)
    n_eq = count(lambda kc, c: kc == thr)
    need = ksel - n_gt
    fix = (n_eq > need) & (thr != int_min)
    thr_eff = jnp.maximum(thr, int_min + 1)

    def bias_body(c, carry):
        bias_ref[c] = jnp.where(keys_ref[c] >= thr_eff, 0.0, neg_inf)
        return carry

    lax.fori_loop(0, nch, bias_body, 0)

    @pl.when(jnp.max(fix.astype(I32)) > 0)
    def _():
        def idx_body(it, v):
            cand = v | lax.shift_left(jnp.int32(1), nbits - 1 - it)
            cnt = count(lambda kc, c: jnp.where(c * tq + krow < cand, kc, int_min) == thr)
            return jnp.where(cnt <= need - 1, cand, v)

        last = lax.fori_loop(0, nbits, idx_body, jnp.zeros((1, tq), I32))

        def fix_body(c, carry):
            kc = keys_ref[c]
            tie_lim = jnp.where(fix, last, jnp.int32(2 ** 30))
            tie_ok = jnp.where(c * tq + krow <= tie_lim, thr_eff, thr_eff + 1)
            bias_ref[c] = jnp.where(kc >= tie_ok, 0.0, neg_inf)
            return carry

        lax.fori_loop(0, nch, fix_body, 0)

    def chunk_step(kslab_of, vt_of, bias_c, n_keys, first):
        alphas = []
        ones = jnp.ones((16, n_keys), BF16)

        def scores(h):
            s = jnp.dot(kslab_of(h), qm_ref[h], preferred_element_type=F32) + bias_c
            s_ref[h, 0:n_keys, :] = s
            smax = jnp.max(s, axis=0, keepdims=True)
            if first:
                m_ref[h:h + 1, :] = smax
            else:
                m_old = m_ref[h:h + 1, :]
                m_new = jnp.maximum(m_old, smax)
                alphas.append(jnp.exp2(m_old - m_new))
                m_ref[h:h + 1, :] = m_new

        def values(h):
            p = jnp.exp2(s_ref[h, 0:n_keys, :] - m_ref[h:h + 1, :]).astype(BF16)
            res = jnp.dot(jnp.concatenate([vt_of(h), ones], axis=0), p,
                          preferred_element_type=F32)
            pv, psum = res[:HEAD_DIM, :], res[HEAD_DIM:HEAD_DIM + 1, :]
            rows = slice(h * HEAD_DIM, (h + 1) * HEAD_DIM)
            if first:
                l_ref[h:h + 1, :] = psum
                acc_ref[rows, :] = pv
            else:
                l_ref[h:h + 1, :] = l_ref[h:h + 1, :] * alphas[h] + psum
                acc_ref[rows, :] = acc_ref[rows, :] * alphas[h] + pv

        for h in range(N_HEADS + FLASH_LAG):
            if h < N_HEADS:
                scores(h)
            if h >= FLASH_LAG:
                values(h - FLASH_LAG)

    meta_row = lax.broadcasted_iota(I32, (CHUNK, tq), 0)
    bias_meta = jnp.where(meta_row >= META_PAD, 0.0, neg_inf)
    chunk_step(lambda h: kmeta_ref[:, (h // 2) * LANES:(h // 2 + 1) * LANES],
               lambda h: vtmeta_ref[h * HEAD_DIM:(h + 1) * HEAD_DIM, :], bias_meta, CHUNK, True)

    def flash_body(c, carry):
        start = pl.multiple_of(c * tq, tq)
        chunk_step(lambda h: k_ref[pl.ds(start, tq), (h // 2) * LANES:(h // 2 + 1) * LANES],
                   lambda h: vt_ref[c, h * HEAD_DIM:(h + 1) * HEAD_DIM, :], bias_ref[c], tq, False)
        return carry

    lax.fori_loop(0, nch, flash_body, 0)

    for h in range(N_HEADS):
        rows = slice(h * HEAD_DIM, (h + 1) * HEAD_DIM)
        acc_ref[rows, :] = acc_ref[rows, :] * (1.0 / l_ref[h:h + 1, :])
    o_ref[...] = acc_ref[...].T.astype(o_ref.dtype)


def _attention(qvt, nat, sm_t, sm_n, nat_meta, qvt_meta, *, nb, seq):
    nq = seq // TQ
    ksel = min(TOPK_KEYS_MAX, seq // 4)
    nbits = max(1, int(math.ceil(math.log2(seq))))
    kernel = functools.partial(_attn_kernel, ksel=ksel, nbits=nbits)
    return pl.pallas_call(
        kernel,
        out_shape=jax.ShapeDtypeStruct((nb * seq, ATTN_WIDTH), BF16),
        grid=(nb, nq),
        in_specs=[
            pl.BlockSpec((None, None, ATTN_WIDTH, TQ), lambda b, j: (b, j, 0, 0)),
            pl.BlockSpec((None, nq, ATTN_WIDTH, TQ), lambda b, j: (b, 0, 1, 0)),
            pl.BlockSpec((seq, ATTN_WIDTH), lambda b, j: (b, NAT_K // ATTN_WIDTH)),
            pl.BlockSpec((None, None, IDX_HEADS * IDX_DIM, TQ), lambda b, j: (b, j, 0, 0)),
            pl.BlockSpec((None, None, LANES, TQ), lambda b, j: (b, j, ST_WI // LANES, 0)),
            pl.BlockSpec((seq, LANES), lambda b, j: (b, SN_KI // LANES)),
            pl.BlockSpec((CHUNK, ATTN_WIDTH), lambda b, j: (0, NAT_K // ATTN_WIDTH)),
            pl.BlockSpec((None, None, ATTN_WIDTH, CHUNK), lambda b, j: (0, 0, 1, 0)),
        ],
        out_specs=pl.BlockSpec((TQ, ATTN_WIDTH), lambda b, j: (b * nq + j, 0)),
        scratch_shapes=[
            pltpu.VMEM((nq, TQ, TQ), I32),
            pltpu.VMEM((nq + nq % 2, TQ, TQ), I16),
            pltpu.VMEM((nq + nq % 2, TQ, TQ), I16),
            pltpu.VMEM((nq, TQ, TQ), F32),
            pltpu.VMEM((N_HEADS, LANES, TQ), BF16),
            pltpu.VMEM((N_HEADS, TQ), F32),
            pltpu.VMEM((N_HEADS, TQ), F32),
            pltpu.VMEM((ATTN_WIDTH, TQ), F32),
            pltpu.VMEM((N_HEADS, TQ, TQ), F32),
        ],
        compiler_params=_cparams(("parallel", "arbitrary")),
        name="attn",
    )(qvt, qvt, nat, sm_t, sm_t, sm_n, nat_meta, qvt_meta)


def _softplus(x):
    return jnp.maximum(x, 0.0) + jnp.log1p(jnp.exp(-jnp.abs(x)))


def _sigmoid(x):
    return 0.5 + 0.5 * jnp.tanh(0.5 * x)


def _silu(x):
    h = 0.5 * x
    return h + h * jnp.tanh(h)


def _pad_transpose(rows):
    padded = jnp.concatenate([rows, jnp.zeros((LANES - rows.shape[0], LANES), rows.dtype)], axis=0)
    return padded.T


STAGE_TAIL = 16
SSD_GROUPS_PER_STEP = 2


def _ssd_kernel(xs_ref, z_ref, b_ref, c_ref, dtt_ref,
                xs_m_ref, b_m_ref, c_m_ref, dtt_m_ref,
                cw_ref, cb_ref, dtb_ref, alog_ref, dsk_ref, ng_ref, exp_ref, tri_ref, shift_ref,
                o_ref, state_ref, *, n_chunks):
    gs = range(SSD_GROUPS_PER_STEP)
    a_neg = [-jnp.exp(alog_ref[g]) for g in gs]
    dt_bias = [dtb_ref[g] for g in gs]
    expand = exp_ref[...]
    tri_u = tri_ref[...]
    li = lax.broadcasted_iota(I32, (CHUNK, CHUNK), 0)
    si = lax.broadcasted_iota(I32, (CHUNK, CHUNK), 1)
    tril = si <= li
    lane = lax.broadcasted_iota(I32, (CHUNK, LANES), 1)
    low_half = lane < SSM_HEAD_DIM
    neg_inf = jnp.float32(-jnp.inf)

    xb, xc = GROUP_CH, GROUP_CH + SSM_STATE

    widths = (GROUP_CH, SSM_STATE, SSM_STATE)
    gcols = lambda g: slice(g * GROUP_CH, (g + 1) * GROUP_CH)

    def pieces(refs, rows, g):
        return [r[rows, g * w:(g + 1) * w] for r, w in zip(refs, widths)]

    def chunk(src, rows, tail_src, trows, dt_of, out_rows, is_meta):
        stacks = []
        for g in gs:
            tail = ([jnp.zeros((STAGE_TAIL, w), BF16) for w in widths] if tail_src is None
                    else pieces(tail_src, trows, g))
            stacks.append(jnp.concatenate(
                [jnp.concatenate(tail, axis=1), jnp.concatenate(pieces(src, rows, g), axis=1)],
                axis=0))
        shifted = [jnp.dot(shift_ref[...], stacks[g], preferred_element_type=F32) for g in gs]
        x, bm_bf, cm_bf = [], [], []
        for g in gs:
            conv = cb_ref[g]
            for k in range(CONV_WIDTH):
                conv = conv + (shifted[g][k * CHUNK:(k + 1) * CHUNK, :]
                               * cw_ref[g, CONV_WIDTH - 1 - k:CONV_WIDTH - k, :])
            act = _silu(conv)
            x.append(act[:, 0:xb])
            bm_bf.append(act[:, xb:xc].astype(BF16))
            cm_bf.append(act[:, xc:].astype(BF16))
        dt_r = []
        for g in gs:
            d = _softplus(dt_of(g) + dt_bias[g])
            if is_meta:
                d = jnp.where(lax.broadcasted_iota(I32, d.shape, 1) >= META_PAD, d, 0.0)
            dt_r.append(d)
        cs_r = [jnp.dot(dt_r[g] * a_neg[g], tri_u, precision=lax.Precision.HIGHEST,
                        preferred_element_type=F32) for g in gs]
        dt_c = [_pad_transpose(dt_r[g]) for g in gs]
        cs_c = [_pad_transpose(cs_r[g]) for g in gs]
        dec_c = [jnp.exp(cs_c[g][CHUNK - 1:CHUNK, :] - cs_c[g]) for g in gs]
        ecs_c = [jnp.exp(cs_c[g]) for g in gs]
        dtx = [jnp.dot(dt_c[g].astype(BF16), expand, preferred_element_type=F32) for g in gs]
        decx = [jnp.dot((dt_c[g] * dec_c[g]).astype(BF16), expand, preferred_element_type=F32)
                for g in gs]
        ecsx = [jnp.dot(ecs_c[g].astype(BF16), expand, preferred_element_type=F32) for g in gs]
        state = [state_ref[g] for g in gs]
        new_part = [lax.dot_general(bm_bf[g], (x[g] * decx[g]).astype(BF16),
                                    (((0,), (0,)), ((), ())), preferred_element_type=F32)
                    for g in gs]
        if not is_meta:
            xdt = [(x[g] * dtx[g]).astype(BF16) for g in gs]
            cb = [lax.dot_general(cm_bf[g], bm_bf[g], (((1,), (1,)), ((), ())),
                                  preferred_element_type=F32) for g in gs]
            y_off = [jnp.dot(cm_bf[g], state[g].astype(BF16), preferred_element_type=F32)
                     * ecsx[g] for g in gs]
            slabs = [[] for _ in gs]
            for pp in range(GROUP_HEADS // 2):
                halves = [[] for _ in gs]
                for e in (2 * pp, 2 * pp + 1):
                    for g in gs:
                        diff = cs_c[g][:, e:e + 1] - cs_r[g][e:e + 1, :]
                        lmat = jnp.exp(jnp.where(tril, diff, neg_inf))
                        halves[g].append(jnp.dot((cb[g] * lmat).astype(BF16),
                                                 xdt[g][:, pp * LANES:(pp + 1) * LANES],
                                                 preferred_element_type=F32))
                for g in gs:
                    slabs[g].append(jnp.where(low_half, halves[g][0], halves[g][1]))
            for g in gs:
                y = jnp.concatenate(slabs[g], axis=1) + y_off[g] + x[g] * dsk_ref[:, gcols(g)]
                y = y * _silu(z_ref[rows, gcols(g)].astype(F32))
                ms = jnp.mean(y * y, axis=-1, keepdims=True)
                o_ref[out_rows, gcols(g)] = (y * lax.rsqrt(ms + RMS_EPS)
                                             * ng_ref[:, gcols(g)]).astype(o_ref.dtype)
        for g in gs:
            state_ref[g] = state[g] * ecsx[g][CHUNK - 1:CHUNK, :] + new_part[g]

    state_ref[...] = jnp.zeros_like(state_ref)
    real, meta = (xs_ref, b_ref, c_ref), (xs_m_ref, b_m_ref, c_m_ref)
    all_rows = slice(0, CHUNK)
    chunk(meta, all_rows, None, None, lambda g: dtt_m_ref[g], None, True)
    chunk(real, all_rows, meta, slice(CHUNK - STAGE_TAIL, CHUNK), lambda g: dtt_ref[g, 0],
          all_rows, False)

    def body(ci, carry):
        rows = pl.ds(pl.multiple_of(ci * CHUNK, CHUNK), CHUNK)
        trows = pl.ds(pl.multiple_of(ci * CHUNK - STAGE_TAIL, STAGE_TAIL), STAGE_TAIL)
        chunk(real, rows, real, trows, lambda g: dtt_ref[g, ci], rows, False)
        return carry

    lax.fori_loop(1, n_chunks, body, 0)


def _ssd(nat, nat_meta, dtt, dtt_meta, conv_w, conv_b, dt_bias, a_log, d_skip, ssm_norm_g,
         *, nb, seq):
    gch, st = GROUP_CH, SSM_STATE
    expand = np.zeros((LANES, gch), np.float32)
    for e in range(GROUP_HEADS):
        expand[e, e * SSM_HEAD_DIM:(e + 1) * SSM_HEAD_DIM] = 1.0
    tri = np.triu(np.ones((CHUNK, CHUNK), np.float32))
    def group_cols(p):
        px = p[:, :SSM_INNER].reshape(-1, SSM_GROUPS, gch)
        pb = p[:, SSM_INNER:SSM_INNER + SSM_GROUPS * st].reshape(-1, SSM_GROUPS, st)
        pc = p[:, SSM_INNER + SSM_GROUPS * st:].reshape(-1, SSM_GROUPS, st)
        return jnp.transpose(jnp.concatenate([px, pb, pc], axis=2), (1, 0, 2))

    cw = group_cols(conv_w)
    cb = group_cols(conv_b.reshape(1, -1))
    stage_w = gch + 2 * st
    shift = np.zeros((CONV_WIDTH * CHUNK, STAGE_TAIL + CHUNK), np.float32)
    for k in range(CONV_WIDTH):
        shift[k * CHUNK + np.arange(CHUNK), STAGE_TAIL + np.arange(CHUNK) - k] = 1.0
    dsk = jnp.repeat(d_skip, SSM_HEAD_DIM).reshape(1, SSM_INNER)
    ng = ssm_norm_g.reshape(1, SSM_INNER)
    dtb = dt_bias.reshape(SSM_GROUPS, GROUP_HEADS, 1)
    alog = a_log.reshape(SSM_GROUPS, GROUP_HEADS, 1)
    gps = SSD_GROUPS_PER_STEP
    xs_blk, z_blk = NAT_XS // (gps * gch), NAT_Z // (gps * gch)
    b_blk, c_blk = NAT_B // (gps * st), NAT_C // (gps * st)
    return pl.pallas_call(
        functools.partial(_ssd_kernel, n_chunks=seq // CHUNK),
        out_shape=jax.ShapeDtypeStruct((nb * seq, SSM_INNER), BF16),
        grid=(nb, SSM_GROUPS // gps),
        in_specs=[
            pl.BlockSpec((seq, gps * gch), lambda b, g: (b, xs_blk + g)),
            pl.BlockSpec((seq, gps * gch), lambda b, g: (b, z_blk + g)),
            pl.BlockSpec((seq, gps * st), lambda b, g: (b, b_blk + g)),
            pl.BlockSpec((seq, gps * st), lambda b, g: (b, c_blk + g)),
            pl.BlockSpec((gps, None, seq // CHUNK, GROUP_HEADS, CHUNK),
                         lambda b, g: (g, b, 0, 0, 0)),
            pl.BlockSpec((CHUNK, gps * gch), lambda b, g: (0, xs_blk + g)),
            pl.BlockSpec((CHUNK, gps * st), lambda b, g: (0, b_blk + g)),
            pl.BlockSpec((CHUNK, gps * st), lambda b, g: (0, c_blk + g)),
            pl.BlockSpec((gps, GROUP_HEADS, CHUNK), lambda b, g: (g, 0, 0)),
            pl.BlockSpec((gps, CONV_WIDTH, stage_w), lambda b, g: (g, 0, 0)),
            pl.BlockSpec((gps, 1, stage_w), lambda b, g: (g, 0, 0)),
            pl.BlockSpec((gps, GROUP_HEADS, 1), lambda b, g: (g, 0, 0)),
            pl.BlockSpec((gps, GROUP_HEADS, 1), lambda b, g: (g, 0, 0)),
            pl.BlockSpec((1, gps * gch), lambda b, g: (0, g)),
            pl.BlockSpec((1, gps * gch), lambda b, g: (0, g)),
            pl.BlockSpec((LANES, gch), lambda b, g: (0, 0)),
            pl.BlockSpec((CHUNK, CHUNK), lambda b, g: (0, 0)),
            pl.BlockSpec((CONV_WIDTH * CHUNK, STAGE_TAIL + CHUNK), lambda b, g: (0, 0)),
        ],
        out_specs=pl.BlockSpec((seq, gps * gch), lambda b, g: (b, g)),
        scratch_shapes=[pltpu.VMEM((gps, SSM_STATE, gch), F32)],
        compiler_params=_cparams(("parallel", "arbitrary")),
        name="ssd",
    )(nat, nat, nat, nat, dtt, nat_meta, nat_meta, nat_meta, dtt_meta,
      cw, cb, dtb, alog, dsk, ng,
      jnp.asarray(expand, BF16), jnp.asarray(tri), jnp.asarray(shift, BF16))


def _merge_kernel(attn_ref, ssm_ref, g0_ref, g1_ref, x_ref, woa_ref, wos_ref, wout_ref,
                  n2g_ref, wrh_ref, wrl_ref, br_ref, ltri_ref,
                  h1_ref, hn2_ref, ti_ref, tw_ref, hist_ref):
    a = jnp.dot(attn_ref[...], woa_ref[...], preferred_element_type=F32)
    s = jnp.dot(ssm_ref[...], wos_ref[...], preferred_element_type=F32)
    g0 = _sigmoid(g0_ref[...].astype(F32))
    g1 = _sigmoid(g1_ref[...].astype(F32))
    u = g0 * a + g1 * s
    h1 = x_ref[...] + jnp.dot(u.astype(BF16), wout_ref[...], preferred_element_type=F32)
    h1_ref[...] = h1
    hn2 = _rms_rows(h1, n2g_ref[...])
    hn2_hi = hn2.astype(BF16)
    hn2_ref[...] = hn2_hi
    hn2_lo = (hn2 - hn2_hi.astype(F32)).astype(BF16)
    logits = (jnp.dot(hn2_hi, wrh_ref[...], preferred_element_type=F32)
              + jnp.dot(hn2_lo, wrh_ref[...], preferred_element_type=F32)
              + jnp.dot(hn2_hi, wrl_ref[...], preferred_element_type=F32)
              + br_ref[...])
    lane = lax.broadcasted_iota(I32, logits.shape, 1)
    work = logits
    ids, vals = [], []
    for _ in range(TOP_K):
        mx = jnp.max(work, axis=1, keepdims=True)
        idx = jnp.min(jnp.where(work == mx, lane, LANES), axis=1, keepdims=True)
        ids.append(idx)
        vals.append(mx)
        work = jnp.where(lane == idx, -jnp.inf, work)
    es = [jnp.exp(v - vals[0]) for v in vals]
    den = es[0] + es[1] + es[2] + es[3]
    routed = jnp.zeros(logits.shape, F32)
    for k in range(TOP_K):
        routed = jnp.where(lane == ids[k], 1.0, routed)
    before = jnp.dot(ltri_ref[...], routed.astype(BF16), preferred_element_type=F32)
    hist_ref[...] = jnp.broadcast_to(jnp.sum(routed, axis=0, keepdims=True),
                                     hist_ref.shape).astype(I32)
    ti = jnp.zeros(logits.shape, I32)
    tw = jnp.zeros(logits.shape, F32)
    for k in range(TOP_K):
        rank = jnp.sum(jnp.where(lane == ids[k], before, 0.0), axis=1, keepdims=True)
        ti = jnp.where(lane == k, ids[k], ti)
        ti = jnp.where(lane == TOP_K + k, rank.astype(I32), ti)
        tw = jnp.where(lane == k, es[k] / den, tw)
    ti_ref[...] = ti
    tw_ref[...] = tw


def _merge(attn, ssm, nat, x, woa, wos, wout, n2g, wr_hi, wr_lo, br, *, tm):
    n = x.shape[0]
    gblk = NAT_G // D_MODEL
    const = lambda shape: pl.BlockSpec(shape, lambda i: (0, 0))
    ltri = jnp.asarray(np.tril(np.ones((tm, tm), np.float32), -1), BF16)
    return pl.pallas_call(
        _merge_kernel,
        out_shape=(jax.ShapeDtypeStruct((n, D_MODEL), F32),
                   jax.ShapeDtypeStruct((n, D_MODEL), BF16),
                   jax.ShapeDtypeStruct((n, LANES), I32),
                   jax.ShapeDtypeStruct((n, LANES), F32),
                   jax.ShapeDtypeStruct((n // tm * 8, LANES), I32)),
        grid=(n // tm,),
        in_specs=[
            pl.BlockSpec((tm, ATTN_WIDTH), lambda i: (i, 0)),
            pl.BlockSpec((tm, SSM_INNER), lambda i: (i, 0)),
            pl.BlockSpec((tm, D_MODEL), lambda i: (i, gblk)),
            pl.BlockSpec((tm, D_MODEL), lambda i: (i, gblk + 1)),
            pl.BlockSpec((tm, D_MODEL), lambda i: (i, 0)),
            const((ATTN_WIDTH, D_MODEL)), const((SSM_INNER, D_MODEL)), const((D_MODEL, D_MODEL)),
            const((1, D_MODEL)), const((D_MODEL, LANES)), const((D_MODEL, LANES)),
            const((1, LANES)), const((tm, tm)),
        ],
        out_specs=(pl.BlockSpec((tm, D_MODEL), lambda i: (i, 0)),
                   pl.BlockSpec((tm, D_MODEL), lambda i: (i, 0)),
                   pl.BlockSpec((tm, LANES), lambda i: (i, 0)),
                   pl.BlockSpec((tm, LANES), lambda i: (i, 0)),
                   pl.BlockSpec((8, LANES), lambda i: (i, 0))),
        compiler_params=_cparams(("parallel",)),
        name="merge",
    )(attn, ssm, nat, nat, x, woa, wos, wout, n2g, wr_hi, wr_lo, br, ltri)


def _moe_kernel(te_ref, nv_ref, x_ref, wgu_ref, bgu_ref, wd_ref, bd_ref, o_ref,
                wgu_bf_ref, wd_bf_ref):
    t = pl.program_id(0)
    live = t < nv_ref[0]

    @pl.when(live & ((t == 0) | (te_ref[t] != te_ref[jnp.maximum(t - 1, 0)])))
    def _():
        wgu_bf_ref[...] = wgu_ref[...].astype(BF16)
        wd_bf_ref[...] = wd_ref[...].astype(BF16)

    @pl.when(live)
    def _():
        gu = jnp.dot(x_ref[...], wgu_bf_ref[...], preferred_element_type=F32) + bgu_ref[...]
        gate = jnp.minimum(gu[:, :EXPERT_FF], SWIGLU_LIMIT)
        up = jnp.clip(gu[:, EXPERT_FF:], -SWIGLU_LIMIT, SWIGLU_LIMIT)
        act = (up + 1.0) * (gate * _sigmoid(SWIGLU_ALPHA * gate))
        y = jnp.dot(act.astype(BF16), wd_bf_ref[...], preferred_element_type=F32) + bd_ref[...]
        o_ref[...] = y.astype(o_ref.dtype)

    @pl.when(t >= nv_ref[0])
    def _():
        o_ref[...] = jnp.zeros_like(o_ref)


def _moe(tile_expert, n_valid, x_sorted, wgu, bgu, wd, bd):
    rows = x_sorted.shape[0]
    n_tiles = rows // MOE_TM
    grid_spec = pltpu.PrefetchScalarGridSpec(
        num_scalar_prefetch=2,
        grid=(n_tiles,),
        in_specs=[
            pl.BlockSpec((MOE_TM, D_MODEL), lambda t, te, nv: (t, 0)),
            pl.BlockSpec((None, D_MODEL, 2 * EXPERT_FF), lambda t, te, nv: (te[t], 0, 0)),
            pl.BlockSpec((None, 1, 2 * EXPERT_FF), lambda t, te, nv: (te[t], 0, 0)),
            pl.BlockSpec((None, EXPERT_FF, D_MODEL), lambda t, te, nv: (te[t], 0, 0)),
            pl.BlockSpec((None, 1, D_MODEL), lambda t, te, nv: (te[t], 0, 0)),
        ],
        out_specs=pl.BlockSpec((MOE_TM, D_MODEL), lambda t, te, nv: (t, 0)),
        scratch_shapes=[pltpu.VMEM((D_MODEL, 2 * EXPERT_FF), BF16),
                        pltpu.VMEM((EXPERT_FF, D_MODEL), BF16)],
    )
    return pl.pallas_call(
        _moe_kernel,
        out_shape=jax.ShapeDtypeStruct((rows, D_MODEL), BF16),
        grid_spec=grid_spec,
        compiler_params=_cparams(("arbitrary",)),
        name="moe",
    )(tile_expert, n_valid, x_sorted, wgu, bgu, wd, bd)


def _final_kernel(h1_ref, y0_ref, y1_ref, y2_ref, y3_ref, tw_ref, fg_ref, o_ref):
    tw = tw_ref[...]
    h = h1_ref[...]
    for k, y_ref in enumerate((y0_ref, y1_ref, y2_ref, y3_ref)):
        h = h + tw[:, k:k + 1] * y_ref[...].astype(F32)
    o_ref[...] = _rms_rows(h, fg_ref[...])


def _final(h1, ys, tw, fg, *, tm):
    n = h1.shape[0]
    row = lambda w: pl.BlockSpec((tm, w), lambda i: (i, 0))
    return pl.pallas_call(
        _final_kernel,
        out_shape=jax.ShapeDtypeStruct((n, D_MODEL), F32),
        grid=(n // tm,),
        in_specs=[row(D_MODEL)] * (1 + TOP_K)
                 + [row(LANES), pl.BlockSpec((1, D_MODEL), lambda i: (0, 0))],
        out_specs=row(D_MODEL),
        compiler_params=_cparams(("parallel",)),
        name="final",
    )(h1, *ys, tw, fg)


def _split_w_in(w_in):
    offs = np.cumsum((0,) + IN_SPLITS)
    names = ("q", "k", "v", "qi", "ki", "wi", "z", "xs", "b", "c", "dt", "gate")
    return {nm: w_in[:, offs[i]:offs[i + 1]] for i, nm in enumerate(names)}


def _expert_dispatch(top_i, rank, hist, n_rows_pad):
    n = top_i.shape[0]
    pairs = n * TOP_K
    counts = jnp.sum(hist, axis=0)
    grp_start = jnp.cumsum(counts) - counts
    tiles_per_e = (counts + MOE_TM - 1) // MOE_TM
    tile_end = jnp.cumsum(tiles_per_e)
    row_start = (tile_end - tiles_per_e) * MOE_TM
    offset = row_start - grp_start
    base = row_start[None, :] + jnp.cumsum(hist, axis=0) - hist
    base_tok = jnp.repeat(base, n // hist.shape[0], axis=0)
    hit = top_i[:, :, None] == jnp.arange(N_EXPERTS, dtype=I32)[None, None, :]
    dest = jnp.sum(jnp.where(hit, base_tok[:, None, :], 0), axis=2) + rank
    token_of_pair = jnp.broadcast_to(jnp.arange(n, dtype=I32)[:, None], (n, TOP_K))
    _, order_tok = lax.sort_key_val(dest.reshape(-1), token_of_pair.reshape(-1))
    n_tiles = n_rows_pad // MOE_TM
    n_valid = tile_end[-1].astype(I32)
    tile_ids = jnp.minimum(jnp.arange(n_tiles, dtype=I32), n_valid - 1)
    tile_expert = jnp.searchsorted(tile_end, tile_ids, side="right",
                                   method="compare_all").astype(I32)
    tile_expert = jnp.minimum(tile_expert, N_EXPERTS - 1)
    rows = jnp.arange(n_rows_pad, dtype=I32)
    e_row = jnp.repeat(tile_expert, MOE_TM)
    s_row = rows - offset[e_row]
    live = (s_row >= grp_start[e_row]) & (s_row < (grp_start + counts)[e_row])
    src_token = jnp.where(live, order_tok[jnp.clip(s_row, 0, pairs - 1)], rows % n)
    return src_token, dest, tile_expert, n_valid.reshape(1)


def kernel(x, meta_tokens, norm1_g, w_in, conv_w, conv_b, dt_bias, a_log, d_skip, ssm_norm_g,
           w_o_attn, w_o_ssm, w_out, norm2_g, w_router, b_router, w_gate_up, b_gate_up,
           w_down, b_down, final_g):
    nb, seq, d = x.shape
    n = nb * seq
    assert d == D_MODEL and seq % TQ == 0 and w_in.shape[0] == 1
    xf = x.reshape(n, d)
    meta_pad = jnp.concatenate([jnp.zeros((META_PAD, d), x.dtype), meta_tokens.astype(x.dtype)], 0)
    g1 = norm1_g[0].reshape(1, d)

    w = _split_w_in(w_in[0])
    w_nat = jnp.concatenate([w["k"], w["z"], w["xs"], w["b"], w["c"], w["gate"]], 1).astype(BF16)
    w_qvt = jnp.concatenate([w["q"], w["v"]], 1).T.astype(BF16)
    zpad = lambda c: jnp.zeros((d, c), F32)
    w_sn = jnp.concatenate([w["ki"], zpad(SN_DT - IDX_DIM), w["dt"],
                            zpad(SN_W - SN_DT - SSM_HEADS)], 1).astype(BF16)
    w_st = jnp.concatenate([w["qi"], w["wi"], zpad(ST_W - ST_WI - IDX_HEADS)], 1).T.astype(BF16)

    tabs = _rope_tables(N_META + jnp.arange(seq, dtype=I32))
    tabs_t = tuple(t.T for t in tabs)
    pos_meta = jnp.maximum(jnp.arange(CHUNK, dtype=I32) - META_PAD, 0)
    tabs_m = _rope_tables(pos_meta)
    tabs_mt = tuple(t.T for t in tabs_m)

    tm_nat = 1024 if seq % 1024 == 0 else TQ
    w_t = jnp.concatenate([w_qvt, w_st], axis=0)
    q_slabs = ATTN_WIDTH // LANES
    tr_rope = tuple(range(q_slabs)) + tuple(
        2 * q_slabs + s for s in range(IDX_HEADS * IDX_DIM // LANES))
    nat, sm_n = _proj_nat(xf, g1, w_nat, w_sn, tabs, tm=tm_nat, tn=1024, rope_tiles=1,
                          seq_tiles=seq // tm_nat, name="proj_nat")
    qvt, sm_t = _proj_tr(xf, g1, w_t, tabs_t, tm=TQ, nb=nb, width_main=2 * ATTN_WIDTH,
                         rope_slabs=tr_rope, name="proj_t")
    nat_m, sm_n_m = _proj_nat(meta_pad, g1, w_nat, w_sn, tabs_m, tm=CHUNK, tn=1024, rope_tiles=1,
                              seq_tiles=1, name="proj_nat_meta")
    qvt_m, _ = _proj_tr(meta_pad, g1, w_t, tabs_mt, tm=CHUNK, nb=1, width_main=2 * ATTN_WIDTH,
                        rope_slabs=tr_rope, name="proj_t_meta")

    attn = _attention(qvt, nat, sm_t, sm_n, nat_m, qvt_m, nb=nb, seq=seq)

    dt_raw = sm_n[:, SN_DT:SN_DT + SSM_HEADS].reshape(nb, seq // CHUNK, CHUNK, SSM_GROUPS,
                                                      GROUP_HEADS)
    dtt = jnp.transpose(dt_raw, (3, 0, 1, 4, 2))
    dt_raw_m = sm_n_m[:, SN_DT:SN_DT + SSM_HEADS].reshape(CHUNK, SSM_GROUPS, GROUP_HEADS)
    dtt_m = jnp.transpose(dt_raw_m, (1, 2, 0))
    ssm = _ssd(nat, nat_m, dtt, dtt_m, conv_w[0], conv_b[0], dt_bias[0], a_log[0], d_skip[0],
               ssm_norm_g[0], nb=nb, seq=seq)

    wr = jnp.concatenate([w_router[0], jnp.zeros((d, LANES - N_EXPERTS), F32)], 1)
    br = jnp.concatenate([b_router[0], jnp.full((LANES - N_EXPERTS,), -1e30, F32)]).reshape(1, LANES)
    wr_hi = wr.astype(BF16)
    wr_lo = (wr - wr_hi.astype(F32)).astype(BF16)
    tm_merge = 512 if n % 512 == 0 else TQ
    h1, hn2, ti, tw, hist = _merge(attn, ssm, nat, xf, w_o_attn[0].astype(BF16),
                                   w_o_ssm[0].astype(BF16), w_out[0].astype(BF16),
                                   norm2_g[0].reshape(1, d), wr_hi, wr_lo, br, tm=tm_merge)

    n_tiles = n * TOP_K // MOE_TM + N_EXPERTS
    src_token, pos, tile_expert, n_valid = _expert_dispatch(
        ti[:, :TOP_K], ti[:, TOP_K:2 * TOP_K], hist[::8, :N_EXPERTS], n_tiles * MOE_TM)
    x_sorted = hn2.at[src_token].get(mode="promise_in_bounds")
    y_sorted = _moe(tile_expert, n_valid, x_sorted, w_gate_up[0],
                    b_gate_up[0].reshape(N_EXPERTS, 1, -1), w_down[0],
                    b_down[0].reshape(N_EXPERTS, 1, -1))
    ys = [y_sorted.at[pos[:, k]].get(mode="promise_in_bounds") for k in range(TOP_K)]

    out = _final(h1, ys, tw, final_g.reshape(1, d), tm=512 if n % 512 == 0 else TQ)
    return out.reshape(nb, seq, d)
```

```python
import functools
import math

import numpy as np
import jax
import jax.numpy as jnp
from jax import lax
from jax.experimental import pallas as pl
from jax.experimental.pallas import tpu as pltpu

F32 = jnp.float32
BF16 = jnp.bfloat16
I32 = jnp.int32
I16 = jnp.int16

D_MODEL = 1024
N_META = 16
RMS_EPS = 1e-6
N_HEADS = 16
HEAD_DIM = 64
ATTN_WIDTH = N_HEADS * HEAD_DIM
ROT_DIM = HEAD_DIM // 4
ROPE_THETA = 500000.0
IDX_HEADS = 8
IDX_DIM = 64
TOPK_KEYS_MAX = 256
SSM_INNER = 2 * D_MODEL
SSM_HEAD_DIM = 64
SSM_HEADS = SSM_INNER // SSM_HEAD_DIM
SSM_GROUPS = 4
SSM_STATE = 128
CONV_WIDTH = 4
CHUNK = 128
N_EXPERTS = 32
TOP_K = 4
EXPERT_FF = D_MODEL
SWIGLU_LIMIT = 7.0
SWIGLU_ALPHA = 1.702
IN_SPLITS = (ATTN_WIDTH, ATTN_WIDTH, ATTN_WIDTH, IDX_HEADS * IDX_DIM, IDX_DIM, IDX_HEADS,
             SSM_INNER, SSM_INNER, SSM_GROUPS * SSM_STATE, SSM_GROUPS * SSM_STATE, SSM_HEADS,
             2 * D_MODEL)

LANES = 128
GROUP_CH = SSM_INNER // SSM_GROUPS
GROUP_HEADS = SSM_HEADS // SSM_GROUPS
META_PAD = CHUNK - N_META
TQ = 256
ATTN_META_ROWS = 32
FLASH_LAG = 8
MOE_TM = 512
INT_MIN = -2147483648
VMEM_LIMIT = 56 * 1024 * 1024

NAT_K, NAT_Z, NAT_XS, NAT_B, NAT_C, NAT_G = 0, 1024, 3072, 5120, 5632, 6144
NAT_W = 8192
SN_KI, SN_DT, SN_W = 0, 128, 256
ST_QI, ST_WI, ST_W = 0, 512, 640


def _cparams(sem):
    return pltpu.CompilerParams(dimension_semantics=sem, vmem_limit_bytes=VMEM_LIMIT)


def _rms_rows(x, g):
    ms = jnp.mean(x * x, axis=-1, keepdims=True)
    return x * lax.rsqrt(ms + RMS_EPS) * g


def _proj_nat_kernel(x_ref, g_ref, w_ref, wsm_ref, cos_ref, sa_ref, sb_ref, o_ref, osm_ref,
                     hn_ref, *, rope_tiles):
    j = pl.program_id(1)

    def rope(a):
        return (a * cos_ref[...] + pltpu.roll(a, 8, 1) * sa_ref[...]
                + pltpu.roll(a, LANES - 8, 1) * sb_ref[...])

    @pl.when(j == 0)
    def _():
        hn = _rms_rows(x_ref[...], g_ref[...]).astype(hn_ref.dtype)
        hn_ref[...] = hn
        small = jnp.dot(hn, wsm_ref[...], preferred_element_type=F32)
        osm_ref[:, 0:LANES] = rope(small[:, 0:LANES])
        osm_ref[:, LANES:] = small[:, LANES:]

    acc = jnp.dot(hn_ref[...], w_ref[...], preferred_element_type=F32)

    @pl.when(j < rope_tiles)
    def _():
        for s in range(acc.shape[1] // LANES):
            cols = slice(s * LANES, (s + 1) * LANES)
            o_ref[:, cols] = rope(acc[:, cols]).astype(o_ref.dtype)

    @pl.when(j >= rope_tiles)
    def _():
        o_ref[...] = acc.astype(o_ref.dtype)


def _proj_nat(x, g, w, w_small, tabs, *, tm, tn, rope_tiles, seq_tiles, name):
    n, d = x.shape
    width, width_sm = w.shape[1], w_small.shape[1]
    cos, sa, sb = tabs
    tab_spec = pl.BlockSpec((tm, LANES), lambda i, j: (i % seq_tiles, 0))
    return pl.pallas_call(
        functools.partial(_proj_nat_kernel, rope_tiles=rope_tiles),
        out_shape=(jax.ShapeDtypeStruct((n, width), BF16),
                   jax.ShapeDtypeStruct((n, width_sm), F32)),
        grid=(n // tm, width // tn),
        in_specs=[
            pl.BlockSpec((tm, d), lambda i, j: (i, 0)),
            pl.BlockSpec((1, d), lambda i, j: (0, 0)),
            pl.BlockSpec((d, tn), lambda i, j: (0, j)),
            pl.BlockSpec((d, width_sm), lambda i, j: (0, 0)),
            tab_spec, tab_spec, tab_spec,
        ],
        out_specs=(pl.BlockSpec((tm, tn), lambda i, j: (i, j)),
                   pl.BlockSpec((tm, width_sm), lambda i, j: (i, 0))),
        scratch_shapes=[pltpu.VMEM((tm, d), BF16)],
        compiler_params=_cparams(("parallel", "arbitrary")),
        name=name,
    )(x, g, w, w_small, cos, sa, sb)


def _proj_tr_kernel(x_ref, g_ref, wt_ref, cos_ref, sa_ref, sb_ref, o_ref, osm_ref, *,
                    rope_slabs):
    hn = _rms_rows(x_ref[...], g_ref[...]).astype(BF16)
    acc = lax.dot_general(wt_ref[...], hn, (((1,), (1,)), ((), ())),
                          preferred_element_type=F32)
    cos, sa, sb = cos_ref[...], sa_ref[...], sb_ref[...]
    n_main = o_ref.shape[0] // LANES
    for s in range(acc.shape[0] // LANES):
        a = acc[s * LANES:(s + 1) * LANES, :]
        if s in rope_slabs:
            a = a * cos + pltpu.roll(a, 8, 0) * sa + pltpu.roll(a, LANES - 8, 0) * sb
        if s < n_main:
            o_ref[s * LANES:(s + 1) * LANES, :] = a.astype(o_ref.dtype)
        else:
            osm_ref[(s - n_main) * LANES:(s - n_main + 1) * LANES, :] = a


def _proj_tr(x, g, wt, tabs_t, *, tm, nb, width_main, rope_slabs, name):
    n, d = x.shape
    width = wt.shape[0]
    width_sm = width - width_main
    seq_tiles = n // nb // tm
    cos, sa, sb = tabs_t
    tab_spec = pl.BlockSpec((LANES, tm), lambda i: (0, i % seq_tiles))
    out_map = lambda i: (i // seq_tiles, i % seq_tiles, 0, 0)
    return pl.pallas_call(
        functools.partial(_proj_tr_kernel, rope_slabs=rope_slabs),
        out_shape=(jax.ShapeDtypeStruct((nb, seq_tiles, width_main, tm), BF16),
                   jax.ShapeDtypeStruct((nb, seq_tiles, width_sm, tm), F32)),
        grid=(n // tm,),
        in_specs=[
            pl.BlockSpec((tm, d), lambda i: (i, 0)),
            pl.BlockSpec((1, d), lambda i: (0, 0)),
            pl.BlockSpec((width, d), lambda i: (0, 0)),
            tab_spec, tab_spec, tab_spec,
        ],
        out_specs=(pl.BlockSpec((None, None, width_main, tm), out_map),
                   pl.BlockSpec((None, None, width_sm, tm), out_map)),
        compiler_params=_cparams(("parallel",)),
        name=name,
    )(x, g, wt, cos, sa, sb)


def _rope_tables(pos):
    half = ROT_DIM // 2
    inv_freq = jnp.exp(-math.log(ROPE_THETA) * jnp.arange(half, dtype=F32) / half)
    ang = pos.astype(F32)[:, None] * inv_freq[None, :]
    cos8, sin8 = jnp.cos(ang), jnp.sin(ang)
    t = pos.shape[0]
    cc = np.arange(LANES) % HEAD_DIM
    in_rot = jnp.asarray(cc < ROT_DIM)
    lo = jnp.asarray(cc < half)
    hi = jnp.asarray((cc >= half) & (cc < ROT_DIM))
    idx = jnp.asarray(cc % half)
    cos_l = jnp.take(cos8, idx, axis=1)
    sin_l = jnp.take(sin8, idx, axis=1)
    cos = jnp.where(in_rot[None, :], cos_l, 1.0)
    sa = jnp.where(hi[None, :], sin_l, 0.0)
    sb = jnp.where(lo[None, :], -sin_l, 0.0)
    del t
    return cos, sa, sb


def _attn_kernel(qt_ref, vt_ref, k_ref, qit_ref, wit_ref, ki_ref, kmeta_ref, vtmeta_ref,
                 o_ref, keys_ref, hi_ref, lo_ref, bias_ref, qm_ref, m_ref, l_ref, acc_ref, s_ref,
                 *, ksel, nbits):
    j = pl.program_id(1)
    tq = TQ
    nch = j + 1
    int_min = jnp.int32(INT_MIN)
    neg_inf = jnp.float32(-jnp.inf)

    row128 = lax.broadcasted_iota(I32, (LANES, tq), 0)
    for h in range(N_HEADS):
        slab = qt_ref[(h // 2) * LANES:(h // 2 + 1) * LANES, :]
        keep = (row128 < HEAD_DIM) if h % 2 == 0 else (row128 >= HEAD_DIM)
        scaled = slab.astype(F32) * (HEAD_DIM ** -0.5 * math.log2(math.e))
        qm_ref[h] = jnp.where(keep, scaled, 0.0).astype(qm_ref.dtype)

    w_all = wit_ref[0:IDX_HEADS, :] * (IDX_HEADS ** -0.5 * IDX_DIM ** -0.5)
    qi_bf = [qit_ref[h * IDX_DIM:(h + 1) * IDX_DIM, :].astype(BF16) for h in range(IDX_HEADS)]
    qidx = j * tq + lax.broadcasted_iota(I32, (tq, tq), 1)
    krow = lax.broadcasted_iota(I32, (tq, tq), 0)

    def score_body(c, carry):
        start = pl.multiple_of(c * tq, tq)
        kic = ki_ref[pl.ds(start, tq), :][:, :IDX_DIM].astype(BF16)
        sc = jnp.zeros((tq, tq), F32)
        for h in range(IDX_HEADS):
            lg = jnp.dot(kic, qi_bf[h], preferred_element_type=F32)
            sc = sc + jnp.maximum(lg, 0.0) * w_all[h:h + 1, :]
        bits = pltpu.bitcast(sc, I32)
        skey = jnp.where(bits < 0, bits ^ jnp.int32(0x7FFFFFFF), bits)
        skey = jnp.where(c * tq + krow <= qidx, skey, int_min)
        keys_ref[c] = skey
        hi_ref[c] = (skey >> 16).astype(I16)
        lo_ref[c] = ((skey & 0xFFFF) - 32768).astype(I16)
        return carry

    lax.fori_loop(0, nch, score_body, 0)

    @pl.when(nch % 2 == 1)
    def _():
        hi_ref[nch] = jnp.full((tq, tq), -32768, I16)
        lo_ref[nch] = jnp.full((tq, tq), -32768, I16)

    npair = (nch + 1) // 2

    def count(pred):
        def body(c, cnt):
            hit = pred(keys_ref[c], c).astype(I32)
            return cnt + hit.reshape(tq // 8, 8, tq).sum(axis=0)
        cnt8 = lax.fori_loop(0, nch, body, jnp.zeros((8, tq), I32))
        return jnp.sum(cnt8, axis=0, keepdims=True)

    def count16(ref, cand):
        cand16 = cand.astype(I16)
        one, zero = jnp.int16(1), jnp.int16(0)

        def body(i, cnt):
            for c in (2 * i, 2 * i + 1):
                hit = pltpu.bitcast(jnp.where(ref[c] >= cand16, one, zero), I32)
                cnt = cnt + hit.reshape(tq // 16, 8, tq).sum(axis=0)
            return cnt

        cnt8 = lax.fori_loop(0, npair, body, jnp.zeros((8, tq), I32))
        both = jnp.sum(cnt8, axis=0, keepdims=True)
        return (both & 0xFFFF) + lax.shift_right_logical(both, 16)

    def search16(ref, want):
        def bit_body(it, t_b):
            cand_b = t_b | lax.shift_left(jnp.int32(1), 15 - it)
            cnt = count16(ref, cand_b - 32768)
            return jnp.where(cnt >= want, cand_b, t_b)
        return lax.fori_loop(0, 16, bit_body, jnp.zeros((1, tq), I32))

    hi_b = search16(hi_ref, ksel)
    thr_hi = hi_b - 32768
    n_above = count16(hi_ref, jnp.minimum(thr_hi + 1, 32767))
    n_above = jnp.where(thr_hi >= 32767, 0, n_above)
    thr_hi16 = thr_hi.astype(I16)

    def bucket_body(c, carry):
        lo_ref[c] = jnp.where(hi_ref[c] == thr_hi16, lo_ref[c], jnp.int16(-32768))
        return carry

    lax.fori_loop(0, 2 * npair, bucket_body, 0)
    lo_b = search16(lo_ref, ksel - n_above)
    thr = lax.shift_left(thr_hi, 16) | lo_b
    n_gt = count(lambda kc, c: kc > thr)
    n_eq = count(lambda kc, c: kc == thr)
    need = ksel - n_gt
    fix = (n_eq > need) & (thr != int_min)
    thr_eff = jnp.maximum(thr, int_min + 1)

    def bias_body(c, carry):
        bias_ref[c] = jnp.where(keys_ref[c] >= thr_eff, 0.0, neg_inf)
        return carry

    lax.fori_loop(0, nch, bias_body, 0)

    @pl.when(jnp.max(fix.astype(I32)) > 0)
    def _():
        def idx_body(it, v):
            cand = v | lax.shift_left(jnp.int32(1), nbits - 1 - it)
            cnt = count(lambda kc, c: jnp.where(c * tq + krow < cand, kc, int_min) == thr)
            return jnp.where(cnt <= need - 1, cand, v)

        last = lax.fori_loop(0, nbits, idx_body, jnp.zeros((1, tq), I32))

        def fix_body(c, carry):
            kc = keys_ref[c]
            tie_lim = jnp.where(fix, last, jnp.int32(2 ** 30))
            tie_ok = jnp.where(c * tq + krow <= tie_lim, thr_eff, thr_eff + 1)
            bias_ref[c] = jnp.where(kc >= tie_ok, 0.0, neg_inf)
            return carry

        lax.fori_loop(0, nch, fix_body, 0)

    def chunk_step(kslab_of, vt_of, bias_c, n_keys, first):
        alphas = []
        ones = jnp.ones((16, n_keys), BF16)

        def scores(h):
            s = jnp.dot(kslab_of(h), qm_ref[h], preferred_element_type=F32) + bias_c
            s_ref[h, 0:n_keys, :] = s
            smax = jnp.max(s, axis=0, keepdims=True)
            if first:
                m_ref[h:h + 1, :] = smax
            else:
                m_old = m_ref[h:h + 1, :]
                m_new = jnp.maximum(m_old, smax)
                alphas.append(jnp.exp2(m_old - m_new))
                m_ref[h:h + 1, :] = m_new

        def values(h):
            p = jnp.exp2(s_ref[h, 0:n_keys, :] - m_ref[h:h + 1, :]).astype(BF16)
            res = jnp.dot(jnp.concatenate([vt_of(h), ones], axis=0), p,
                          preferred_element_type=F32)
            pv, psum = res[:HEAD_DIM, :], res[HEAD_DIM:HEAD_DIM + 1, :]
            rows = slice(h * HEAD_DIM, (h + 1) * HEAD_DIM)
            if first:
                l_ref[h:h + 1, :] = psum
                acc_ref[rows, :] = pv
            else:
                l_ref[h:h + 1, :] = l_ref[h:h + 1, :] * alphas[h] + psum
                acc_ref[rows, :] = acc_ref[rows, :] * alphas[h] + pv

        for h in range(N_HEADS + FLASH_LAG):
            if h < N_HEADS:
                scores(h)
            if h >= FLASH_LAG:
                values(h - FLASH_LAG)

    meta_row = lax.broadcasted_iota(I32, (ATTN_META_ROWS, tq), 0)
    bias_meta = jnp.where(meta_row >= ATTN_META_ROWS - N_META, 0.0, neg_inf)
    chunk_step(lambda h: kmeta_ref[:, (h // 2) * LANES:(h // 2 + 1) * LANES],
               lambda h: vtmeta_ref[h * HEAD_DIM:(h + 1) * HEAD_DIM, :], bias_meta,
               ATTN_META_ROWS, True)

    def flash_body(c, carry):
        start = pl.multiple_of(c * tq, tq)
        chunk_step(lambda h: k_ref[pl.ds(start, tq), (h // 2) * LANES:(h // 2 + 1) * LANES],
                   lambda h: vt_ref[c, h * HEAD_DIM:(h + 1) * HEAD_DIM, :], bias_ref[c], tq, False)
        return carry

    lax.fori_loop(0, nch, flash_body, 0)

    for h in range(N_HEADS):
        rows = slice(h * HEAD_DIM, (h + 1) * HEAD_DIM)
        acc_ref[rows, :] = acc_ref[rows, :] * (1.0 / l_ref[h:h + 1, :])
    o_ref[...] = acc_ref[...].T.astype(o_ref.dtype)


def _attention(qvt, nat, sm_t, sm_n, nat_meta, qvt_meta, *, nb, seq):
    nq = seq // TQ
    ksel = min(TOPK_KEYS_MAX, seq // 4)
    nbits = max(1, int(math.ceil(math.log2(seq))))
    kernel = functools.partial(_attn_kernel, ksel=ksel, nbits=nbits)
    return pl.pallas_call(
        kernel,
        out_shape=jax.ShapeDtypeStruct((nb * seq, ATTN_WIDTH), BF16),
        grid=(nb, nq),
        in_specs=[
            pl.BlockSpec((None, None, ATTN_WIDTH, TQ), lambda b, j: (b, j, 0, 0)),
            pl.BlockSpec((None, nq, ATTN_WIDTH, TQ), lambda b, j: (b, 0, 1, 0)),
            pl.BlockSpec((seq, ATTN_WIDTH), lambda b, j: (b, NAT_K // ATTN_WIDTH)),
            pl.BlockSpec((None, None, IDX_HEADS * IDX_DIM, TQ), lambda b, j: (b, j, 0, 0)),
            pl.BlockSpec((None, None, LANES, TQ), lambda b, j: (b, j, ST_WI // LANES, 0)),
            pl.BlockSpec((seq, LANES), lambda b, j: (b, SN_KI // LANES)),
            pl.BlockSpec((ATTN_META_ROWS, ATTN_WIDTH),
                         lambda b, j: (CHUNK // ATTN_META_ROWS - 1, NAT_K // ATTN_WIDTH)),
            pl.BlockSpec((ATTN_WIDTH, ATTN_META_ROWS), lambda b, j: (0, 0)),
        ],
        out_specs=pl.BlockSpec((TQ, ATTN_WIDTH), lambda b, j: (b * nq + j, 0)),
        scratch_shapes=[
            pltpu.VMEM((nq, TQ, TQ), I32),
            pltpu.VMEM((nq + nq % 2, TQ, TQ), I16),
            pltpu.VMEM((nq + nq % 2, TQ, TQ), I16),
            pltpu.VMEM((nq, TQ, TQ), F32),
            pltpu.VMEM((N_HEADS, LANES, TQ), BF16),
            pltpu.VMEM((N_HEADS, TQ), F32),
            pltpu.VMEM((N_HEADS, TQ), F32),
            pltpu.VMEM((ATTN_WIDTH, TQ), F32),
            pltpu.VMEM((N_HEADS, TQ, TQ), F32),
        ],
        compiler_params=_cparams(("parallel", "arbitrary")),
        name="attn",
    )(qvt, qvt, nat, sm_t, sm_t, sm_n, nat_meta, qvt_meta)


def _softplus(x):
    return jnp.maximum(x, 0.0) + jnp.log1p(jnp.exp(-jnp.abs(x)))


def _sigmoid(x):
    return 0.5 + 0.5 * jnp.tanh(0.5 * x)


def _silu(x):
    h = 0.5 * x
    return h + h * jnp.tanh(h)


def _pad_transpose(rows):
    padded = jnp.concatenate([rows, jnp.zeros((LANES - rows.shape[0], LANES), rows.dtype)], axis=0)
    return padded.T


STAGE_TAIL = 16
SSD_GROUPS_PER_STEP = 2


def _ssd_kernel(xs_ref, z_ref, b_ref, c_ref, dtt_ref,
                xs_m_ref, b_m_ref, c_m_ref, dtt_m_ref,
                cw_ref, cb_ref, dtb_ref, alog_ref, dsk_ref, ng_ref, exp_ref, tri_ref, shift_ref,
                o_ref, state_ref, dt_all_ref, cs_all_ref, *, n_chunks):
    gs = range(SSD_GROUPS_PER_STEP)
    a_neg = [-jnp.exp(alog_ref[g]) for g in gs]
    dt_bias = [dtb_ref[g] for g in gs]
    expand = exp_ref[...]
    tri_u = tri_ref[...]
    li = lax.broadcasted_iota(I32, (CHUNK, CHUNK), 0)
    si = lax.broadcasted_iota(I32, (CHUNK, CHUNK), 1)
    tril = si <= li
    lane = lax.broadcasted_iota(I32, (CHUNK, LANES), 1)
    low_half = lane < SSM_HEAD_DIM
    neg_inf = jnp.float32(-jnp.inf)

    xb, xc = GROUP_CH, GROUP_CH + SSM_STATE

    def cumsum_tokens(rows):
        return jnp.dot(rows, tri_u, precision=lax.Precision.HIGHEST, preferred_element_type=F32)

    def token_major(dt_r, cs_r):
        dt_c = _pad_transpose(dt_r)
        cs_c = _pad_transpose(cs_r)
        dec_c = jnp.exp(cs_c[CHUNK - 1:CHUNK, :] - cs_c)
        return cs_c, dt_c.astype(BF16), (dt_c * dec_c).astype(BF16), jnp.exp(cs_c).astype(BF16)

    widths = (GROUP_CH, SSM_STATE, SSM_STATE)
    gcols = lambda g: slice(g * GROUP_CH, (g + 1) * GROUP_CH)

    def pieces(refs, rows, g):
        return [r[rows, g * w:(g + 1) * w] for r, w in zip(refs, widths)]

    def chunk(src, rows, tail_src, trows, dt_of, out_rows, is_meta):
        stacks = []
        for g in gs:
            tail = ([jnp.zeros((STAGE_TAIL, w), BF16) for w in widths] if tail_src is None
                    else pieces(tail_src, trows, g))
            stacks.append(jnp.concatenate(
                [jnp.concatenate(tail, axis=1), jnp.concatenate(pieces(src, rows, g), axis=1)],
                axis=0))
        shifted = [jnp.dot(shift_ref[...], stacks[g], preferred_element_type=F32) for g in gs]
        x, bm_bf, cm_bf = [], [], []
        for g in gs:
            conv = cb_ref[g]
            for k in range(CONV_WIDTH):
                conv = conv + (shifted[g][k * CHUNK:(k + 1) * CHUNK, :]
                               * cw_ref[g, CONV_WIDTH - 1 - k:CONV_WIDTH - k, :])
            act = _silu(conv)
            x.append(act[:, 0:xb])
            bm_bf.append(act[:, xb:xc].astype(BF16))
            cm_bf.append(act[:, xc:].astype(BF16))
        if is_meta:
            dt_r = []
            for g in gs:
                d = _softplus(dt_of(g) + dt_bias[g])
                dt_r.append(jnp.where(lax.broadcasted_iota(I32, d.shape, 1) >= META_PAD, d, 0.0))
            cs_r = [cumsum_tokens(dt_r[g] * a_neg[g]) for g in gs]
        else:
            dt_r, cs_r = zip(*[dt_of(g) for g in gs])
        cs_c, dt_bf, dd_bf, ec_bf = zip(*[token_major(dt_r[g], cs_r[g]) for g in gs])
        dtx = [jnp.dot(dt_bf[g], expand, preferred_element_type=F32) for g in gs]
        decx = [jnp.dot(dd_bf[g], expand, preferred_element_type=F32) for g in gs]
        ecsx = [jnp.dot(ec_bf[g], expand, preferred_element_type=F32) for g in gs]
        state = [state_ref[g] for g in gs]
        new_part = [lax.dot_general(bm_bf[g], (x[g] * decx[g]).astype(BF16),
                                    (((0,), (0,)), ((), ())), preferred_element_type=F32)
                    for g in gs]
        if not is_meta:
            xdt = [(x[g] * dtx[g]).astype(BF16) for g in gs]
            cb = [lax.dot_general(cm_bf[g], bm_bf[g], (((1,), (1,)), ((), ())),
                                  preferred_element_type=F32) for g in gs]
            y_off = [jnp.dot(cm_bf[g], state[g].astype(BF16), preferred_element_type=F32)
                     * ecsx[g] for g in gs]
            slabs = [[] for _ in gs]
            for pp in range(GROUP_HEADS // 2):
                halves = [[] for _ in gs]
                for e in (2 * pp, 2 * pp + 1):
                    for g in gs:
                        diff = cs_c[g][:, e:e + 1] - cs_r[g][e:e + 1, :]
                        lmat = jnp.exp(jnp.where(tril, diff, neg_inf))
                        halves[g].append(jnp.dot((cb[g] * lmat).astype(BF16),
                                                 xdt[g][:, pp * LANES:(pp + 1) * LANES],
                                                 preferred_element_type=F32))
                for g in gs:
                    slabs[g].append(jnp.where(low_half, halves[g][0], halves[g][1]))
            for g in gs:
                y = jnp.concatenate(slabs[g], axis=1) + y_off[g] + x[g] * dsk_ref[:, gcols(g)]
                y = y * _silu(z_ref[rows, gcols(g)].astype(F32))
                ms = jnp.mean(y * y, axis=-1, keepdims=True)
                o_ref[out_rows, gcols(g)] = (y * lax.rsqrt(ms + RMS_EPS)
                                             * ng_ref[:, gcols(g)]).astype(o_ref.dtype)
        for g in gs:
            state_ref[g] = state[g] * ecsx[g][CHUNK - 1:CHUNK, :] + new_part[g]

    state_ref[...] = jnp.zeros_like(state_ref)
    n_rows = n_chunks * GROUP_HEADS
    for g in gs:
        dt_all = _softplus(dtt_ref[g].reshape(n_rows, CHUNK) + jnp.tile(dt_bias[g], (n_chunks, 1)))
        dt_all_ref[g] = dt_all
        cs_all_ref[g] = cumsum_tokens(dt_all * jnp.tile(a_neg[g], (n_chunks, 1)))

    def heads_of(ci):
        return (slice(ci * GROUP_HEADS, (ci + 1) * GROUP_HEADS) if isinstance(ci, int)
                else pl.ds(pl.multiple_of(ci * GROUP_HEADS, GROUP_HEADS), GROUP_HEADS))

    def dt_cs(ci):
        return lambda g: (dt_all_ref[g, heads_of(ci), :], cs_all_ref[g, heads_of(ci), :])

    real, meta = (xs_ref, b_ref, c_ref), (xs_m_ref, b_m_ref, c_m_ref)
    all_rows = slice(0, CHUNK)
    chunk(meta, all_rows, None, None, lambda g: dtt_m_ref[g], None, True)
    chunk(real, all_rows, meta, slice(CHUNK - STAGE_TAIL, CHUNK), dt_cs(0), all_rows, False)

    def body(ci, carry):
        rows = pl.ds(pl.multiple_of(ci * CHUNK, CHUNK), CHUNK)
        trows = pl.ds(pl.multiple_of(ci * CHUNK - STAGE_TAIL, STAGE_TAIL), STAGE_TAIL)
        chunk(real, rows, real, trows, dt_cs(ci), rows, False)
        return carry

    lax.fori_loop(1, n_chunks, body, 0)


def _ssd(nat, nat_meta, dtt, dtt_meta, conv_w, conv_b, dt_bias, a_log, d_skip, ssm_norm_g,
         *, nb, seq):
    gch, st = GROUP_CH, SSM_STATE
    expand = np.zeros((LANES, gch), np.float32)
    for e in range(GROUP_HEADS):
        expand[e, e * SSM_HEAD_DIM:(e + 1) * SSM_HEAD_DIM] = 1.0
    tri = np.triu(np.ones((CHUNK, CHUNK), np.float32))
    def group_cols(p):
        px = p[:, :SSM_INNER].reshape(-1, SSM_GROUPS, gch)
        pb = p[:, SSM_INNER:SSM_INNER + SSM_GROUPS * st].reshape(-1, SSM_GROUPS, st)
        pc = p[:, SSM_INNER + SSM_GROUPS * st:].reshape(-1, SSM_GROUPS, st)
        return jnp.transpose(jnp.concatenate([px, pb, pc], axis=2), (1, 0, 2))

    cw = group_cols(conv_w)
    cb = group_cols(conv_b.reshape(1, -1))
    stage_w = gch + 2 * st
    shift = np.zeros((CONV_WIDTH * CHUNK, STAGE_TAIL + CHUNK), np.float32)
    for k in range(CONV_WIDTH):
        shift[k * CHUNK + np.arange(CHUNK), STAGE_TAIL + np.arange(CHUNK) - k] = 1.0
    dsk = jnp.repeat(d_skip, SSM_HEAD_DIM).reshape(1, SSM_INNER)
    ng = ssm_norm_g.reshape(1, SSM_INNER)
    dtb = dt_bias.reshape(SSM_GROUPS, GROUP_HEADS, 1)
    alog = a_log.reshape(SSM_GROUPS, GROUP_HEADS, 1)
    gps = SSD_GROUPS_PER_STEP
    xs_blk, z_blk = NAT_XS // (gps * gch), NAT_Z // (gps * gch)
    b_blk, c_blk = NAT_B // (gps * st), NAT_C // (gps * st)
    return pl.pallas_call(
        functools.partial(_ssd_kernel, n_chunks=seq // CHUNK),
        out_shape=jax.ShapeDtypeStruct((nb * seq, SSM_INNER), BF16),
        grid=(nb, SSM_GROUPS // gps),
        in_specs=[
            pl.BlockSpec((seq, gps * gch), lambda b, g: (b, xs_blk + g)),
            pl.BlockSpec((seq, gps * gch), lambda b, g: (b, z_blk + g)),
            pl.BlockSpec((seq, gps * st), lambda b, g: (b, b_blk + g)),
            pl.BlockSpec((seq, gps * st), lambda b, g: (b, c_blk + g)),
            pl.BlockSpec((gps, None, seq // CHUNK, GROUP_HEADS, CHUNK),
                         lambda b, g: (g, b, 0, 0, 0)),
            pl.BlockSpec((CHUNK, gps * gch), lambda b, g: (0, xs_blk + g)),
            pl.BlockSpec((CHUNK, gps * st), lambda b, g: (0, b_blk + g)),
            pl.BlockSpec((CHUNK, gps * st), lambda b, g: (0, c_blk + g)),
            pl.BlockSpec((gps, GROUP_HEADS, CHUNK), lambda b, g: (g, 0, 0)),
            pl.BlockSpec((gps, CONV_WIDTH, stage_w), lambda b, g: (g, 0, 0)),
            pl.BlockSpec((gps, 1, stage_w), lambda b, g: (g, 0, 0)),
            pl.BlockSpec((gps, GROUP_HEADS, 1), lambda b, g: (g, 0, 0)),
            pl.BlockSpec((gps, GROUP_HEADS, 1), lambda b, g: (g, 0, 0)),
            pl.BlockSpec((1, gps * gch), lambda b, g: (0, g)),
            pl.BlockSpec((1, gps * gch), lambda b, g: (0, g)),
            pl.BlockSpec((LANES, gch), lambda b, g: (0, 0)),
            pl.BlockSpec((CHUNK, CHUNK), lambda b, g: (0, 0)),
            pl.BlockSpec((CONV_WIDTH * CHUNK, STAGE_TAIL + CHUNK), lambda b, g: (0, 0)),
        ],
        out_specs=pl.BlockSpec((seq, gps * gch), lambda b, g: (b, g)),
        scratch_shapes=[pltpu.VMEM((gps, SSM_STATE, gch), F32),
                        pltpu.VMEM((gps, seq // CHUNK * GROUP_HEADS, CHUNK), F32),
                        pltpu.VMEM((gps, seq // CHUNK * GROUP_HEADS, CHUNK), F32)],
        compiler_params=_cparams(("parallel", "arbitrary")),
        name="ssd",
    )(nat, nat, nat, nat, dtt, nat_meta, nat_meta, nat_meta, dtt_meta,
      cw, cb, dtb, alog, dsk, ng,
      jnp.asarray(expand, BF16), jnp.asarray(tri), jnp.asarray(shift, BF16))


def _merge_kernel(attn_ref, ssm_ref, g0_ref, g1_ref, x_ref, woa_ref, wos_ref, wout_ref,
                  n2g_ref, wrh_ref, wrl_ref, br_ref, ltri_ref,
                  h1_ref, hn2_ref, ti_ref, tw_ref, hist_ref):
    a = jnp.dot(attn_ref[...], woa_ref[...], preferred_element_type=F32)
    s = jnp.dot(ssm_ref[...], wos_ref[...], preferred_element_type=F32)
    g0 = _sigmoid(g0_ref[...].astype(F32))
    g1 = _sigmoid(g1_ref[...].astype(F32))
    u = g0 * a + g1 * s
    h1 = x_ref[...] + jnp.dot(u.astype(BF16), wout_ref[...], preferred_element_type=F32)
    h1_ref[...] = h1
    hn2 = _rms_rows(h1, n2g_ref[...])
    hn2_hi = hn2.astype(BF16)
    hn2_ref[...] = hn2_hi
    hn2_lo = (hn2 - hn2_hi.astype(F32)).astype(BF16)
    logits = (jnp.dot(hn2_hi, wrh_ref[...], preferred_element_type=F32)
              + jnp.dot(hn2_lo, wrh_ref[...], preferred_element_type=F32)
              + jnp.dot(hn2_hi, wrl_ref[...], preferred_element_type=F32)
              + br_ref[...])
    lane = lax.broadcasted_iota(I32, logits.shape, 1)
    work = logits
    ids, vals = [], []
    for _ in range(TOP_K):
        mx = jnp.max(work, axis=1, keepdims=True)
        idx = jnp.min(jnp.where(work == mx, lane, LANES), axis=1, keepdims=True)
        ids.append(idx)
        vals.append(mx)
        work = jnp.where(lane == idx, -jnp.inf, work)
    es = [jnp.exp(v - vals[0]) for v in vals]
    den = es[0] + es[1] + es[2] + es[3]
    routed = jnp.zeros(logits.shape, F32)
    for k in range(TOP_K):
        routed = jnp.where(lane == ids[k], 1.0, routed)
    before = jnp.dot(ltri_ref[...], routed.astype(BF16), preferred_element_type=F32)
    hist_ref[...] = jnp.broadcast_to(jnp.sum(routed, axis=0, keepdims=True),
                                     hist_ref.shape).astype(I32)
    ti = jnp.zeros(logits.shape, I32)
    tw = jnp.zeros(logits.shape, F32)
    for k in range(TOP_K):
        rank = jnp.sum(jnp.where(lane == ids[k], before, 0.0), axis=1, keepdims=True)
        ti = jnp.where(lane == k, ids[k], ti)
        ti = jnp.where(lane == TOP_K + k, rank.astype(I32), ti)
        tw = jnp.where(lane == k, es[k] / den, tw)
    ti_ref[...] = ti
    tw_ref[...] = tw


def _merge(attn, ssm, nat, x, woa, wos, wout, n2g, wr_hi, wr_lo, br, *, tm):
    n = x.shape[0]
    gblk = NAT_G // D_MODEL
    const = lambda shape: pl.BlockSpec(shape, lambda i: (0, 0))
    ltri = jnp.asarray(np.tril(np.ones((tm, tm), np.float32), -1), BF16)
    return pl.pallas_call(
        _merge_kernel,
        out_shape=(jax.ShapeDtypeStruct((n, D_MODEL), F32),
                   jax.ShapeDtypeStruct((n, D_MODEL), BF16),
                   jax.ShapeDtypeStruct((n, LANES), I32),
                   jax.ShapeDtypeStruct((n, LANES), F32),
                   jax.ShapeDtypeStruct((n // tm * 8, LANES), I32)),
        grid=(n // tm,),
        in_specs=[
            pl.BlockSpec((tm, ATTN_WIDTH), lambda i: (i, 0)),
            pl.BlockSpec((tm, SSM_INNER), lambda i: (i, 0)),
            pl.BlockSpec((tm, D_MODEL), lambda i: (i, gblk)),
            pl.BlockSpec((tm, D_MODEL), lambda i: (i, gblk + 1)),
            pl.BlockSpec((tm, D_MODEL), lambda i: (i, 0)),
            const((ATTN_WIDTH, D_MODEL)), const((SSM_INNER, D_MODEL)), const((D_MODEL, D_MODEL)),
            const((1, D_MODEL)), const((D_MODEL, LANES)), const((D_MODEL, LANES)),
            const((1, LANES)), const((tm, tm)),
        ],
        out_specs=(pl.BlockSpec((tm, D_MODEL), lambda i: (i, 0)),
                   pl.BlockSpec((tm, D_MODEL), lambda i: (i, 0)),
                   pl.BlockSpec((tm, LANES), lambda i: (i, 0)),
                   pl.BlockSpec((tm, LANES), lambda i: (i, 0)),
                   pl.BlockSpec((8, LANES), lambda i: (i, 0))),
        compiler_params=_cparams(("parallel",)),
        name="merge",
    )(attn, ssm, nat, nat, x, woa, wos, wout, n2g, wr_hi, wr_lo, br, ltri)


def _moe_kernel(te_ref, nv_ref, x_ref, wgu_ref, bgu_ref, wd_ref, bd_ref, o_ref,
                wgu_bf_ref, wd_bf_ref):
    t = pl.program_id(0)
    live = t < nv_ref[0]

    @pl.when(live & ((t == 0) | (te_ref[t] != te_ref[jnp.maximum(t - 1, 0)])))
    def _():
        wgu_bf_ref[...] = wgu_ref[...].astype(BF16)
        wd_bf_ref[...] = wd_ref[...].astype(BF16)

    @pl.when(live)
    def _():
        gu = jnp.dot(x_ref[...], wgu_bf_ref[...], preferred_element_type=F32) + bgu_ref[...]
        gate = jnp.minimum(gu[:, :EXPERT_FF], SWIGLU_LIMIT)
        up = jnp.clip(gu[:, EXPERT_FF:], -SWIGLU_LIMIT, SWIGLU_LIMIT)
        act = (up + 1.0) * (gate * _sigmoid(SWIGLU_ALPHA * gate))
        y = jnp.dot(act.astype(BF16), wd_bf_ref[...], preferred_element_type=F32) + bd_ref[...]
        o_ref[...] = y.astype(o_ref.dtype)

    @pl.when(t >= nv_ref[0])
    def _():
        o_ref[...] = jnp.zeros_like(o_ref)


def _moe(tile_expert, n_valid, x_sorted, wgu, bgu, wd, bd):
    rows = x_sorted.shape[0]
    n_tiles = rows // MOE_TM
    grid_spec = pltpu.PrefetchScalarGridSpec(
        num_scalar_prefetch=2,
        grid=(n_tiles,),
        in_specs=[
            pl.BlockSpec((MOE_TM, D_MODEL), lambda t, te, nv: (t, 0)),
            pl.BlockSpec((None, D_MODEL, 2 * EXPERT_FF), lambda t, te, nv: (te[t], 0, 0)),
            pl.BlockSpec((None, 1, 2 * EXPERT_FF), lambda t, te, nv: (te[t], 0, 0)),
            pl.BlockSpec((None, EXPERT_FF, D_MODEL), lambda t, te, nv: (te[t], 0, 0)),
            pl.BlockSpec((None, 1, D_MODEL), lambda t, te, nv: (te[t], 0, 0)),
        ],
        out_specs=pl.BlockSpec((MOE_TM, D_MODEL), lambda t, te, nv: (t, 0)),
        scratch_shapes=[pltpu.VMEM((D_MODEL, 2 * EXPERT_FF), BF16),
                        pltpu.VMEM((EXPERT_FF, D_MODEL), BF16)],
    )
    return pl.pallas_call(
        _moe_kernel,
        out_shape=jax.ShapeDtypeStruct((rows, D_MODEL), BF16),
        grid_spec=grid_spec,
        compiler_params=_cparams(("arbitrary",)),
        name="moe",
    )(tile_expert, n_valid, x_sorted, wgu, bgu, wd, bd)


def _final_kernel(h1_ref, y0_ref, y1_ref, y2_ref, y3_ref, tw_ref, fg_ref, o_ref):
    tw = tw_ref[...]
    h = h1_ref[...]
    for k, y_ref in enumerate((y0_ref, y1_ref, y2_ref, y3_ref)):
        h = h + tw[:, k:k + 1] * y_ref[...].astype(F32)
    o_ref[...] = _rms_rows(h, fg_ref[...])


def _final(h1, ys, tw, fg, *, tm):
    n = h1.shape[0]
    row = lambda w: pl.BlockSpec((tm, w), lambda i: (i, 0))
    return pl.pallas_call(
        _final_kernel,
        out_shape=jax.ShapeDtypeStruct((n, D_MODEL), F32),
        grid=(n // tm,),
        in_specs=[row(D_MODEL)] * (1 + TOP_K)
                 + [row(LANES), pl.BlockSpec((1, D_MODEL), lambda i: (0, 0))],
        out_specs=row(D_MODEL),
        compiler_params=_cparams(("parallel",)),
        name="final",
    )(h1, *ys, tw, fg)


def _split_w_in(w_in):
    offs = np.cumsum((0,) + IN_SPLITS)
    names = ("q", "k", "v", "qi", "ki", "wi", "z", "xs", "b", "c", "dt", "gate")
    return {nm: w_in[:, offs[i]:offs[i + 1]] for i, nm in enumerate(names)}


def _expert_dispatch(top_i, rank, hist, n_rows_pad):
    n = top_i.shape[0]
    pairs = n * TOP_K
    counts = jnp.sum(hist, axis=0)
    grp_start = jnp.cumsum(counts) - counts
    tiles_per_e = (counts + MOE_TM - 1) // MOE_TM
    tile_end = jnp.cumsum(tiles_per_e)
    row_start = (tile_end - tiles_per_e) * MOE_TM
    offset = row_start - grp_start
    base = row_start[None, :] + jnp.cumsum(hist, axis=0) - hist
    base_tok = jnp.repeat(base, n // hist.shape[0], axis=0)
    hit = top_i[:, :, None] == jnp.arange(N_EXPERTS, dtype=I32)[None, None, :]
    dest = jnp.sum(jnp.where(hit, base_tok[:, None, :], 0), axis=2) + rank
    token_of_pair = jnp.broadcast_to(jnp.arange(n, dtype=I32)[:, None], (n, TOP_K))
    _, order_tok = lax.sort_key_val(dest.reshape(-1), token_of_pair.reshape(-1))
    n_tiles = n_rows_pad // MOE_TM
    n_valid = tile_end[-1].astype(I32)
    tile_ids = jnp.minimum(jnp.arange(n_tiles, dtype=I32), n_valid - 1)
    tile_expert = jnp.searchsorted(tile_end, tile_ids, side="right",
                                   method="compare_all").astype(I32)
    tile_expert = jnp.minimum(tile_expert, N_EXPERTS - 1)
    rows = jnp.arange(n_rows_pad, dtype=I32)
    e_row = jnp.repeat(tile_expert, MOE_TM)
    s_row = rows - offset[e_row]
    live = (s_row >= grp_start[e_row]) & (s_row < (grp_start + counts)[e_row])
    src_token = jnp.where(live, order_tok[jnp.clip(s_row, 0, pairs - 1)], rows % n)
    return src_token, dest, tile_expert, n_valid.reshape(1)


def kernel(x, meta_tokens, norm1_g, w_in, conv_w, conv_b, dt_bias, a_log, d_skip, ssm_norm_g,
           w_o_attn, w_o_ssm, w_out, norm2_g, w_router, b_router, w_gate_up, b_gate_up,
           w_down, b_down, final_g):
    nb, seq, d = x.shape
    n = nb * seq
    assert d == D_MODEL and seq % TQ == 0 and w_in.shape[0] == 1
    xf = x.reshape(n, d)
    meta_pad = jnp.concatenate([jnp.zeros((META_PAD, d), x.dtype), meta_tokens.astype(x.dtype)], 0)
    g1 = norm1_g[0].reshape(1, d)

    w = _split_w_in(w_in[0])
    w_nat = jnp.concatenate([w["k"], w["z"], w["xs"], w["b"], w["c"], w["gate"]], 1).astype(BF16)
    w_qvt = jnp.concatenate([w["q"], w["v"]], 1).T.astype(BF16)
    zpad = lambda c: jnp.zeros((d, c), F32)
    w_sn = jnp.concatenate([w["ki"], zpad(SN_DT - IDX_DIM), w["dt"],
                            zpad(SN_W - SN_DT - SSM_HEADS)], 1).astype(BF16)
    w_st = jnp.concatenate([w["qi"], w["wi"], zpad(ST_W - ST_WI - IDX_HEADS)], 1).T.astype(BF16)

    tabs = _rope_tables(N_META + jnp.arange(seq, dtype=I32))
    tabs_t = tuple(t.T for t in tabs)
    pos_meta = jnp.maximum(jnp.arange(CHUNK, dtype=I32) - META_PAD, 0)
    tabs_m = _rope_tables(pos_meta)
    tabs_mt = tuple(t.T for t in tabs_m)

    tm_nat = 1024 if seq % 1024 == 0 else TQ
    w_t = jnp.concatenate([w_qvt, w_st], axis=0)
    q_slabs = ATTN_WIDTH // LANES
    tr_rope = tuple(range(q_slabs)) + tuple(
        2 * q_slabs + s for s in range(IDX_HEADS * IDX_DIM // LANES))
    nat, sm_n = _proj_nat(xf, g1, w_nat, w_sn, tabs, tm=tm_nat, tn=1024, rope_tiles=1,
                          seq_tiles=seq // tm_nat, name="proj_nat")
    qvt, sm_t = _proj_tr(xf, g1, w_t, tabs_t, tm=TQ, nb=nb, width_main=2 * ATTN_WIDTH,
                         rope_slabs=tr_rope, name="proj_t")
    nat_m, sm_n_m = _proj_nat(meta_pad, g1, w_nat, w_sn, tabs_m, tm=CHUNK, tn=1024, rope_tiles=1,
                              seq_tiles=1, name="proj_nat_meta")
    qvt_m, _ = _proj_tr(meta_pad, g1, w_t, tabs_mt, tm=CHUNK, nb=1, width_main=2 * ATTN_WIDTH,
                        rope_slabs=tr_rope, name="proj_t_meta")

    vt_meta = qvt_m[0, 0, ATTN_WIDTH:, CHUNK - ATTN_META_ROWS:]
    attn = _attention(qvt, nat, sm_t, sm_n, nat_m, vt_meta, nb=nb, seq=seq)

    dt_raw = sm_n[:, SN_DT:SN_DT + SSM_HEADS].reshape(nb, seq // CHUNK, CHUNK, SSM_GROUPS,
                                                      GROUP_HEADS)
    dtt = jnp.transpose(dt_raw, (3, 0, 1, 4, 2))
    dt_raw_m = sm_n_m[:, SN_DT:SN_DT + SSM_HEADS].reshape(CHUNK, SSM_GROUPS, GROUP_HEADS)
    dtt_m = jnp.transpose(dt_raw_m, (1, 2, 0))
    ssm = _ssd(nat, nat_m, dtt, dtt_m, conv_w[0], conv_b[0], dt_bias[0], a_log[0], d_skip[0],
               ssm_norm_g[0], nb=nb, seq=seq)

    wr = jnp.concatenate([w_router[0], jnp.zeros((d, LANES - N_EXPERTS), F32)], 1)
    br = jnp.concatenate([b_router[0], jnp.full((LANES - N_EXPERTS,), -1e30, F32)]).reshape(1, LANES)
    wr_hi = wr.astype(BF16)
    wr_lo = (wr - wr_hi.astype(F32)).astype(BF16)
    tm_merge = 512 if n % 512 == 0 else TQ
    h1, hn2, ti, tw, hist = _merge(attn, ssm, nat, xf, w_o_attn[0].astype(BF16),
                                   w_o_ssm[0].astype(BF16), w_out[0].astype(BF16),
                                   norm2_g[0].reshape(1, d), wr_hi, wr_lo, br, tm=tm_merge)

    n_tiles = n * TOP_K // MOE_TM + N_EXPERTS
    src_token, pos, tile_expert, n_valid = _expert_dispatch(
        ti[:, :TOP_K], ti[:, TOP_K:2 * TOP_K], hist[::8, :N_EXPERTS], n_tiles * MOE_TM)
    x_sorted = hn2.at[src_token].get(mode="promise_in_bounds")
    y_sorted = _moe(tile_expert, n_valid, x_sorted, w_gate_up[0],
                    b_gate_up[0].reshape(N_EXPERTS, 1, -1), w_down[0],
                    b_down[0].reshape(N_EXPERTS, 1, -1))
    ys = [y_sorted.at[pos[:, k]].get(mode="promise_in_bounds") for k in range(TOP_K)]

    out = _final(h1, ys, tw, final_g.reshape(1, d), tm=512 if n % 512 == 0 else TQ)
    return out.reshape(nb, seq, d)
```

```python
import functools
import math

import numpy as np
import jax
import jax.numpy as jnp
from jax import lax
from jax.experimental import pallas as pl
from jax.experimental.pallas import tpu as pltpu

F32 = jnp.float32
BF16 = jnp.bfloat16
I32 = jnp.int32
I16 = jnp.int16

D_MODEL = 1024
N_META = 16
RMS_EPS = 1e-6
N_HEADS = 16
HEAD_DIM = 64
ATTN_WIDTH = N_HEADS * HEAD_DIM
ROT_DIM = HEAD_DIM // 4
ROPE_THETA = 500000.0
IDX_HEADS = 8
IDX_DIM = 64
TOPK_KEYS_MAX = 256
SSM_INNER = 2 * D_MODEL
SSM_HEAD_DIM = 64
SSM_HEADS = SSM_INNER // SSM_HEAD_DIM
SSM_GROUPS = 4
SSM_STATE = 128
CONV_WIDTH = 4
CHUNK = 128
N_EXPERTS = 32
TOP_K = 4
EXPERT_FF = D_MODEL
SWIGLU_LIMIT = 7.0
SWIGLU_ALPHA = 1.702
IN_SPLITS = (ATTN_WIDTH, ATTN_WIDTH, ATTN_WIDTH, IDX_HEADS * IDX_DIM, IDX_DIM, IDX_HEADS,
             SSM_INNER, SSM_INNER, SSM_GROUPS * SSM_STATE, SSM_GROUPS * SSM_STATE, SSM_HEADS,
             2 * D_MODEL)

LANES = 128
GROUP_CH = SSM_INNER // SSM_GROUPS
GROUP_HEADS = SSM_HEADS // SSM_GROUPS
META_PAD = CHUNK - N_META
TQ = 256
ATTN_META_ROWS = 32
FLASH_LAG = 8
MOE_TM = 512
INT_MIN = -2147483648
VMEM_LIMIT = 56 * 1024 * 1024

NAT_K, NAT_Z, NAT_XS, NAT_B, NAT_C, NAT_G = 0, 1024, 3072, 5120, 5632, 6144
NAT_W = 8192
NAT_TILE = 1024


def _nat_spec(rows, width, col, row_block):
    assert NAT_TILE % width == 0

    def index(*grid):
        c = col(*grid)
        return c // NAT_TILE, row_block(*grid), (c % NAT_TILE) // width

    return pl.BlockSpec((None, rows, width), index)
SN_KI, SN_DT, SN_W = 0, 128, 256
ST_QI, ST_WI, ST_W = 0, 512, 640


def _cparams(sem):
    return pltpu.CompilerParams(dimension_semantics=sem, vmem_limit_bytes=VMEM_LIMIT)


def _rms_rows(x, g):
    ms = jnp.mean(x * x, axis=-1, keepdims=True)
    return x * lax.rsqrt(ms + RMS_EPS) * g


def _proj_nat_kernel(x_ref, g_ref, w_ref, wsm_ref, cos_ref, sa_ref, sb_ref, o_ref, osm_ref,
                     hn_ref, *, rope_tiles):
    j = pl.program_id(1)

    def rope(a):
        return (a * cos_ref[...] + pltpu.roll(a, 8, 1) * sa_ref[...]
                + pltpu.roll(a, LANES - 8, 1) * sb_ref[...])

    @pl.when(j == 0)
    def _():
        hn = _rms_rows(x_ref[...], g_ref[...]).astype(hn_ref.dtype)
        hn_ref[...] = hn
        small = jnp.dot(hn, wsm_ref[...], preferred_element_type=F32)
        osm_ref[:, 0:LANES] = rope(small[:, 0:LANES])
        osm_ref[:, LANES:] = small[:, LANES:]

    acc = jnp.dot(hn_ref[...], w_ref[...], preferred_element_type=F32)

    @pl.when(j < rope_tiles)
    def _():
        for s in range(acc.shape[1] // LANES):
            cols = slice(s * LANES, (s + 1) * LANES)
            o_ref[:, cols] = rope(acc[:, cols]).astype(o_ref.dtype)

    @pl.when(j >= rope_tiles)
    def _():
        o_ref[...] = acc.astype(o_ref.dtype)


def _proj_nat(x, g, w_tiles, w_small, tabs, *, tm, rope_tiles, seq_tiles, name):
    n, d = x.shape
    n_col, _, tn = w_tiles.shape
    width_sm = w_small.shape[1]
    cos, sa, sb = tabs
    tab_spec = pl.BlockSpec((tm, LANES), lambda i, j: (i % seq_tiles, 0))
    return pl.pallas_call(
        functools.partial(_proj_nat_kernel, rope_tiles=rope_tiles),
        out_shape=(jax.ShapeDtypeStruct((n_col, n, tn), BF16),
                   jax.ShapeDtypeStruct((n, width_sm), F32)),
        grid=(n // tm, n_col),
        in_specs=[
            pl.BlockSpec((tm, d), lambda i, j: (i, 0)),
            pl.BlockSpec((1, d), lambda i, j: (0, 0)),
            pl.BlockSpec((None, d, tn), lambda i, j: (j, 0, 0)),
            pl.BlockSpec((d, width_sm), lambda i, j: (0, 0)),
            tab_spec, tab_spec, tab_spec,
        ],
        out_specs=(pl.BlockSpec((None, tm, tn), lambda i, j: (j, i, 0)),
                   pl.BlockSpec((tm, width_sm), lambda i, j: (i, 0))),
        scratch_shapes=[pltpu.VMEM((tm, d), BF16)],
        compiler_params=_cparams(("parallel", "arbitrary")),
        name=name,
    )(x, g, w_tiles, w_small, cos, sa, sb)


def _proj_tr_kernel(x_ref, g_ref, wt_ref, cos_ref, sa_ref, sb_ref, o_ref, osm_ref, *,
                    rope_slabs):
    hn = _rms_rows(x_ref[...], g_ref[...]).astype(BF16)
    acc = lax.dot_general(wt_ref[...], hn, (((1,), (1,)), ((), ())),
                          preferred_element_type=F32)
    cos, sa, sb = cos_ref[...], sa_ref[...], sb_ref[...]
    n_main = o_ref.shape[0] // LANES
    for s in range(acc.shape[0] // LANES):
        a = acc[s * LANES:(s + 1) * LANES, :]
        if s in rope_slabs:
            a = a * cos + pltpu.roll(a, 8, 0) * sa + pltpu.roll(a, LANES - 8, 0) * sb
        if s < n_main:
            o_ref[s * LANES:(s + 1) * LANES, :] = a.astype(o_ref.dtype)
        else:
            osm_ref[(s - n_main) * LANES:(s - n_main + 1) * LANES, :] = a


def _proj_tr(x, g, wt, tabs_t, *, tm, nb, width_main, rope_slabs, name):
    n, d = x.shape
    width = wt.shape[0]
    width_sm = width - width_main
    seq_tiles = n // nb // tm
    cos, sa, sb = tabs_t
    tab_spec = pl.BlockSpec((LANES, tm), lambda i: (0, i % seq_tiles))
    out_map = lambda i: (i // seq_tiles, i % seq_tiles, 0, 0)
    return pl.pallas_call(
        functools.partial(_proj_tr_kernel, rope_slabs=rope_slabs),
        out_shape=(jax.ShapeDtypeStruct((nb, seq_tiles, width_main, tm), BF16),
                   jax.ShapeDtypeStruct((nb, seq_tiles, width_sm, tm), F32)),
        grid=(n // tm,),
        in_specs=[
            pl.BlockSpec((tm, d), lambda i: (i, 0)),
            pl.BlockSpec((1, d), lambda i: (0, 0)),
            pl.BlockSpec((width, d), lambda i: (0, 0)),
            tab_spec, tab_spec, tab_spec,
        ],
        out_specs=(pl.BlockSpec((None, None, width_main, tm), out_map),
                   pl.BlockSpec((None, None, width_sm, tm), out_map)),
        compiler_params=_cparams(("parallel",)),
        name=name,
    )(x, g, wt, cos, sa, sb)


def _rope_tables(pos):
    half = ROT_DIM // 2
    inv_freq = jnp.exp(-math.log(ROPE_THETA) * jnp.arange(half, dtype=F32) / half)
    ang = pos.astype(F32)[:, None] * inv_freq[None, :]
    cos8, sin8 = jnp.cos(ang), jnp.sin(ang)
    t = pos.shape[0]
    cc = np.arange(LANES) % HEAD_DIM
    in_rot = jnp.asarray(cc < ROT_DIM)
    lo = jnp.asarray(cc < half)
    hi = jnp.asarray((cc >= half) & (cc < ROT_DIM))
    idx = jnp.asarray(cc % half)
    cos_l = jnp.take(cos8, idx, axis=1)
    sin_l = jnp.take(sin8, idx, axis=1)
    cos = jnp.where(in_rot[None, :], cos_l, 1.0)
    sa = jnp.where(hi[None, :], sin_l, 0.0)
    sb = jnp.where(lo[None, :], -sin_l, 0.0)
    del t
    return cos, sa, sb


def _attn_kernel(qt_ref, vt_ref, k_ref, qit_ref, wit_ref, ki_ref, kmeta_ref, vtmeta_ref,
                 o_ref, keys_ref, hi_ref, lo_ref, bias_ref, qm_ref, m_ref, l_ref, acc_ref, s_ref,
                 *, ksel, nbits):
    j = pl.program_id(1)
    tq = TQ
    nch = j + 1
    int_min = jnp.int32(INT_MIN)
    neg_inf = jnp.float32(-jnp.inf)

    row128 = lax.broadcasted_iota(I32, (LANES, tq), 0)
    for h in range(N_HEADS):
        slab = qt_ref[(h // 2) * LANES:(h // 2 + 1) * LANES, :]
        keep = (row128 < HEAD_DIM) if h % 2 == 0 else (row128 >= HEAD_DIM)
        scaled = slab.astype(F32) * (HEAD_DIM ** -0.5 * math.log2(math.e))
        qm_ref[h] = jnp.where(keep, scaled, 0.0).astype(qm_ref.dtype)

    w_all = wit_ref[0:IDX_HEADS, :] * (IDX_HEADS ** -0.5 * IDX_DIM ** -0.5)
    qi_bf = [qit_ref[h * IDX_DIM:(h + 1) * IDX_DIM, :].astype(BF16) for h in range(IDX_HEADS)]
    qidx = j * tq + lax.broadcasted_iota(I32, (tq, tq), 1)
    krow = lax.broadcasted_iota(I32, (tq, tq), 0)

    def score_body(c, carry):
        start = pl.multiple_of(c * tq, tq)
        kic = ki_ref[pl.ds(start, tq), :][:, :IDX_DIM].astype(BF16)
        sc = jnp.zeros((tq, tq), F32)
        for h in range(IDX_HEADS):
            lg = jnp.dot(kic, qi_bf[h], preferred_element_type=F32)
            sc = sc + jnp.maximum(lg, 0.0) * w_all[h:h + 1, :]
        bits = pltpu.bitcast(sc, I32)
        skey = jnp.where(bits < 0, bits ^ jnp.int32(0x7FFFFFFF), bits)
        skey = jnp.where(c * tq + krow <= qidx, skey, int_min)
        keys_ref[c] = skey
        hi_ref[c] = (skey >> 16).astype(I16)
        lo_ref[c] = ((skey & 0xFFFF) - 32768).astype(I16)
        return carry

    lax.fori_loop(0, nch, score_body, 0)

    @pl.when(nch % 2 == 1)
    def _():
        hi_ref[nch] = jnp.full((tq, tq), -32768, I16)
        lo_ref[nch] = jnp.full((tq, tq), -32768, I16)

    npair = (nch + 1) // 2

    def count(pred):
        def body(c, cnt):
            hit = pred(keys_ref[c], c).astype(I32)
            return cnt + hit.reshape(tq // 8, 8, tq).sum(axis=0)
        cnt8 = lax.fori_loop(0, nch, body, jnp.zeros((8, tq), I32))
        return jnp.sum(cnt8, axis=0, keepdims=True)

    def count16(ref, cand):
        cand16 = cand.astype(I16)
        one, zero = jnp.int16(1), jnp.int16(0)

        def body(i, cnt):
            for c in (2 * i, 2 * i + 1):
                hit = pltpu.bitcast(jnp.where(ref[c] >= cand16, one, zero), I32)
                cnt = cnt + hit.reshape(tq // 16, 8, tq).sum(axis=0)
            return cnt

        cnt8 = lax.fori_loop(0, npair, body, jnp.zeros((8, tq), I32))
        both = jnp.sum(cnt8, axis=0, keepdims=True)
        return (both & 0xFFFF) + lax.shift_right_logical(both, 16)

    def search16(ref, want):
        def bit_body(it, t_b):
            cand_b = t_b | lax.shift_left(jnp.int32(1), 15 - it)
            cnt = count16(ref, cand_b - 32768)
            return jnp.where(cnt >= want, cand_b, t_b)
        return lax.fori_loop(0, 16, bit_body, jnp.zeros((1, tq), I32))

    hi_b = search16(hi_ref, ksel)
    thr_hi = hi_b - 32768
    n_above = count16(hi_ref, jnp.minimum(thr_hi + 1, 32767))
    n_above = jnp.where(thr_hi >= 32767, 0, n_above)
    thr_hi16 = thr_hi.astype(I16)

    def bucket_body(c, carry):
        lo_ref[c] = jnp.where(hi_ref[c] == thr_hi16, lo_ref[c], jnp.int16(-32768))
        return carry

    lax.fori_loop(0, 2 * npair, bucket_body, 0)
    lo_b = search16(lo_ref, ksel - n_above)
    thr = lax.shift_left(thr_hi, 16) | lo_b
    n_gt = count(lambda kc, c: kc > thr)
    n_eq = count(lambda kc, c: kc == thr)
    need = ksel - n_gt
    fix = (n_eq > need) & (thr != int_min)
    thr_eff = jnp.maximum(thr, int_min + 1)

    def bias_body(c, carry):
        bias_ref[c] = jnp.where(keys_ref[c] >= thr_eff, 0.0, neg_inf)
        return carry

    lax.fori_loop(0, nch, bias_body, 0)

    @pl.when(jnp.max(fix.astype(I32)) > 0)
    def _():
        def idx_body(it, v):
            cand = v | lax.shift_left(jnp.int32(1), nbits - 1 - it)
            cnt = count(lambda kc, c: jnp.where(c * tq + krow < cand, kc, int_min) == thr)
            return jnp.where(cnt <= need - 1, cand, v)

        last = lax.fori_loop(0, nbits, idx_body, jnp.zeros((1, tq), I32))

        def fix_body(c, carry):
            kc = keys_ref[c]
            tie_lim = jnp.where(fix, last, jnp.int32(2 ** 30))
            tie_ok = jnp.where(c * tq + krow <= tie_lim, thr_eff, thr_eff + 1)
            bias_ref[c] = jnp.where(kc >= tie_ok, 0.0, neg_inf)
            return carry

        lax.fori_loop(0, nch, fix_body, 0)

    def chunk_step(kslab_of, vt_of, bias_c, n_keys, first):
        alphas = []
        ones = jnp.ones((16, n_keys), BF16)

        def scores(h):
            s = jnp.dot(kslab_of(h), qm_ref[h], preferred_element_type=F32) + bias_c
            s_ref[h, 0:n_keys, :] = s
            smax = jnp.max(s, axis=0, keepdims=True)
            if first:
                m_ref[h:h + 1, :] = smax
            else:
                m_old = m_ref[h:h + 1, :]
                m_new = jnp.maximum(m_old, smax)
                alphas.append(jnp.exp2(m_old - m_new))
                m_ref[h:h + 1, :] = m_new

        def values(h):
            p = jnp.exp2(s_ref[h, 0:n_keys, :] - m_ref[h:h + 1, :]).astype(BF16)
            res = jnp.dot(jnp.concatenate([vt_of(h), ones], axis=0), p,
                          preferred_element_type=F32)
            pv, psum = res[:HEAD_DIM, :], res[HEAD_DIM:HEAD_DIM + 1, :]
            rows = slice(h * HEAD_DIM, (h + 1) * HEAD_DIM)
            if first:
                l_ref[h:h + 1, :] = psum
                acc_ref[rows, :] = pv
            else:
                l_ref[h:h + 1, :] = l_ref[h:h + 1, :] * alphas[h] + psum
                acc_ref[rows, :] = acc_ref[rows, :] * alphas[h] + pv

        for h in range(N_HEADS + FLASH_LAG):
            if h < N_HEADS:
                scores(h)
            if h >= FLASH_LAG:
                values(h - FLASH_LAG)

    meta_row = lax.broadcasted_iota(I32, (ATTN_META_ROWS, tq), 0)
    bias_meta = jnp.where(meta_row >= ATTN_META_ROWS - N_META, 0.0, neg_inf)
    chunk_step(lambda h: kmeta_ref[:, (h // 2) * LANES:(h // 2 + 1) * LANES],
               lambda h: vtmeta_ref[h * HEAD_DIM:(h + 1) * HEAD_DIM, :], bias_meta,
               ATTN_META_ROWS, True)

    def flash_body(c, carry):
        start = pl.multiple_of(c * tq, tq)
        chunk_step(lambda h: k_ref[pl.ds(start, tq), (h // 2) * LANES:(h // 2 + 1) * LANES],
                   lambda h: vt_ref[c, h * HEAD_DIM:(h + 1) * HEAD_DIM, :], bias_ref[c], tq, False)
        return carry

    lax.fori_loop(0, nch, flash_body, 0)

    for h in range(N_HEADS):
        rows = slice(h * HEAD_DIM, (h + 1) * HEAD_DIM)
        acc_ref[rows, :] = acc_ref[rows, :] * (1.0 / l_ref[h:h + 1, :])
    o_ref[...] = acc_ref[...].T.astype(o_ref.dtype)


def _attention(qvt, nat, sm_t, sm_n, nat_meta, qvt_meta, *, nb, seq):
    nq = seq // TQ
    ksel = min(TOPK_KEYS_MAX, seq // 4)
    nbits = max(1, int(math.ceil(math.log2(seq))))
    kernel = functools.partial(_attn_kernel, ksel=ksel, nbits=nbits)
    return pl.pallas_call(
        kernel,
        out_shape=jax.ShapeDtypeStruct((nb * seq, ATTN_WIDTH), BF16),
        grid=(nb, nq),
        in_specs=[
            pl.BlockSpec((None, None, ATTN_WIDTH, TQ), lambda b, j: (b, j, 0, 0)),
            pl.BlockSpec((None, nq, ATTN_WIDTH, TQ), lambda b, j: (b, 0, 1, 0)),
            _nat_spec(seq, ATTN_WIDTH, lambda b, j: NAT_K, lambda b, j: b),
            pl.BlockSpec((None, None, IDX_HEADS * IDX_DIM, TQ), lambda b, j: (b, j, 0, 0)),
            pl.BlockSpec((None, None, LANES, TQ), lambda b, j: (b, j, ST_WI // LANES, 0)),
            pl.BlockSpec((seq, LANES), lambda b, j: (b, SN_KI // LANES)),
            _nat_spec(ATTN_META_ROWS, ATTN_WIDTH, lambda b, j: NAT_K,
                      lambda b, j: CHUNK // ATTN_META_ROWS - 1),
            pl.BlockSpec((ATTN_WIDTH, ATTN_META_ROWS), lambda b, j: (0, 0)),
        ],
        out_specs=pl.BlockSpec((TQ, ATTN_WIDTH), lambda b, j: (b * nq + j, 0)),
        scratch_shapes=[
            pltpu.VMEM((nq, TQ, TQ), I32),
            pltpu.VMEM((nq + nq % 2, TQ, TQ), I16),
            pltpu.VMEM((nq + nq % 2, TQ, TQ), I16),
            pltpu.VMEM((nq, TQ, TQ), F32),
            pltpu.VMEM((N_HEADS, LANES, TQ), BF16),
            pltpu.VMEM((N_HEADS, TQ), F32),
            pltpu.VMEM((N_HEADS, TQ), F32),
            pltpu.VMEM((ATTN_WIDTH, TQ), F32),
            pltpu.VMEM((N_HEADS, TQ, TQ), F32),
        ],
        compiler_params=_cparams(("parallel", "arbitrary")),
        name="attn",
    )(qvt, qvt, nat, sm_t, sm_t, sm_n, nat_meta, qvt_meta)


def _softplus(x):
    return jnp.maximum(x, 0.0) + jnp.log1p(jnp.exp(-jnp.abs(x)))


def _sigmoid(x):
    return 0.5 + 0.5 * jnp.tanh(0.5 * x)


def _silu(x):
    h = 0.5 * x
    return h + h * jnp.tanh(h)


def _pad_transpose(rows):
    padded = jnp.concatenate([rows, jnp.zeros((LANES - rows.shape[0], LANES), rows.dtype)], axis=0)
    return padded.T


STAGE_TAIL = 16
SSD_GROUPS_PER_STEP = 2


def _ssd_kernel(xs_ref, z_ref, b_ref, c_ref, dtt_ref,
                xs_m_ref, b_m_ref, c_m_ref, dtt_m_ref,
                cw_ref, cb_ref, dtb_ref, alog_ref, dsk_ref, ng_ref, exp_ref, tri_ref, shift_ref,
                o_ref, state_ref, dt_all_ref, cs_all_ref, *, n_chunks):
    gs = range(SSD_GROUPS_PER_STEP)
    a_neg = [-jnp.exp(alog_ref[g]) for g in gs]
    dt_bias = [dtb_ref[g] for g in gs]
    expand = exp_ref[...]
    tri_u = tri_ref[...]
    li = lax.broadcasted_iota(I32, (CHUNK, CHUNK), 0)
    si = lax.broadcasted_iota(I32, (CHUNK, CHUNK), 1)
    tril = si <= li
    lane = lax.broadcasted_iota(I32, (CHUNK, LANES), 1)
    low_half = lane < SSM_HEAD_DIM
    neg_inf = jnp.float32(-jnp.inf)

    xb, xc = GROUP_CH, GROUP_CH + SSM_STATE

    def cumsum_tokens(rows):
        return jnp.dot(rows, tri_u, precision=lax.Precision.HIGHEST, preferred_element_type=F32)

    def token_major(dt_r, cs_r):
        dt_c = _pad_transpose(dt_r)
        cs_c = _pad_transpose(cs_r)
        dec_c = jnp.exp(cs_c[CHUNK - 1:CHUNK, :] - cs_c)
        return cs_c, dt_c.astype(BF16), (dt_c * dec_c).astype(BF16), jnp.exp(cs_c).astype(BF16)

    widths = (GROUP_CH, SSM_STATE, SSM_STATE)
    gcols = lambda g: slice(g * GROUP_CH, (g + 1) * GROUP_CH)

    def pieces(refs, rows, g):
        return [r[rows, g * w:(g + 1) * w] for r, w in zip(refs, widths)]

    def chunk(src, rows, tail_src, trows, dt_of, out_rows, is_meta):
        stacks = []
        for g in gs:
            tail = ([jnp.zeros((STAGE_TAIL, w), BF16) for w in widths] if tail_src is None
                    else pieces(tail_src, trows, g))
            stacks.append(jnp.concatenate(
                [jnp.concatenate(tail, axis=1), jnp.concatenate(pieces(src, rows, g), axis=1)],
                axis=0))
        shifted = [jnp.dot(shift_ref[...], stacks[g], preferred_element_type=F32) for g in gs]
        x, bm_bf, cm_bf = [], [], []
        for g in gs:
            conv = cb_ref[g]
            for k in range(CONV_WIDTH):
                conv = conv + (shifted[g][k * CHUNK:(k + 1) * CHUNK, :]
                               * cw_ref[g, CONV_WIDTH - 1 - k:CONV_WIDTH - k, :])
            act = _silu(conv)
            x.append(act[:, 0:xb])
            bm_bf.append(act[:, xb:xc].astype(BF16))
            cm_bf.append(act[:, xc:].astype(BF16))
        if is_meta:
            dt_r = []
            for g in gs:
                d = _softplus(dt_of(g) + dt_bias[g])
                dt_r.append(jnp.where(lax.broadcasted_iota(I32, d.shape, 1) >= META_PAD, d, 0.0))
            cs_r = [cumsum_tokens(dt_r[g] * a_neg[g]) for g in gs]
        else:
            dt_r, cs_r = zip(*[dt_of(g) for g in gs])
        cs_c, dt_bf, dd_bf, ec_bf = zip(*[token_major(dt_r[g], cs_r[g]) for g in gs])
        dtx = [jnp.dot(dt_bf[g], expand, preferred_element_type=F32) for g in gs]
        decx = [jnp.dot(dd_bf[g], expand, preferred_element_type=F32) for g in gs]
        ecsx = [jnp.dot(ec_bf[g], expand, preferred_element_type=F32) for g in gs]
        state = [state_ref[g] for g in gs]
        new_part = [lax.dot_general(bm_bf[g], (x[g] * decx[g]).astype(BF16),
                                    (((0,), (0,)), ((), ())), preferred_element_type=F32)
                    for g in gs]
        if not is_meta:
            xdt = [(x[g] * dtx[g]).astype(BF16) for g in gs]
            cb = [lax.dot_general(cm_bf[g], bm_bf[g], (((1,), (1,)), ((), ())),
                                  preferred_element_type=F32) for g in gs]
            y_off = [jnp.dot(cm_bf[g], state[g].astype(BF16), preferred_element_type=F32)
                     * ecsx[g] for g in gs]
            slabs = [[] for _ in gs]
            for pp in range(GROUP_HEADS // 2):
                halves = [[] for _ in gs]
                for e in (2 * pp, 2 * pp + 1):
                    for g in gs:
                        diff = cs_c[g][:, e:e + 1] - cs_r[g][e:e + 1, :]
                        lmat = jnp.exp(jnp.where(tril, diff, neg_inf))
                        halves[g].append(jnp.dot((cb[g] * lmat).astype(BF16),
                                                 xdt[g][:, pp * LANES:(pp + 1) * LANES],
                                                 preferred_element_type=F32))
                for g in gs:
                    slabs[g].append(jnp.where(low_half, halves[g][0], halves[g][1]))
            for g in gs:
                y = jnp.concatenate(slabs[g], axis=1) + y_off[g] + x[g] * dsk_ref[:, gcols(g)]
                y = y * _silu(z_ref[rows, gcols(g)].astype(F32))
                ms = jnp.mean(y * y, axis=-1, keepdims=True)
                o_ref[out_rows, gcols(g)] = (y * lax.rsqrt(ms + RMS_EPS)
                                             * ng_ref[:, gcols(g)]).astype(o_ref.dtype)
        for g in gs:
            state_ref[g] = state[g] * ecsx[g][CHUNK - 1:CHUNK, :] + new_part[g]

    state_ref[...] = jnp.zeros_like(state_ref)
    n_rows = n_chunks * GROUP_HEADS
    for g in gs:
        dt_all = _softplus(dtt_ref[g].reshape(n_rows, CHUNK) + jnp.tile(dt_bias[g], (n_chunks, 1)))
        dt_all_ref[g] = dt_all
        cs_all_ref[g] = cumsum_tokens(dt_all * jnp.tile(a_neg[g], (n_chunks, 1)))

    def heads_of(ci):
        return (slice(ci * GROUP_HEADS, (ci + 1) * GROUP_HEADS) if isinstance(ci, int)
                else pl.ds(pl.multiple_of(ci * GROUP_HEADS, GROUP_HEADS), GROUP_HEADS))

    def dt_cs(ci):
        return lambda g: (dt_all_ref[g, heads_of(ci), :], cs_all_ref[g, heads_of(ci), :])

    real, meta = (xs_ref, b_ref, c_ref), (xs_m_ref, b_m_ref, c_m_ref)
    all_rows = slice(0, CHUNK)
    chunk(meta, all_rows, None, None, lambda g: dtt_m_ref[g], None, True)
    chunk(real, all_rows, meta, slice(CHUNK - STAGE_TAIL, CHUNK), dt_cs(0), all_rows, False)

    def body(ci, carry):
        rows = pl.ds(pl.multiple_of(ci * CHUNK, CHUNK), CHUNK)
        trows = pl.ds(pl.multiple_of(ci * CHUNK - STAGE_TAIL, STAGE_TAIL), STAGE_TAIL)
        chunk(real, rows, real, trows, dt_cs(ci), rows, False)
        return carry

    lax.fori_loop(1, n_chunks, body, 0)


def _ssd(nat, nat_meta, dtt, dtt_meta, conv_w, conv_b, dt_bias, a_log, d_skip, ssm_norm_g,
         *, nb, seq):
    gch, st = GROUP_CH, SSM_STATE
    expand = np.zeros((LANES, gch), np.float32)
    for e in range(GROUP_HEADS):
        expand[e, e * SSM_HEAD_DIM:(e + 1) * SSM_HEAD_DIM] = 1.0
    tri = np.triu(np.ones((CHUNK, CHUNK), np.float32))
    def group_cols(p):
        px = p[:, :SSM_INNER].reshape(-1, SSM_GROUPS, gch)
        pb = p[:, SSM_INNER:SSM_INNER + SSM_GROUPS * st].reshape(-1, SSM_GROUPS, st)
        pc = p[:, SSM_INNER + SSM_GROUPS * st:].reshape(-1, SSM_GROUPS, st)
        return jnp.transpose(jnp.concatenate([px, pb, pc], axis=2), (1, 0, 2))

    cw = group_cols(conv_w)
    cb = group_cols(conv_b.reshape(1, -1))
    stage_w = gch + 2 * st
    shift = np.zeros((CONV_WIDTH * CHUNK, STAGE_TAIL + CHUNK), np.float32)
    for k in range(CONV_WIDTH):
        shift[k * CHUNK + np.arange(CHUNK), STAGE_TAIL + np.arange(CHUNK) - k] = 1.0
    dsk = jnp.repeat(d_skip, SSM_HEAD_DIM).reshape(1, SSM_INNER)
    ng = ssm_norm_g.reshape(1, SSM_INNER)
    dtb = dt_bias.reshape(SSM_GROUPS, GROUP_HEADS, 1)
    alog = a_log.reshape(SSM_GROUPS, GROUP_HEADS, 1)
    gps = SSD_GROUPS_PER_STEP
    wide, narrow = gps * gch, gps * st
    batch_row, first_row = (lambda b, g: b), (lambda b, g: 0)
    return pl.pallas_call(
        functools.partial(_ssd_kernel, n_chunks=seq // CHUNK),
        out_shape=jax.ShapeDtypeStruct((nb * seq, SSM_INNER), BF16),
        grid=(nb, SSM_GROUPS // gps),
        in_specs=[
            _nat_spec(seq, wide, lambda b, g: NAT_XS + g * wide, batch_row),
            _nat_spec(seq, wide, lambda b, g: NAT_Z + g * wide, batch_row),
            _nat_spec(seq, narrow, lambda b, g: NAT_B + g * narrow, batch_row),
            _nat_spec(seq, narrow, lambda b, g: NAT_C + g * narrow, batch_row),
            pl.BlockSpec((gps, None, seq // CHUNK, GROUP_HEADS, CHUNK),
                         lambda b, g: (g, b, 0, 0, 0)),
            _nat_spec(CHUNK, wide, lambda b, g: NAT_XS + g * wide, first_row),
            _nat_spec(CHUNK, narrow, lambda b, g: NAT_B + g * narrow, first_row),
            _nat_spec(CHUNK, narrow, lambda b, g: NAT_C + g * narrow, first_row),
            pl.BlockSpec((gps, GROUP_HEADS, CHUNK), lambda b, g: (g, 0, 0)),
            pl.BlockSpec((gps, CONV_WIDTH, stage_w), lambda b, g: (g, 0, 0)),
            pl.BlockSpec((gps, 1, stage_w), lambda b, g: (g, 0, 0)),
            pl.BlockSpec((gps, GROUP_HEADS, 1), lambda b, g: (g, 0, 0)),
            pl.BlockSpec((gps, GROUP_HEADS, 1), lambda b, g: (g, 0, 0)),
            pl.BlockSpec((1, gps * gch), lambda b, g: (0, g)),
            pl.BlockSpec((1, gps * gch), lambda b, g: (0, g)),
            pl.BlockSpec((LANES, gch), lambda b, g: (0, 0)),
            pl.BlockSpec((CHUNK, CHUNK), lambda b, g: (0, 0)),
            pl.BlockSpec((CONV_WIDTH * CHUNK, STAGE_TAIL + CHUNK), lambda b, g: (0, 0)),
        ],
        out_specs=pl.BlockSpec((seq, gps * gch), lambda b, g: (b, g)),
        scratch_shapes=[pltpu.VMEM((gps, SSM_STATE, gch), F32),
                        pltpu.VMEM((gps, seq // CHUNK * GROUP_HEADS, CHUNK), F32),
                        pltpu.VMEM((gps, seq // CHUNK * GROUP_HEADS, CHUNK), F32)],
        compiler_params=_cparams(("parallel", "arbitrary")),
        name="ssd",
    )(nat, nat, nat, nat, dtt, nat_meta, nat_meta, nat_meta, dtt_meta,
      cw, cb, dtb, alog, dsk, ng,
      jnp.asarray(expand, BF16), jnp.asarray(tri), jnp.asarray(shift, BF16))


def _merge_kernel(attn_ref, ssm_ref, g0_ref, g1_ref, x_ref, woa_ref, wos_ref, wout_ref,
                  n2g_ref, wrh_ref, wrl_ref, br_ref, ltri_ref,
                  h1_ref, hn2_ref, ti_ref, tw_ref, hist_ref):
    a = jnp.dot(attn_ref[...], woa_ref[...], preferred_element_type=F32)
    s = jnp.dot(ssm_ref[...], wos_ref[...], preferred_element_type=F32)
    g0 = _sigmoid(g0_ref[...].astype(F32))
    g1 = _sigmoid(g1_ref[...].astype(F32))
    u = g0 * a + g1 * s
    h1 = x_ref[...] + jnp.dot(u.astype(BF16), wout_ref[...], preferred_element_type=F32)
    h1_ref[...] = h1
    hn2 = _rms_rows(h1, n2g_ref[...])
    hn2_hi = hn2.astype(BF16)
    hn2_ref[...] = hn2_hi
    hn2_lo = (hn2 - hn2_hi.astype(F32)).astype(BF16)
    logits = (jnp.dot(hn2_hi, wrh_ref[...], preferred_element_type=F32)
              + jnp.dot(hn2_lo, wrh_ref[...], preferred_element_type=F32)
              + jnp.dot(hn2_hi, wrl_ref[...], preferred_element_type=F32)
              + br_ref[...])
    lane = lax.broadcasted_iota(I32, logits.shape, 1)
    work = logits
    ids, vals = [], []
    for _ in range(TOP_K):
        mx = jnp.max(work, axis=1, keepdims=True)
        idx = jnp.min(jnp.where(work == mx, lane, LANES), axis=1, keepdims=True)
        ids.append(idx)
        vals.append(mx)
        work = jnp.where(lane == idx, -jnp.inf, work)
    es = [jnp.exp(v - vals[0]) for v in vals]
    den = es[0] + es[1] + es[2] + es[3]
    routed = jnp.zeros(logits.shape, F32)
    for k in range(TOP_K):
        routed = jnp.where(lane == ids[k], 1.0, routed)
    before = jnp.dot(ltri_ref[...], routed.astype(BF16), preferred_element_type=F32)
    hist_ref[...] = jnp.broadcast_to(jnp.sum(routed, axis=0, keepdims=True),
                                     hist_ref.shape).astype(I32)
    ti = jnp.zeros(logits.shape, I32)
    tw = jnp.zeros(logits.shape, F32)
    for k in range(TOP_K):
        rank = jnp.sum(jnp.where(lane == ids[k], before, 0.0), axis=1, keepdims=True)
        ti = jnp.where(lane == k, ids[k], ti)
        ti = jnp.where(lane == TOP_K + k, rank.astype(I32), ti)
        tw = jnp.where(lane == k, es[k] / den, tw)
    ti_ref[...] = ti
    tw_ref[...] = tw


def _merge(attn, ssm, nat, x, woa, wos, wout, n2g, wr_hi, wr_lo, br, *, tm):
    n = x.shape[0]
    const = lambda shape: pl.BlockSpec(shape, lambda i: (0, 0))
    ltri = jnp.asarray(np.tril(np.ones((tm, tm), np.float32), -1), BF16)
    return pl.pallas_call(
        _merge_kernel,
        out_shape=(jax.ShapeDtypeStruct((n, D_MODEL), F32),
                   jax.ShapeDtypeStruct((n, D_MODEL), BF16),
                   jax.ShapeDtypeStruct((n, LANES), I32),
                   jax.ShapeDtypeStruct((n, LANES), F32),
                   jax.ShapeDtypeStruct((n // tm * 8, LANES), I32)),
        grid=(n // tm,),
        in_specs=[
            pl.BlockSpec((tm, ATTN_WIDTH), lambda i: (i, 0)),
            pl.BlockSpec((tm, SSM_INNER), lambda i: (i, 0)),
            _nat_spec(tm, D_MODEL, lambda i: NAT_G, lambda i: i),
            _nat_spec(tm, D_MODEL, lambda i: NAT_G + D_MODEL, lambda i: i),
            pl.BlockSpec((tm, D_MODEL), lambda i: (i, 0)),
            const((ATTN_WIDTH, D_MODEL)), const((SSM_INNER, D_MODEL)), const((D_MODEL, D_MODEL)),
            const((1, D_MODEL)), const((D_MODEL, LANES)), const((D_MODEL, LANES)),
            const((1, LANES)), const((tm, tm)),
        ],
        out_specs=(pl.BlockSpec((tm, D_MODEL), lambda i: (i, 0)),
                   pl.BlockSpec((tm, D_MODEL), lambda i: (i, 0)),
                   pl.BlockSpec((tm, LANES), lambda i: (i, 0)),
                   pl.BlockSpec((tm, LANES), lambda i: (i, 0)),
                   pl.BlockSpec((8, LANES), lambda i: (i, 0))),
        compiler_params=_cparams(("parallel",)),
        name="merge",
    )(attn, ssm, nat, nat, x, woa, wos, wout, n2g, wr_hi, wr_lo, br, ltri)


def _moe_kernel(te_ref, nv_ref, x_ref, wgu_ref, bgu_ref, wd_ref, bd_ref, o_ref,
                wgu_bf_ref, wd_bf_ref):
    t = pl.program_id(0)
    live = t < nv_ref[0]

    @pl.when(live & ((t == 0) | (te_ref[t] != te_ref[jnp.maximum(t - 1, 0)])))
    def _():
        wgu_bf_ref[...] = wgu_ref[...].astype(BF16)
        wd_bf_ref[...] = wd_ref[...].astype(BF16)

    @pl.when(live)
    def _():
        gu = jnp.dot(x_ref[...], wgu_bf_ref[...], preferred_element_type=F32) + bgu_ref[...]
        gate = jnp.minimum(gu[:, :EXPERT_FF], SWIGLU_LIMIT)
        up = jnp.clip(gu[:, EXPERT_FF:], -SWIGLU_LIMIT, SWIGLU_LIMIT)
        act = (up + 1.0) * (gate * _sigmoid(SWIGLU_ALPHA * gate))
        y = jnp.dot(act.astype(BF16), wd_bf_ref[...], preferred_element_type=F32) + bd_ref[...]
        o_ref[...] = y.astype(o_ref.dtype)

    @pl.when(t >= nv_ref[0])
    def _():
        o_ref[...] = jnp.zeros_like(o_ref)


def _moe(tile_expert, n_valid, x_sorted, wgu, bgu, wd, bd):
    rows = x_sorted.shape[0]
    n_tiles = rows // MOE_TM
    grid_spec = pltpu.PrefetchScalarGridSpec(
        num_scalar_prefetch=2,
        grid=(n_tiles,),
        in_specs=[
            pl.BlockSpec((MOE_TM, D_MODEL), lambda t, te, nv: (t, 0)),
            pl.BlockSpec((None, D_MODEL, 2 * EXPERT_FF), lambda t, te, nv: (te[t], 0, 0)),
            pl.BlockSpec((None, 1, 2 * EXPERT_FF), lambda t, te, nv: (te[t], 0, 0)),
            pl.BlockSpec((None, EXPERT_FF, D_MODEL), lambda t, te, nv: (te[t], 0, 0)),
            pl.BlockSpec((None, 1, D_MODEL), lambda t, te, nv: (te[t], 0, 0)),
        ],
        out_specs=pl.BlockSpec((MOE_TM, D_MODEL), lambda t, te, nv: (t, 0)),
        scratch_shapes=[pltpu.VMEM((D_MODEL, 2 * EXPERT_FF), BF16),
                        pltpu.VMEM((EXPERT_FF, D_MODEL), BF16)],
    )
    return pl.pallas_call(
        _moe_kernel,
        out_shape=jax.ShapeDtypeStruct((rows, D_MODEL), BF16),
        grid_spec=grid_spec,
        compiler_params=_cparams(("arbitrary",)),
        name="moe",
    )(tile_expert, n_valid, x_sorted, wgu, bgu, wd, bd)


def _final_kernel(h1_ref, y0_ref, y1_ref, y2_ref, y3_ref, tw_ref, fg_ref, o_ref):
    tw = tw_ref[...]
    h = h1_ref[...]
    for k, y_ref in enumerate((y0_ref, y1_ref, y2_ref, y3_ref)):
        h = h + tw[:, k:k + 1] * y_ref[...].astype(F32)
    o_ref[...] = _rms_rows(h, fg_ref[...])


def _final(h1, ys, tw, fg, *, tm):
    n = h1.shape[0]
    row = lambda w: pl.BlockSpec((tm, w), lambda i: (i, 0))
    return pl.pallas_call(
        _final_kernel,
        out_shape=jax.ShapeDtypeStruct((n, D_MODEL), F32),
        grid=(n // tm,),
        in_specs=[row(D_MODEL)] * (1 + TOP_K)
                 + [row(LANES), pl.BlockSpec((1, D_MODEL), lambda i: (0, 0))],
        out_specs=row(D_MODEL),
        compiler_params=_cparams(("parallel",)),
        name="final",
    )(h1, *ys, tw, fg)


def _split_w_in(w_in):
    offs = np.cumsum((0,) + IN_SPLITS)
    names = ("q", "k", "v", "qi", "ki", "wi", "z", "xs", "b", "c", "dt", "gate")
    return {nm: w_in[:, offs[i]:offs[i + 1]] for i, nm in enumerate(names)}


def _expert_dispatch(top_i, rank, hist, n_rows_pad):
    n = top_i.shape[0]
    pairs = n * TOP_K
    counts = jnp.sum(hist, axis=0)
    grp_start = jnp.cumsum(counts) - counts
    tiles_per_e = (counts + MOE_TM - 1) // MOE_TM
    tile_end = jnp.cumsum(tiles_per_e)
    row_start = (tile_end - tiles_per_e) * MOE_TM
    offset = row_start - grp_start
    base = row_start[None, :] + jnp.cumsum(hist, axis=0) - hist
    base_tok = jnp.repeat(base, n // hist.shape[0], axis=0)
    hit = top_i[:, :, None] == jnp.arange(N_EXPERTS, dtype=I32)[None, None, :]
    dest = jnp.sum(jnp.where(hit, base_tok[:, None, :], 0), axis=2) + rank
    token_of_pair = jnp.broadcast_to(jnp.arange(n, dtype=I32)[:, None], (n, TOP_K))
    _, order_tok = lax.sort_key_val(dest.reshape(-1), token_of_pair.reshape(-1))
    n_tiles = n_rows_pad // MOE_TM
    n_valid = tile_end[-1].astype(I32)
    tile_ids = jnp.minimum(jnp.arange(n_tiles, dtype=I32), n_valid - 1)
    tile_expert = jnp.searchsorted(tile_end, tile_ids, side="right",
                                   method="compare_all").astype(I32)
    tile_expert = jnp.minimum(tile_expert, N_EXPERTS - 1)
    rows = jnp.arange(n_rows_pad, dtype=I32)
    e_row = jnp.repeat(tile_expert, MOE_TM)
    s_row = rows - offset[e_row]
    live = (s_row >= grp_start[e_row]) & (s_row < (grp_start + counts)[e_row])
    src_token = jnp.where(live, order_tok[jnp.clip(s_row, 0, pairs - 1)], rows % n)
    return src_token, dest, tile_expert, n_valid.reshape(1)


def kernel(x, meta_tokens, norm1_g, w_in, conv_w, conv_b, dt_bias, a_log, d_skip, ssm_norm_g,
           w_o_attn, w_o_ssm, w_out, norm2_g, w_router, b_router, w_gate_up, b_gate_up,
           w_down, b_down, final_g):
    nb, seq, d = x.shape
    n = nb * seq
    assert d == D_MODEL and seq % TQ == 0 and w_in.shape[0] == 1
    xf = x.reshape(n, d)
    meta_pad = jnp.concatenate([jnp.zeros((META_PAD, d), x.dtype), meta_tokens.astype(x.dtype)], 0)
    g1 = norm1_g[0].reshape(1, d)

    w = _split_w_in(w_in[0])
    w_nat = jnp.concatenate([w["k"], w["z"], w["xs"], w["b"], w["c"], w["gate"]], 1).astype(BF16)
    w_nat = jnp.transpose(w_nat.reshape(d, NAT_W // NAT_TILE, NAT_TILE), (1, 0, 2))
    w_qvt = jnp.concatenate([w["q"], w["v"]], 1).T.astype(BF16)
    zpad = lambda c: jnp.zeros((d, c), F32)
    w_sn = jnp.concatenate([w["ki"], zpad(SN_DT - IDX_DIM), w["dt"],
                            zpad(SN_W - SN_DT - SSM_HEADS)], 1).astype(BF16)
    w_st = jnp.concatenate([w["qi"], w["wi"], zpad(ST_W - ST_WI - IDX_HEADS)], 1).T.astype(BF16)

    tabs = _rope_tables(N_META + jnp.arange(seq, dtype=I32))
    tabs_t = tuple(t.T for t in tabs)
    pos_meta = jnp.maximum(jnp.arange(CHUNK, dtype=I32) - META_PAD, 0)
    tabs_m = _rope_tables(pos_meta)
    tabs_mt = tuple(t.T for t in tabs_m)

    tm_nat = 1024 if seq % 1024 == 0 else TQ
    w_t = jnp.concatenate([w_qvt, w_st], axis=0)
    q_slabs = ATTN_WIDTH // LANES
    tr_rope = tuple(range(q_slabs)) + tuple(
        2 * q_slabs + s for s in range(IDX_HEADS * IDX_DIM // LANES))
    nat, sm_n = _proj_nat(xf, g1, w_nat, w_sn, tabs, tm=tm_nat, rope_tiles=1,
                          seq_tiles=seq // tm_nat, name="proj_nat")
    qvt, sm_t = _proj_tr(xf, g1, w_t, tabs_t, tm=TQ, nb=nb, width_main=2 * ATTN_WIDTH,
                         rope_slabs=tr_rope, name="proj_t")
    nat_m, sm_n_m = _proj_nat(meta_pad, g1, w_nat, w_sn, tabs_m, tm=CHUNK, rope_tiles=1,
                              seq_tiles=1, name="proj_nat_meta")
    qvt_m, _ = _proj_tr(meta_pad, g1, w_t, tabs_mt, tm=CHUNK, nb=1, width_main=2 * ATTN_WIDTH,
                        rope_slabs=tr_rope, name="proj_t_meta")

    vt_meta = qvt_m[0, 0, ATTN_WIDTH:, CHUNK - ATTN_META_ROWS:]
    attn = _attention(qvt, nat, sm_t, sm_n, nat_m, vt_meta, nb=nb, seq=seq)

    dt_raw = sm_n[:, SN_DT:SN_DT + SSM_HEADS].reshape(nb, seq // CHUNK, CHUNK, SSM_GROUPS,
                                                      GROUP_HEADS)
    dtt = jnp.transpose(dt_raw, (3, 0, 1, 4, 2))
    dt_raw_m = sm_n_m[:, SN_DT:SN_DT + SSM_HEADS].reshape(CHUNK, SSM_GROUPS, GROUP_HEADS)
    dtt_m = jnp.transpose(dt_raw_m, (1, 2, 0))
    ssm = _ssd(nat, nat_m, dtt, dtt_m, conv_w[0], conv_b[0], dt_bias[0], a_log[0], d_skip[0],
               ssm_norm_g[0], nb=nb, seq=seq)

    wr = jnp.concatenate([w_router[0], jnp.zeros((d, LANES - N_EXPERTS), F32)], 1)
    br = jnp.concatenate([b_router[0], jnp.full((LANES - N_EXPERTS,), -1e30, F32)]).reshape(1, LANES)
    wr_hi = wr.astype(BF16)
    wr_lo = (wr - wr_hi.astype(F32)).astype(BF16)
    tm_merge = 512 if n % 512 == 0 else TQ
    h1, hn2, ti, tw, hist = _merge(attn, ssm, nat, xf, w_o_attn[0].astype(BF16),
                                   w_o_ssm[0].astype(BF16), w_out[0].astype(BF16),
                                   norm2_g[0].reshape(1, d), wr_hi, wr_lo, br, tm=tm_merge)

    n_tiles = n * TOP_K // MOE_TM + N_EXPERTS
    src_token, pos, tile_expert, n_valid = _expert_dispatch(
        ti[:, :TOP_K], ti[:, TOP_K:2 * TOP_K], hist[::8, :N_EXPERTS], n_tiles * MOE_TM)
    x_sorted = hn2.at[src_token].get(mode="promise_in_bounds")
    y_sorted = _moe(tile_expert, n_valid, x_sorted, w_gate_up[0],
                    b_gate_up[0].reshape(N_EXPERTS, 1, -1), w_down[0],
                    b_down[0].reshape(N_EXPERTS, 1, -1))
    ys = [y_sorted.at[pos[:, k]].get(mode="promise_in_bounds") for k in range(TOP_K)]

    out = _final(h1, ys, tw, final_g.reshape(1, d), tm=512 if n % 512 == 0 else TQ)
    return out.reshape(nb, seq, d)
```

```python
import functools
import math

import numpy as np
import jax
import jax.numpy as jnp
from jax import lax
from jax.experimental import pallas as pl
from jax.experimental.pallas import tpu as pltpu

F32 = jnp.float32
BF16 = jnp.bfloat16
I32 = jnp.int32
I16 = jnp.int16

D_MODEL = 1024
N_META = 16
RMS_EPS = 1e-6
N_HEADS = 16
HEAD_DIM = 64
ATTN_WIDTH = N_HEADS * HEAD_DIM
ROT_DIM = HEAD_DIM // 4
ROPE_THETA = 500000.0
IDX_HEADS = 8
IDX_DIM = 64
TOPK_KEYS_MAX = 256
SSM_INNER = 2 * D_MODEL
SSM_HEAD_DIM = 64
SSM_HEADS = SSM_INNER // SSM_HEAD_DIM
SSM_GROUPS = 4
SSM_STATE = 128
CONV_WIDTH = 4
CHUNK = 128
N_EXPERTS = 32
TOP_K = 4
EXPERT_FF = D_MODEL
SWIGLU_LIMIT = 7.0
SWIGLU_ALPHA = 1.702
IN_SPLITS = (ATTN_WIDTH, ATTN_WIDTH, ATTN_WIDTH, IDX_HEADS * IDX_DIM, IDX_DIM, IDX_HEADS,
             SSM_INNER, SSM_INNER, SSM_GROUPS * SSM_STATE, SSM_GROUPS * SSM_STATE, SSM_HEADS,
             2 * D_MODEL)

LANES = 128
GROUP_CH = SSM_INNER // SSM_GROUPS
GROUP_HEADS = SSM_HEADS // SSM_GROUPS
META_PAD = CHUNK - N_META
TQ = 256
ATTN_META_ROWS = 32
FLASH_LAG = 8
MOE_TM = 512
INT_MIN = -2147483648
VMEM_LIMIT = 56 * 1024 * 1024

NAT_K, NAT_Z, NAT_XS, NAT_B, NAT_C, NAT_G = 0, 1024, 3072, 5120, 5632, 6144
NAT_W = 8192
NAT_TILE = 1024


def _nat_spec(rows, width, col, row_block):
    assert NAT_TILE % width == 0

    def index(*grid):
        c = col(*grid)
        return c // NAT_TILE, row_block(*grid), (c % NAT_TILE) // width

    return pl.BlockSpec((None, rows, width), index)
SN_KI, SN_DT, SN_W = 0, 128, 256
ST_QI, ST_WI, ST_W = 0, 512, 640


def _cparams(sem):
    return pltpu.CompilerParams(dimension_semantics=sem, vmem_limit_bytes=VMEM_LIMIT)


def _rms_rows(x, g):
    ms = jnp.mean(x * x, axis=-1, keepdims=True)
    return x * lax.rsqrt(ms + RMS_EPS) * g


def _proj_nat_kernel(x_ref, g_ref, w_ref, wsm_ref, cos_ref, sa_ref, sb_ref, o_ref, osm_ref,
                     hn_ref, *, rope_tiles):
    j = pl.program_id(1)

    def rope(a):
        return (a * cos_ref[...] + pltpu.roll(a, 8, 1) * sa_ref[...]
                + pltpu.roll(a, LANES - 8, 1) * sb_ref[...])

    @pl.when(j == 0)
    def _():
        hn = _rms_rows(x_ref[...], g_ref[...]).astype(hn_ref.dtype)
        hn_ref[...] = hn
        small = jnp.dot(hn, wsm_ref[...], preferred_element_type=F32)
        osm_ref[:, 0:LANES] = rope(small[:, 0:LANES])
        osm_ref[:, LANES:] = small[:, LANES:]

    @pl.when(j < rope_tiles)
    def _():
        acc = jnp.dot(hn_ref[...], w_ref[...], preferred_element_type=F32)
        for s in range(acc.shape[1] // LANES):
            cols = slice(s * LANES, (s + 1) * LANES)
            o_ref[:, cols] = rope(acc[:, cols]).astype(o_ref.dtype)

    @pl.when(j >= rope_tiles)
    def _():
        o_ref[...] = jnp.dot(hn_ref[...], w_ref[...],
                             preferred_element_type=F32).astype(o_ref.dtype)


def _proj_nat(x, g, w_tiles, w_small, tabs, *, tm, rope_tiles, seq_tiles, name):
    n, d = x.shape
    n_col, _, tn = w_tiles.shape
    width_sm = w_small.shape[1]
    cos, sa, sb = tabs
    tab_spec = pl.BlockSpec((tm, LANES), lambda i, j: (i % seq_tiles, 0))
    return pl.pallas_call(
        functools.partial(_proj_nat_kernel, rope_tiles=rope_tiles),
        out_shape=(jax.ShapeDtypeStruct((n_col, n, tn), BF16),
                   jax.ShapeDtypeStruct((n, width_sm), F32)),
        grid=(n // tm, n_col),
        in_specs=[
            pl.BlockSpec((tm, d), lambda i, j: (i, 0)),
            pl.BlockSpec((1, d), lambda i, j: (0, 0)),
            pl.BlockSpec((None, d, tn), lambda i, j: (j, 0, 0)),
            pl.BlockSpec((d, width_sm), lambda i, j: (0, 0)),
            tab_spec, tab_spec, tab_spec,
        ],
        out_specs=(pl.BlockSpec((None, tm, tn), lambda i, j: (j, i, 0)),
                   pl.BlockSpec((tm, width_sm), lambda i, j: (i, 0))),
        scratch_shapes=[pltpu.VMEM((tm, d), BF16)],
        compiler_params=_cparams(("parallel", "arbitrary")),
        name=name,
    )(x, g, w_tiles, w_small, cos, sa, sb)


def _proj_tr_kernel(x_ref, g_ref, wt_ref, cos_ref, sa_ref, sb_ref, o_ref, osm_ref, *,
                    rope_slabs):
    hn = _rms_rows(x_ref[...], g_ref[...]).astype(BF16)
    acc = lax.dot_general(wt_ref[...], hn, (((1,), (1,)), ((), ())),
                          preferred_element_type=F32)
    cos, sa, sb = cos_ref[...], sa_ref[...], sb_ref[...]
    n_main = o_ref.shape[0] // LANES
    for s in range(acc.shape[0] // LANES):
        a = acc[s * LANES:(s + 1) * LANES, :]
        if s in rope_slabs:
            a = a * cos + pltpu.roll(a, 8, 0) * sa + pltpu.roll(a, LANES - 8, 0) * sb
        if s < n_main:
            o_ref[s * LANES:(s + 1) * LANES, :] = a.astype(o_ref.dtype)
        else:
            osm_ref[(s - n_main) * LANES:(s - n_main + 1) * LANES, :] = a


def _proj_tr(x, g, wt, tabs_t, *, tm, nb, width_main, rope_slabs, name):
    n, d = x.shape
    width = wt.shape[0]
    width_sm = width - width_main
    seq_tiles = n // nb // tm
    cos, sa, sb = tabs_t
    tab_spec = pl.BlockSpec((LANES, tm), lambda i: (0, i % seq_tiles))
    out_map = lambda i: (i // seq_tiles, i % seq_tiles, 0, 0)
    return pl.pallas_call(
        functools.partial(_proj_tr_kernel, rope_slabs=rope_slabs),
        out_shape=(jax.ShapeDtypeStruct((nb, seq_tiles, width_main, tm), BF16),
                   jax.ShapeDtypeStruct((nb, seq_tiles, width_sm, tm), F32)),
        grid=(n // tm,),
        in_specs=[
            pl.BlockSpec((tm, d), lambda i: (i, 0)),
            pl.BlockSpec((1, d), lambda i: (0, 0)),
            pl.BlockSpec((width, d), lambda i: (0, 0)),
            tab_spec, tab_spec, tab_spec,
        ],
        out_specs=(pl.BlockSpec((None, None, width_main, tm), out_map),
                   pl.BlockSpec((None, None, width_sm, tm), out_map)),
        compiler_params=_cparams(("parallel",)),
        name=name,
    )(x, g, wt, cos, sa, sb)


def _rope_tables(pos):
    half = ROT_DIM // 2
    inv_freq = jnp.exp(-math.log(ROPE_THETA) * jnp.arange(half, dtype=F32) / half)
    ang = pos.astype(F32)[:, None] * inv_freq[None, :]
    cos8, sin8 = jnp.cos(ang), jnp.sin(ang)
    t = pos.shape[0]
    cc = np.arange(LANES) % HEAD_DIM
    in_rot = jnp.asarray(cc < ROT_DIM)
    lo = jnp.asarray(cc < half)
    hi = jnp.asarray((cc >= half) & (cc < ROT_DIM))
    idx = jnp.asarray(cc % half)
    cos_l = jnp.take(cos8, idx, axis=1)
    sin_l = jnp.take(sin8, idx, axis=1)
    cos = jnp.where(in_rot[None, :], cos_l, 1.0)
    sa = jnp.where(hi[None, :], sin_l, 0.0)
    sb = jnp.where(lo[None, :], -sin_l, 0.0)
    del t
    return cos, sa, sb


def _attn_kernel(qt_ref, vt_ref, k_ref, qit_ref, wit_ref, ki_ref, kmeta_ref, vtmeta_ref,
                 o_ref, keys_ref, hi_ref, lo_ref, bias_ref, qm_ref, m_ref, l_ref, acc_ref, s_ref,
                 *, ksel, nbits):
    j = pl.program_id(1)
    tq = TQ
    nch = j + 1
    int_min = jnp.int32(INT_MIN)
    neg_inf = jnp.float32(-jnp.inf)

    row128 = lax.broadcasted_iota(I32, (LANES, tq), 0)
    for h in range(N_HEADS):
        slab = qt_ref[(h // 2) * LANES:(h // 2 + 1) * LANES, :]
        keep = (row128 < HEAD_DIM) if h % 2 == 0 else (row128 >= HEAD_DIM)
        scaled = slab.astype(F32) * (HEAD_DIM ** -0.5 * math.log2(math.e))
        qm_ref[h] = jnp.where(keep, scaled, 0.0).astype(qm_ref.dtype)

    w_all = wit_ref[0:IDX_HEADS, :] * (IDX_HEADS ** -0.5 * IDX_DIM ** -0.5)
    qi_bf = [qit_ref[h * IDX_DIM:(h + 1) * IDX_DIM, :].astype(BF16) for h in range(IDX_HEADS)]
    qidx = j * tq + lax.broadcasted_iota(I32, (tq, tq), 1)
    krow = lax.broadcasted_iota(I32, (tq, tq), 0)

    def score_body(c, carry):
        start = pl.multiple_of(c * tq, tq)
        kic = ki_ref[pl.ds(start, tq), :][:, :IDX_DIM].astype(BF16)
        sc = jnp.zeros((tq, tq), F32)
        for h in range(IDX_HEADS):
            lg = jnp.dot(kic, qi_bf[h], preferred_element_type=F32)
            sc = sc + jnp.maximum(lg, 0.0) * w_all[h:h + 1, :]
        bits = pltpu.bitcast(sc, I32)
        skey = jnp.where(bits < 0, bits ^ jnp.int32(0x7FFFFFFF), bits)
        skey = jnp.where(c * tq + krow <= qidx, skey, int_min)
        keys_ref[c] = skey
        hi_ref[c] = (skey >> 16).astype(I16)
        lo_ref[c] = ((skey & 0xFFFF) - 32768).astype(I16)
        return carry

    lax.fori_loop(0, nch, score_body, 0)

    @pl.when(nch % 2 == 1)
    def _():
        hi_ref[nch] = jnp.full((tq, tq), -32768, I16)
        lo_ref[nch] = jnp.full((tq, tq), -32768, I16)

    npair = (nch + 1) // 2

    def count(pred):
        def body(c, cnt):
            hit = pred(keys_ref[c], c).astype(I32)
            return cnt + hit.reshape(tq // 8, 8, tq).sum(axis=0)
        cnt8 = lax.fori_loop(0, nch, body, jnp.zeros((8, tq), I32))
        return jnp.sum(cnt8, axis=0, keepdims=True)

    def count16(ref, cand):
        cand16 = cand.astype(I16)
        one, zero = jnp.int16(1), jnp.int16(0)

        def body(i, cnt):
            for c in (2 * i, 2 * i + 1):
                hit = pltpu.bitcast(jnp.where(ref[c] >= cand16, one, zero), I32)
                cnt = cnt + hit.reshape(tq // 16, 8, tq).sum(axis=0)
            return cnt

        cnt8 = lax.fori_loop(0, npair, body, jnp.zeros((8, tq), I32))
        both = jnp.sum(cnt8, axis=0, keepdims=True)
        return (both & 0xFFFF) + lax.shift_right_logical(both, 16)

    def search16(ref, want):
        def bit_body(it, t_b):
            cand_b = t_b | lax.shift_left(jnp.int32(1), 15 - it)
            cnt = count16(ref, cand_b - 32768)
            return jnp.where(cnt >= want, cand_b, t_b)
        return lax.fori_loop(0, 16, bit_body, jnp.zeros((1, tq), I32))

    hi_b = search16(hi_ref, ksel)
    thr_hi = hi_b - 32768
    n_above = count16(hi_ref, jnp.minimum(thr_hi + 1, 32767))
    n_above = jnp.where(thr_hi >= 32767, 0, n_above)
    thr_hi16 = thr_hi.astype(I16)

    def bucket_body(c, carry):
        lo_ref[c] = jnp.where(hi_ref[c] == thr_hi16, lo_ref[c], jnp.int16(-32768))
        return carry

    lax.fori_loop(0, 2 * npair, bucket_body, 0)
    lo_b = search16(lo_ref, ksel - n_above)
    thr = lax.shift_left(thr_hi, 16) | lo_b
    n_gt = count(lambda kc, c: kc > thr)
    n_eq = count(lambda kc, c: kc == thr)
    need = ksel - n_gt
    fix = (n_eq > need) & (thr != int_min)
    thr_eff = jnp.maximum(thr, int_min + 1)

    def bias_body(c, carry):
        bias_ref[c] = jnp.where(keys_ref[c] >= thr_eff, 0.0, neg_inf)
        return carry

    lax.fori_loop(0, nch, bias_body, 0)

    @pl.when(jnp.max(fix.astype(I32)) > 0)
    def _():
        def idx_body(it, v):
            cand = v | lax.shift_left(jnp.int32(1), nbits - 1 - it)
            cnt = count(lambda kc, c: jnp.where(c * tq + krow < cand, kc, int_min) == thr)
            return jnp.where(cnt <= need - 1, cand, v)

        last = lax.fori_loop(0, nbits, idx_body, jnp.zeros((1, tq), I32))

        def fix_body(c, carry):
            kc = keys_ref[c]
            tie_lim = jnp.where(fix, last, jnp.int32(2 ** 30))
            tie_ok = jnp.where(c * tq + krow <= tie_lim, thr_eff, thr_eff + 1)
            bias_ref[c] = jnp.where(kc >= tie_ok, 0.0, neg_inf)
            return carry

        lax.fori_loop(0, nch, fix_body, 0)

    def chunk_step(kslab_of, vt_of, bias_c, n_keys, first):
        alphas = []
        ones = jnp.ones((16, n_keys), BF16)

        def scores(h):
            s = jnp.dot(kslab_of(h), qm_ref[h], preferred_element_type=F32) + bias_c
            s_ref[h, 0:n_keys, :] = s
            smax = jnp.max(s, axis=0, keepdims=True)
            if first:
                m_ref[h:h + 1, :] = smax
            else:
                m_old = m_ref[h:h + 1, :]
                m_new = jnp.maximum(m_old, smax)
                alphas.append(jnp.exp2(m_old - m_new))
                m_ref[h:h + 1, :] = m_new

        def values(h):
            p = jnp.exp2(s_ref[h, 0:n_keys, :] - m_ref[h:h + 1, :]).astype(BF16)
            res = jnp.dot(jnp.concatenate([vt_of(h), ones], axis=0), p,
                          preferred_element_type=F32)
            pv, psum = res[:HEAD_DIM, :], res[HEAD_DIM:HEAD_DIM + 1, :]
            rows = slice(h * HEAD_DIM, (h + 1) * HEAD_DIM)
            if first:
                l_ref[h:h + 1, :] = psum
                acc_ref[rows, :] = pv
            else:
                l_ref[h:h + 1, :] = l_ref[h:h + 1, :] * alphas[h] + psum
                acc_ref[rows, :] = acc_ref[rows, :] * alphas[h] + pv

        for h in range(N_HEADS + FLASH_LAG):
            if h < N_HEADS:
                scores(h)
            if h >= FLASH_LAG:
                values(h - FLASH_LAG)

    meta_row = lax.broadcasted_iota(I32, (ATTN_META_ROWS, tq), 0)
    bias_meta = jnp.where(meta_row >= ATTN_META_ROWS - N_META, 0.0, neg_inf)
    chunk_step(lambda h: kmeta_ref[:, (h // 2) * LANES:(h // 2 + 1) * LANES],
               lambda h: vtmeta_ref[h * HEAD_DIM:(h + 1) * HEAD_DIM, :], bias_meta,
               ATTN_META_ROWS, True)

    def flash_body(c, carry):
        start = pl.multiple_of(c * tq, tq)
        chunk_step(lambda h: k_ref[pl.ds(start, tq), (h // 2) * LANES:(h // 2 + 1) * LANES],
                   lambda h: vt_ref[c, h * HEAD_DIM:(h + 1) * HEAD_DIM, :], bias_ref[c], tq, False)
        return carry

    lax.fori_loop(0, nch, flash_body, 0)

    for h in range(N_HEADS):
        rows = slice(h * HEAD_DIM, (h + 1) * HEAD_DIM)
        acc_ref[rows, :] = acc_ref[rows, :] * (1.0 / l_ref[h:h + 1, :])
    o_ref[...] = acc_ref[...].T.astype(o_ref.dtype)


def _attention(qvt, nat, sm_t, sm_n, nat_meta, qvt_meta, *, nb, seq):
    nq = seq // TQ
    ksel = min(TOPK_KEYS_MAX, seq // 4)
    nbits = max(1, int(math.ceil(math.log2(seq))))
    kernel = functools.partial(_attn_kernel, ksel=ksel, nbits=nbits)
    return pl.pallas_call(
        kernel,
        out_shape=jax.ShapeDtypeStruct((nb * seq, ATTN_WIDTH), BF16),
        grid=(nb, nq),
        in_specs=[
            pl.BlockSpec((None, None, ATTN_WIDTH, TQ), lambda b, j: (b, j, 0, 0)),
            pl.BlockSpec((None, nq, ATTN_WIDTH, TQ), lambda b, j: (b, 0, 1, 0)),
            _nat_spec(seq, ATTN_WIDTH, lambda b, j: NAT_K, lambda b, j: b),
            pl.BlockSpec((None, None, IDX_HEADS * IDX_DIM, TQ), lambda b, j: (b, j, 0, 0)),
            pl.BlockSpec((None, None, LANES, TQ), lambda b, j: (b, j, ST_WI // LANES, 0)),
            pl.BlockSpec((seq, LANES), lambda b, j: (b, SN_KI // LANES)),
            _nat_spec(ATTN_META_ROWS, ATTN_WIDTH, lambda b, j: NAT_K,
                      lambda b, j: CHUNK // ATTN_META_ROWS - 1),
            pl.BlockSpec((ATTN_WIDTH, ATTN_META_ROWS), lambda b, j: (0, 0)),
        ],
        out_specs=pl.BlockSpec((TQ, ATTN_WIDTH), lambda b, j: (b * nq + j, 0)),
        scratch_shapes=[
            pltpu.VMEM((nq, TQ, TQ), I32),
            pltpu.VMEM((nq + nq % 2, TQ, TQ), I16),
            pltpu.VMEM((nq + nq % 2, TQ, TQ), I16),
            pltpu.VMEM((nq, TQ, TQ), F32),
            pltpu.VMEM((N_HEADS, LANES, TQ), BF16),
            pltpu.VMEM((N_HEADS, TQ), F32),
            pltpu.VMEM((N_HEADS, TQ), F32),
            pltpu.VMEM((ATTN_WIDTH, TQ), F32),
            pltpu.VMEM((N_HEADS, TQ, TQ), F32),
        ],
        compiler_params=_cparams(("parallel", "arbitrary")),
        name="attn",
    )(qvt, qvt, nat, sm_t, sm_t, sm_n, nat_meta, qvt_meta)


def _softplus(x):
    return jnp.maximum(x, 0.0) + jnp.log1p(jnp.exp(-jnp.abs(x)))


def _sigmoid(x):
    return 0.5 + 0.5 * jnp.tanh(0.5 * x)


def _silu(x):
    h = 0.5 * x
    return h + h * jnp.tanh(h)


def _pad_transpose(rows):
    padded = jnp.concatenate([rows, jnp.zeros((LANES - rows.shape[0], LANES), rows.dtype)], axis=0)
    return padded.T


STAGE_TAIL = 16
SSD_GROUPS_PER_STEP = 2


def _ssd_kernel(xs_ref, z_ref, b_ref, c_ref, dtt_ref,
                xs_m_ref, b_m_ref, c_m_ref, dtt_m_ref,
                cw_ref, cb_ref, dtb_ref, alog_ref, dsk_ref, ng_ref, exp_ref, tri_ref, shift_ref,
                o_ref, state_ref, dt_all_ref, cs_all_ref, *, n_chunks):
    gs = range(SSD_GROUPS_PER_STEP)
    a_neg = [-jnp.exp(alog_ref[g]) for g in gs]
    dt_bias = [dtb_ref[g] for g in gs]
    expand = exp_ref[...]
    tri_u = tri_ref[...]
    li = lax.broadcasted_iota(I32, (CHUNK, CHUNK), 0)
    si = lax.broadcasted_iota(I32, (CHUNK, CHUNK), 1)
    tril = si <= li
    lane = lax.broadcasted_iota(I32, (CHUNK, LANES), 1)
    low_half = lane < SSM_HEAD_DIM
    neg_inf = jnp.float32(-jnp.inf)

    xb, xc = GROUP_CH, GROUP_CH + SSM_STATE

    def cumsum_tokens(rows):
        return jnp.dot(rows, tri_u, precision=lax.Precision.HIGHEST, preferred_element_type=F32)

    def token_major(dt_r, cs_r):
        dt_c = _pad_transpose(dt_r)
        cs_c = _pad_transpose(cs_r)
        dec_c = jnp.exp(cs_c[CHUNK - 1:CHUNK, :] - cs_c)
        return cs_c, dt_c.astype(BF16), (dt_c * dec_c).astype(BF16), jnp.exp(cs_c).astype(BF16)

    widths = (GROUP_CH, SSM_STATE, SSM_STATE)
    gcols = lambda g: slice(g * GROUP_CH, (g + 1) * GROUP_CH)

    def pieces(refs, rows, g):
        return [r[rows, g * w:(g + 1) * w] for r, w in zip(refs, widths)]

    def chunk(src, rows, tail_src, trows, dt_of, out_rows, is_meta):
        stacks = []
        for g in gs:
            tail = ([jnp.zeros((STAGE_TAIL, w), BF16) for w in widths] if tail_src is None
                    else pieces(tail_src, trows, g))
            stacks.append(jnp.concatenate(
                [jnp.concatenate(tail, axis=1), jnp.concatenate(pieces(src, rows, g), axis=1)],
                axis=0))
        shifted = [jnp.dot(shift_ref[...], stacks[g], preferred_element_type=F32) for g in gs]
        x, bm_bf, cm_bf = [], [], []
        for g in gs:
            conv = cb_ref[g]
            for k in range(CONV_WIDTH):
                conv = conv + (shifted[g][k * CHUNK:(k + 1) * CHUNK, :]
                               * cw_ref[g, CONV_WIDTH - 1 - k:CONV_WIDTH - k, :])
            act = _silu(conv)
            x.append(act[:, 0:xb])
            bm_bf.append(act[:, xb:xc].astype(BF16))
            cm_bf.append(act[:, xc:].astype(BF16))
        if is_meta:
            dt_r = []
            for g in gs:
                d = _softplus(dt_of(g) + dt_bias[g])
                dt_r.append(jnp.where(lax.broadcasted_iota(I32, d.shape, 1) >= META_PAD, d, 0.0))
            cs_r = [cumsum_tokens(dt_r[g] * a_neg[g]) for g in gs]
        else:
            dt_r, cs_r = zip(*[dt_of(g) for g in gs])
        cs_c, dt_bf, dd_bf, ec_bf = zip(*[token_major(dt_r[g], cs_r[g]) for g in gs])
        dtx = [jnp.dot(dt_bf[g], expand, preferred_element_type=F32) for g in gs]
        decx = [jnp.dot(dd_bf[g], expand, preferred_element_type=F32) for g in gs]
        ecsx = [jnp.dot(ec_bf[g], expand, preferred_element_type=F32) for g in gs]
        state = [state_ref[g] for g in gs]
        new_part = [lax.dot_general(bm_bf[g], (x[g] * decx[g]).astype(BF16),
                                    (((0,), (0,)), ((), ())), preferred_element_type=F32)
                    for g in gs]
        if not is_meta:
            xdt = [(x[g] * dtx[g]).astype(BF16) for g in gs]
            cb = [lax.dot_general(cm_bf[g], bm_bf[g], (((1,), (1,)), ((), ())),
                                  preferred_element_type=F32) for g in gs]
            y_off = [jnp.dot(cm_bf[g], state[g].astype(BF16), preferred_element_type=F32)
                     * ecsx[g] for g in gs]
            slabs = [[] for _ in gs]
            for pp in range(GROUP_HEADS // 2):
                halves = [[] for _ in gs]
                for e in (2 * pp, 2 * pp + 1):
                    for g in gs:
                        diff = cs_c[g][:, e:e + 1] - cs_r[g][e:e + 1, :]
                        lmat = jnp.exp(jnp.where(tril, diff, neg_inf))
                        halves[g].append(jnp.dot((cb[g] * lmat).astype(BF16),
                                                 xdt[g][:, pp * LANES:(pp + 1) * LANES],
                                                 preferred_element_type=F32))
                for g in gs:
                    slabs[g].append(jnp.where(low_half, halves[g][0], halves[g][1]))
            for g in gs:
                y = jnp.concatenate(slabs[g], axis=1) + y_off[g] + x[g] * dsk_ref[:, gcols(g)]
                y = y * _silu(z_ref[rows, gcols(g)].astype(F32))
                ms = jnp.mean(y * y, axis=-1, keepdims=True)
                o_ref[out_rows, gcols(g)] = (y * lax.rsqrt(ms + RMS_EPS)
                                             * ng_ref[:, gcols(g)]).astype(o_ref.dtype)
        for g in gs:
            state_ref[g] = state[g] * ecsx[g][CHUNK - 1:CHUNK, :] + new_part[g]

    state_ref[...] = jnp.zeros_like(state_ref)
    n_rows = n_chunks * GROUP_HEADS
    for g in gs:
        dt_all = _softplus(dtt_ref[g].reshape(n_rows, CHUNK) + jnp.tile(dt_bias[g], (n_chunks, 1)))
        dt_all_ref[g] = dt_all
        cs_all_ref[g] = cumsum_tokens(dt_all * jnp.tile(a_neg[g], (n_chunks, 1)))

    def heads_of(ci):
        return (slice(ci * GROUP_HEADS, (ci + 1) * GROUP_HEADS) if isinstance(ci, int)
                else pl.ds(pl.multiple_of(ci * GROUP_HEADS, GROUP_HEADS), GROUP_HEADS))

    def dt_cs(ci):
        return lambda g: (dt_all_ref[g, heads_of(ci), :], cs_all_ref[g, heads_of(ci), :])

    real, meta = (xs_ref, b_ref, c_ref), (xs_m_ref, b_m_ref, c_m_ref)
    all_rows = slice(0, CHUNK)
    chunk(meta, all_rows, None, None, lambda g: dtt_m_ref[g], None, True)
    chunk(real, all_rows, meta, slice(CHUNK - STAGE_TAIL, CHUNK), dt_cs(0), all_rows, False)

    def body(ci, carry):
        rows = pl.ds(pl.multiple_of(ci * CHUNK, CHUNK), CHUNK)
        trows = pl.ds(pl.multiple_of(ci * CHUNK - STAGE_TAIL, STAGE_TAIL), STAGE_TAIL)
        chunk(real, rows, real, trows, dt_cs(ci), rows, False)
        return carry

    lax.fori_loop(1, n_chunks, body, 0)


def _ssd(nat, nat_meta, dtt, dtt_meta, conv_w, conv_b, dt_bias, a_log, d_skip, ssm_norm_g,
         *, nb, seq):
    gch, st = GROUP_CH, SSM_STATE
    expand = np.zeros((LANES, gch), np.float32)
    for e in range(GROUP_HEADS):
        expand[e, e * SSM_HEAD_DIM:(e + 1) * SSM_HEAD_DIM] = 1.0
    tri = np.triu(np.ones((CHUNK, CHUNK), np.float32))
    def group_cols(p):
        px = p[:, :SSM_INNER].reshape(-1, SSM_GROUPS, gch)
        pb = p[:, SSM_INNER:SSM_INNER + SSM_GROUPS * st].reshape(-1, SSM_GROUPS, st)
        pc = p[:, SSM_INNER + SSM_GROUPS * st:].reshape(-1, SSM_GROUPS, st)
        return jnp.transpose(jnp.concatenate([px, pb, pc], axis=2), (1, 0, 2))

    cw = group_cols(conv_w)
    cb = group_cols(conv_b.reshape(1, -1))
    stage_w = gch + 2 * st
    shift = np.zeros((CONV_WIDTH * CHUNK, STAGE_TAIL + CHUNK), np.float32)
    for k in range(CONV_WIDTH):
        shift[k * CHUNK + np.arange(CHUNK), STAGE_TAIL + np.arange(CHUNK) - k] = 1.0
    dsk = jnp.repeat(d_skip, SSM_HEAD_DIM).reshape(1, SSM_INNER)
    ng = ssm_norm_g.reshape(1, SSM_INNER)
    dtb = dt_bias.reshape(SSM_GROUPS, GROUP_HEADS, 1)
    alog = a_log.reshape(SSM_GROUPS, GROUP_HEADS, 1)
    gps = SSD_GROUPS_PER_STEP
    wide, narrow = gps * gch, gps * st
    batch_row, first_row = (lambda b, g: b), (lambda b, g: 0)
    return pl.pallas_call(
        functools.partial(_ssd_kernel, n_chunks=seq // CHUNK),
        out_shape=jax.ShapeDtypeStruct((nb * seq, SSM_INNER), BF16),
        grid=(nb, SSM_GROUPS // gps),
        in_specs=[
            _nat_spec(seq, wide, lambda b, g: NAT_XS + g * wide, batch_row),
            _nat_spec(seq, wide, lambda b, g: NAT_Z + g * wide, batch_row),
            _nat_spec(seq, narrow, lambda b, g: NAT_B + g * narrow, batch_row),
            _nat_spec(seq, narrow, lambda b, g: NAT_C + g * narrow, batch_row),
            pl.BlockSpec((gps, None, seq // CHUNK, GROUP_HEADS, CHUNK),
                         lambda b, g: (g, b, 0, 0, 0)),
            _nat_spec(CHUNK, wide, lambda b, g: NAT_XS + g * wide, first_row),
            _nat_spec(CHUNK, narrow, lambda b, g: NAT_B + g * narrow, first_row),
            _nat_spec(CHUNK, narrow, lambda b, g: NAT_C + g * narrow, first_row),
            pl.BlockSpec((gps, GROUP_HEADS, CHUNK), lambda b, g: (g, 0, 0)),
            pl.BlockSpec((gps, CONV_WIDTH, stage_w), lambda b, g: (g, 0, 0)),
            pl.BlockSpec((gps, 1, stage_w), lambda b, g: (g, 0, 0)),
            pl.BlockSpec((gps, GROUP_HEADS, 1), lambda b, g: (g, 0, 0)),
            pl.BlockSpec((gps, GROUP_HEADS, 1), lambda b, g: (g, 0, 0)),
            pl.BlockSpec((1, gps * gch), lambda b, g: (0, g)),
            pl.BlockSpec((1, gps * gch), lambda b, g: (0, g)),
            pl.BlockSpec((LANES, gch), lambda b, g: (0, 0)),
            pl.BlockSpec((CHUNK, CHUNK), lambda b, g: (0, 0)),
            pl.BlockSpec((CONV_WIDTH * CHUNK, STAGE_TAIL + CHUNK), lambda b, g: (0, 0)),
        ],
        out_specs=pl.BlockSpec((seq, gps * gch), lambda b, g: (b, g)),
        scratch_shapes=[pltpu.VMEM((gps, SSM_STATE, gch), F32),
                        pltpu.VMEM((gps, seq // CHUNK * GROUP_HEADS, CHUNK), F32),
                        pltpu.VMEM((gps, seq // CHUNK * GROUP_HEADS, CHUNK), F32)],
        compiler_params=_cparams(("parallel", "arbitrary")),
        name="ssd",
    )(nat, nat, nat, nat, dtt, nat_meta, nat_meta, nat_meta, dtt_meta,
      cw, cb, dtb, alog, dsk, ng,
      jnp.asarray(expand, BF16), jnp.asarray(tri), jnp.asarray(shift, BF16))


def _merge_kernel(attn_ref, ssm_ref, g0_ref, g1_ref, x_ref, woa_ref, wos_ref, wout_ref,
                  n2g_ref, wrh_ref, wrl_ref, br_ref, ltri_ref,
                  h1_ref, hn2_ref, ti_ref, tw_ref, hist_ref):
    a = jnp.dot(attn_ref[...], woa_ref[...], preferred_element_type=F32)
    s = jnp.dot(ssm_ref[...], wos_ref[...], preferred_element_type=F32)
    g0 = _sigmoid(g0_ref[...].astype(F32))
    g1 = _sigmoid(g1_ref[...].astype(F32))
    u = g0 * a + g1 * s
    h1 = x_ref[...] + jnp.dot(u.astype(BF16), wout_ref[...], preferred_element_type=F32)
    h1_ref[...] = h1
    hn2 = _rms_rows(h1, n2g_ref[...])
    hn2_hi = hn2.astype(BF16)
    hn2_ref[...] = hn2_hi
    hn2_lo = (hn2 - hn2_hi.astype(F32)).astype(BF16)
    logits = (jnp.dot(hn2_hi, wrh_ref[...], preferred_element_type=F32)
              + jnp.dot(hn2_lo, wrh_ref[...], preferred_element_type=F32)
              + jnp.dot(hn2_hi, wrl_ref[...], preferred_element_type=F32)
              + br_ref[...])
    lane = lax.broadcasted_iota(I32, logits.shape, 1)
    work = logits
    ids, vals = [], []
    for _ in range(TOP_K):
        mx = jnp.max(work, axis=1, keepdims=True)
        idx = jnp.min(jnp.where(work == mx, lane, LANES), axis=1, keepdims=True)
        ids.append(idx)
        vals.append(mx)
        work = jnp.where(lane == idx, -jnp.inf, work)
    es = [jnp.exp(v - vals[0]) for v in vals]
    den = es[0] + es[1] + es[2] + es[3]
    routed = jnp.zeros(logits.shape, F32)
    for k in range(TOP_K):
        routed = jnp.where(lane == ids[k], 1.0, routed)
    before = jnp.dot(ltri_ref[...], routed.astype(BF16), preferred_element_type=F32)
    hist_ref[...] = jnp.broadcast_to(jnp.sum(routed, axis=0, keepdims=True),
                                     hist_ref.shape).astype(I32)
    ti = jnp.zeros(logits.shape, I32)
    tw = jnp.zeros(logits.shape, F32)
    for k in range(TOP_K):
        rank = jnp.sum(jnp.where(lane == ids[k], before, 0.0), axis=1, keepdims=True)
        ti = jnp.where(lane == k, ids[k], ti)
        ti = jnp.where(lane == TOP_K + k, rank.astype(I32), ti)
        tw = jnp.where(lane == k, es[k] / den, tw)
    ti_ref[...] = ti
    tw_ref[...] = tw


def _merge(attn, ssm, nat, x, woa, wos, wout, n2g, wr_hi, wr_lo, br, *, tm):
    n = x.shape[0]
    const = lambda shape: pl.BlockSpec(shape, lambda i: (0, 0))
    ltri = jnp.asarray(np.tril(np.ones((tm, tm), np.float32), -1), BF16)
    return pl.pallas_call(
        _merge_kernel,
        out_shape=(jax.ShapeDtypeStruct((n, D_MODEL), F32),
                   jax.ShapeDtypeStruct((n, D_MODEL), BF16),
                   jax.ShapeDtypeStruct((n, LANES), I32),
                   jax.ShapeDtypeStruct((n, LANES), F32),
                   jax.ShapeDtypeStruct((n // tm * 8, LANES), I32)),
        grid=(n // tm,),
        in_specs=[
            pl.BlockSpec((tm, ATTN_WIDTH), lambda i: (i, 0)),
            pl.BlockSpec((tm, SSM_INNER), lambda i: (i, 0)),
            _nat_spec(tm, D_MODEL, lambda i: NAT_G, lambda i: i),
            _nat_spec(tm, D_MODEL, lambda i: NAT_G + D_MODEL, lambda i: i),
            pl.BlockSpec((tm, D_MODEL), lambda i: (i, 0)),
            const((ATTN_WIDTH, D_MODEL)), const((SSM_INNER, D_MODEL)), const((D_MODEL, D_MODEL)),
            const((1, D_MODEL)), const((D_MODEL, LANES)), const((D_MODEL, LANES)),
            const((1, LANES)), const((tm, tm)),
        ],
        out_specs=(pl.BlockSpec((tm, D_MODEL), lambda i: (i, 0)),
                   pl.BlockSpec((tm, D_MODEL), lambda i: (i, 0)),
                   pl.BlockSpec((tm, LANES), lambda i: (i, 0)),
                   pl.BlockSpec((tm, LANES), lambda i: (i, 0)),
                   pl.BlockSpec((8, LANES), lambda i: (i, 0))),
        compiler_params=_cparams(("parallel",)),
        name="merge",
    )(attn, ssm, nat, nat, x, woa, wos, wout, n2g, wr_hi, wr_lo, br, ltri)


def _moe_kernel(te_ref, nv_ref, x_ref, wgu_ref, bgu_ref, wd_ref, bd_ref, o_ref,
                wgu_bf_ref, wd_bf_ref):
    t = pl.program_id(0)
    live = t < nv_ref[0]

    @pl.when(live & ((t == 0) | (te_ref[t] != te_ref[jnp.maximum(t - 1, 0)])))
    def _():
        wgu_bf_ref[...] = wgu_ref[...].astype(BF16)
        wd_bf_ref[...] = wd_ref[...].astype(BF16)

    @pl.when(live)
    def _():
        gu = jnp.dot(x_ref[...], wgu_bf_ref[...], preferred_element_type=F32) + bgu_ref[...]
        gate = jnp.minimum(gu[:, :EXPERT_FF], SWIGLU_LIMIT)
        up = jnp.clip(gu[:, EXPERT_FF:], -SWIGLU_LIMIT, SWIGLU_LIMIT)
        act = (up + 1.0) * (gate * _sigmoid(SWIGLU_ALPHA * gate))
        y = jnp.dot(act.astype(BF16), wd_bf_ref[...], preferred_element_type=F32) + bd_ref[...]
        o_ref[...] = y.astype(o_ref.dtype)

    @pl.when(t >= nv_ref[0])
    def _():
        o_ref[...] = jnp.zeros_like(o_ref)


def _moe(tile_expert, n_valid, x_sorted, wgu, bgu, wd, bd):
    rows = x_sorted.shape[0]
    n_tiles = rows // MOE_TM
    grid_spec = pltpu.PrefetchScalarGridSpec(
        num_scalar_prefetch=2,
        grid=(n_tiles,),
        in_specs=[
            pl.BlockSpec((MOE_TM, D_MODEL), lambda t, te, nv: (t, 0)),
            pl.BlockSpec((None, D_MODEL, 2 * EXPERT_FF), lambda t, te, nv: (te[t], 0, 0)),
            pl.BlockSpec((None, 1, 2 * EXPERT_FF), lambda t, te, nv: (te[t], 0, 0)),
            pl.BlockSpec((None, EXPERT_FF, D_MODEL), lambda t, te, nv: (te[t], 0, 0)),
            pl.BlockSpec((None, 1, D_MODEL), lambda t, te, nv: (te[t], 0, 0)),
        ],
        out_specs=pl.BlockSpec((MOE_TM, D_MODEL), lambda t, te, nv: (t, 0)),
        scratch_shapes=[pltpu.VMEM((D_MODEL, 2 * EXPERT_FF), BF16),
                        pltpu.VMEM((EXPERT_FF, D_MODEL), BF16)],
    )
    return pl.pallas_call(
        _moe_kernel,
        out_shape=jax.ShapeDtypeStruct((rows, D_MODEL), BF16),
        grid_spec=grid_spec,
        compiler_params=_cparams(("arbitrary",)),
        name="moe",
    )(tile_expert, n_valid, x_sorted, wgu, bgu, wd, bd)


def _final_kernel(h1_ref, y0_ref, y1_ref, y2_ref, y3_ref, tw_ref, fg_ref, o_ref):
    tw = tw_ref[...]
    h = h1_ref[...]
    for k, y_ref in enumerate((y0_ref, y1_ref, y2_ref, y3_ref)):
        h = h + tw[:, k:k + 1] * y_ref[...].astype(F32)
    o_ref[...] = _rms_rows(h, fg_ref[...])


def _final(h1, ys, tw, fg, *, tm):
    n = h1.shape[0]
    row = lambda w: pl.BlockSpec((tm, w), lambda i: (i, 0))
    return pl.pallas_call(
        _final_kernel,
        out_shape=jax.ShapeDtypeStruct((n, D_MODEL), F32),
        grid=(n // tm,),
        in_specs=[row(D_MODEL)] * (1 + TOP_K)
                 + [row(LANES), pl.BlockSpec((1, D_MODEL), lambda i: (0, 0))],
        out_specs=row(D_MODEL),
        compiler_params=_cparams(("parallel",)),
        name="final",
    )(h1, *ys, tw, fg)


def _split_w_in(w_in):
    offs = np.cumsum((0,) + IN_SPLITS)
    names = ("q", "k", "v", "qi", "ki", "wi", "z", "xs", "b", "c", "dt", "gate")
    return {nm: w_in[:, offs[i]:offs[i + 1]] for i, nm in enumerate(names)}


def _expert_dispatch(top_i, rank, hist, n_rows_pad):
    n = top_i.shape[0]
    pairs = n * TOP_K
    counts = jnp.sum(hist, axis=0)
    grp_start = jnp.cumsum(counts) - counts
    tiles_per_e = (counts + MOE_TM - 1) // MOE_TM
    tile_end = jnp.cumsum(tiles_per_e)
    row_start = (tile_end - tiles_per_e) * MOE_TM
    offset = row_start - grp_start
    base = row_start[None, :] + jnp.cumsum(hist, axis=0) - hist
    base_tok = jnp.repeat(base, n // hist.shape[0], axis=0)
    hit = top_i[:, :, None] == jnp.arange(N_EXPERTS, dtype=I32)[None, None, :]
    dest = jnp.sum(jnp.where(hit, base_tok[:, None, :], 0), axis=2) + rank
    token_of_pair = jnp.broadcast_to(jnp.arange(n, dtype=I32)[:, None], (n, TOP_K))
    _, order_tok = lax.sort_key_val(dest.reshape(-1), token_of_pair.reshape(-1))
    n_tiles = n_rows_pad // MOE_TM
    n_valid = tile_end[-1].astype(I32)
    tile_ids = jnp.minimum(jnp.arange(n_tiles, dtype=I32), n_valid - 1)
    tile_expert = jnp.searchsorted(tile_end, tile_ids, side="right",
                                   method="compare_all").astype(I32)
    tile_expert = jnp.minimum(tile_expert, N_EXPERTS - 1)
    rows = jnp.arange(n_rows_pad, dtype=I32)
    e_row = jnp.repeat(tile_expert, MOE_TM)
    s_row = rows - offset[e_row]
    live = (s_row >= grp_start[e_row]) & (s_row < (grp_start + counts)[e_row])
    src_token = jnp.where(live, order_tok[jnp.clip(s_row, 0, pairs - 1)], rows % n)
    return src_token, dest, tile_expert, n_valid.reshape(1)


def kernel(x, meta_tokens, norm1_g, w_in, conv_w, conv_b, dt_bias, a_log, d_skip, ssm_norm_g,
           w_o_attn, w_o_ssm, w_out, norm2_g, w_router, b_router, w_gate_up, b_gate_up,
           w_down, b_down, final_g):
    nb, seq, d = x.shape
    n = nb * seq
    assert d == D_MODEL and seq % TQ == 0 and w_in.shape[0] == 1
    xf = x.reshape(n, d)
    meta_pad = jnp.concatenate([jnp.zeros((META_PAD, d), x.dtype), meta_tokens.astype(x.dtype)], 0)
    g1 = norm1_g[0].reshape(1, d)

    w = _split_w_in(w_in[0])
    w_nat = jnp.concatenate([w["k"], w["z"], w["xs"], w["b"], w["c"], w["gate"]], 1).astype(BF16)
    w_nat = jnp.transpose(w_nat.reshape(d, NAT_W // NAT_TILE, NAT_TILE), (1, 0, 2))
    w_qvt = jnp.concatenate([w["q"], w["v"]], 1).T.astype(BF16)
    zpad = lambda c: jnp.zeros((d, c), F32)
    w_sn = jnp.concatenate([w["ki"], zpad(SN_DT - IDX_DIM), w["dt"],
                            zpad(SN_W - SN_DT - SSM_HEADS)], 1).astype(BF16)
    w_st = jnp.concatenate([w["qi"], w["wi"], zpad(ST_W - ST_WI - IDX_HEADS)], 1).T.astype(BF16)

    tabs = _rope_tables(N_META + jnp.arange(seq, dtype=I32))
    tabs_t = tuple(t.T for t in tabs)
    pos_meta = jnp.maximum(jnp.arange(CHUNK, dtype=I32) - META_PAD, 0)
    tabs_m = _rope_tables(pos_meta)
    tabs_mt = tuple(t.T for t in tabs_m)

    tm_nat = 1024 if seq % 1024 == 0 else TQ
    w_t = jnp.concatenate([w_qvt, w_st], axis=0)
    q_slabs = ATTN_WIDTH // LANES
    tr_rope = tuple(range(q_slabs)) + tuple(
        2 * q_slabs + s for s in range(IDX_HEADS * IDX_DIM // LANES))
    nat, sm_n = _proj_nat(xf, g1, w_nat, w_sn, tabs, tm=tm_nat, rope_tiles=1,
                          seq_tiles=seq // tm_nat, name="proj_nat")
    qvt, sm_t = _proj_tr(xf, g1, w_t, tabs_t, tm=TQ, nb=nb, width_main=2 * ATTN_WIDTH,
                         rope_slabs=tr_rope, name="proj_t")
    nat_m, sm_n_m = _proj_nat(meta_pad, g1, w_nat, w_sn, tabs_m, tm=CHUNK, rope_tiles=1,
                              seq_tiles=1, name="proj_nat_meta")
    qvt_m, _ = _proj_tr(meta_pad, g1, w_t, tabs_mt, tm=CHUNK, nb=1, width_main=2 * ATTN_WIDTH,
                        rope_slabs=tr_rope, name="proj_t_meta")

    vt_meta = qvt_m[0, 0, ATTN_WIDTH:, CHUNK - ATTN_META_ROWS:]
    attn = _attention(qvt, nat, sm_t, sm_n, nat_m, vt_meta, nb=nb, seq=seq)

    dt_raw = sm_n[:, SN_DT:SN_DT + SSM_HEADS].reshape(nb, seq // CHUNK, CHUNK, SSM_GROUPS,
                                                      GROUP_HEADS)
    dtt = jnp.transpose(dt_raw, (3, 0, 1, 4, 2))
    dt_raw_m = sm_n_m[:, SN_DT:SN_DT + SSM_HEADS].reshape(CHUNK, SSM_GROUPS, GROUP_HEADS)
    dtt_m = jnp.transpose(dt_raw_m, (1, 2, 0))
    ssm = _ssd(nat, nat_m, dtt, dtt_m, conv_w[0], conv_b[0], dt_bias[0], a_log[0], d_skip[0],
               ssm_norm_g[0], nb=nb, seq=seq)

    wr = jnp.concatenate([w_router[0], jnp.zeros((d, LANES - N_EXPERTS), F32)], 1)
    br = jnp.concatenate([b_router[0], jnp.full((LANES - N_EXPERTS,), -1e30, F32)]).reshape(1, LANES)
    wr_hi = wr.astype(BF16)
    wr_lo = (wr - wr_hi.astype(F32)).astype(BF16)
    tm_merge = 512 if n % 512 == 0 else TQ
    h1, hn2, ti, tw, hist = _merge(attn, ssm, nat, xf, w_o_attn[0].astype(BF16),
                                   w_o_ssm[0].astype(BF16), w_out[0].astype(BF16),
                                   norm2_g[0].reshape(1, d), wr_hi, wr_lo, br, tm=tm_merge)

    n_tiles = n * TOP_K // MOE_TM + N_EXPERTS
    src_token, pos, tile_expert, n_valid = _expert_dispatch(
        ti[:, :TOP_K], ti[:, TOP_K:2 * TOP_K], hist[::8, :N_EXPERTS], n_tiles * MOE_TM)
    x_sorted = hn2.at[src_token].get(mode="promise_in_bounds")
    y_sorted = _moe(tile_expert, n_valid, x_sorted, w_gate_up[0],
                    b_gate_up[0].reshape(N_EXPERTS, 1, -1), w_down[0],
                    b_down[0].reshape(N_EXPERTS, 1, -1))
    ys = [y_sorted.at[pos[:, k]].get(mode="promise_in_bounds") for k in range(TOP_K)]

    out = _final(h1, ys, tw, final_g.reshape(1, d), tm=512 if n % 512 == 0 else TQ)
    return out.reshape(nb, seq, d)
```

```python
import functools
import math

import numpy as np
import jax
import jax.numpy as jnp
from jax import lax
from jax.experimental import pallas as pl
from jax.experimental.pallas import tpu as pltpu

F32 = jnp.float32
BF16 = jnp.bfloat16
I32 = jnp.int32
I16 = jnp.int16

D_MODEL = 1024
N_META = 16
RMS_EPS = 1e-6
N_HEADS = 16
HEAD_DIM = 64
ATTN_WIDTH = N_HEADS * HEAD_DIM
ROT_DIM = HEAD_DIM // 4
ROPE_THETA = 500000.0
IDX_HEADS = 8
IDX_DIM = 64
TOPK_KEYS_MAX = 256
SSM_INNER = 2 * D_MODEL
SSM_HEAD_DIM = 64
SSM_HEADS = SSM_INNER // SSM_HEAD_DIM
SSM_GROUPS = 4
SSM_STATE = 128
CONV_WIDTH = 4
CHUNK = 128
N_EXPERTS = 32
TOP_K = 4
EXPERT_FF = D_MODEL
SWIGLU_LIMIT = 7.0
SWIGLU_ALPHA = 1.702
IN_SPLITS = (ATTN_WIDTH, ATTN_WIDTH, ATTN_WIDTH, IDX_HEADS * IDX_DIM, IDX_DIM, IDX_HEADS,
             SSM_INNER, SSM_INNER, SSM_GROUPS * SSM_STATE, SSM_GROUPS * SSM_STATE, SSM_HEADS,
             2 * D_MODEL)

LANES = 128
GROUP_CH = SSM_INNER // SSM_GROUPS
GROUP_HEADS = SSM_HEADS // SSM_GROUPS
META_PAD = CHUNK - N_META
TQ = 256
ATTN_META_ROWS = 32
FLASH_LAG = 8
MOE_TM = 512
MOE_PARTS = 2
INT_MIN = -2147483648
VMEM_LIMIT = 56 * 1024 * 1024

NAT_K, NAT_Z, NAT_XS, NAT_B, NAT_C, NAT_G = 0, 1024, 3072, 5120, 5632, 6144
NAT_W = 8192
NAT_TILE = 1024


def _nat_spec(rows, width, col, row_block):
    assert NAT_TILE % width == 0

    def index(*grid):
        c = col(*grid)
        return c // NAT_TILE, row_block(*grid), (c % NAT_TILE) // width

    return pl.BlockSpec((None, rows, width), index)
SN_KI, SN_DT, SN_W = 0, 128, 256
ST_QI, ST_WI, ST_W = 0, 512, 640


def _cparams(sem):
    return pltpu.CompilerParams(dimension_semantics=sem, vmem_limit_bytes=VMEM_LIMIT)


def _rms_rows(x, g):
    ms = jnp.mean(x * x, axis=-1, keepdims=True)
    return x * lax.rsqrt(ms + RMS_EPS) * g


def _proj_nat_kernel(x_ref, g_ref, w_ref, wsm_ref, cos_ref, sa_ref, sb_ref, o_ref, osm_ref,
                     hn_ref, *, rope_tiles):
    j = pl.program_id(1)

    def rope(a):
        return (a * cos_ref[...] + pltpu.roll(a, 8, 1) * sa_ref[...]
                + pltpu.roll(a, LANES - 8, 1) * sb_ref[...])

    @pl.when(j == 0)
    def _():
        hn = _rms_rows(x_ref[...], g_ref[...]).astype(hn_ref.dtype)
        hn_ref[...] = hn
        small = jnp.dot(hn, wsm_ref[...], preferred_element_type=F32)
        osm_ref[:, 0:LANES] = rope(small[:, 0:LANES])
        osm_ref[:, LANES:] = small[:, LANES:]

    @pl.when(j < rope_tiles)
    def _():
        acc = jnp.dot(hn_ref[...], w_ref[...], preferred_element_type=F32)
        for s in range(acc.shape[1] // LANES):
            cols = slice(s * LANES, (s + 1) * LANES)
            o_ref[:, cols] = rope(acc[:, cols]).astype(o_ref.dtype)

    @pl.when(j >= rope_tiles)
    def _():
        o_ref[...] = jnp.dot(hn_ref[...], w_ref[...],
                             preferred_element_type=F32).astype(o_ref.dtype)


def _proj_nat(x, g, w_tiles, w_small, tabs, *, tm, rope_tiles, seq_tiles, name):
    n, d = x.shape
    n_col, _, tn = w_tiles.shape
    width_sm = w_small.shape[1]
    cos, sa, sb = tabs
    tab_spec = pl.BlockSpec((tm, LANES), lambda i, j: (i % seq_tiles, 0))
    return pl.pallas_call(
        functools.partial(_proj_nat_kernel, rope_tiles=rope_tiles),
        out_shape=(jax.ShapeDtypeStruct((n_col, n, tn), BF16),
                   jax.ShapeDtypeStruct((n, width_sm), F32)),
        grid=(n // tm, n_col),
        in_specs=[
            pl.BlockSpec((tm, d), lambda i, j: (i, 0)),
            pl.BlockSpec((1, d), lambda i, j: (0, 0)),
            pl.BlockSpec((None, d, tn), lambda i, j: (j, 0, 0)),
            pl.BlockSpec((d, width_sm), lambda i, j: (0, 0)),
            tab_spec, tab_spec, tab_spec,
        ],
        out_specs=(pl.BlockSpec((None, tm, tn), lambda i, j: (j, i, 0)),
                   pl.BlockSpec((tm, width_sm), lambda i, j: (i, 0))),
        scratch_shapes=[pltpu.VMEM((tm, d), BF16)],
        compiler_params=_cparams(("parallel", "arbitrary")),
        name=name,
    )(x, g, w_tiles, w_small, cos, sa, sb)


def _proj_tr_kernel(x_ref, g_ref, wt_ref, cos_ref, sa_ref, sb_ref, o_ref, osm_ref, *,
                    rope_slabs):
    hn = _rms_rows(x_ref[...], g_ref[...]).astype(BF16)
    acc = lax.dot_general(wt_ref[...], hn, (((1,), (1,)), ((), ())),
                          preferred_element_type=F32)
    cos, sa, sb = cos_ref[...], sa_ref[...], sb_ref[...]
    n_main = o_ref.shape[0] // LANES
    for s in range(acc.shape[0] // LANES):
        a = acc[s * LANES:(s + 1) * LANES, :]
        if s in rope_slabs:
            a = a * cos + pltpu.roll(a, 8, 0) * sa + pltpu.roll(a, LANES - 8, 0) * sb
        if s < n_main:
            o_ref[s * LANES:(s + 1) * LANES, :] = a.astype(o_ref.dtype)
        else:
            osm_ref[(s - n_main) * LANES:(s - n_main + 1) * LANES, :] = a


def _proj_tr(x, g, wt, tabs_t, *, tm, nb, width_main, rope_slabs, name):
    n, d = x.shape
    width = wt.shape[0]
    width_sm = width - width_main
    seq_tiles = n // nb // tm
    cos, sa, sb = tabs_t
    tab_spec = pl.BlockSpec((LANES, tm), lambda i: (0, i % seq_tiles))
    out_map = lambda i: (i // seq_tiles, i % seq_tiles, 0, 0)
    return pl.pallas_call(
        functools.partial(_proj_tr_kernel, rope_slabs=rope_slabs),
        out_shape=(jax.ShapeDtypeStruct((nb, seq_tiles, width_main, tm), BF16),
                   jax.ShapeDtypeStruct((nb, seq_tiles, width_sm, tm), F32)),
        grid=(n // tm,),
        in_specs=[
            pl.BlockSpec((tm, d), lambda i: (i, 0)),
            pl.BlockSpec((1, d), lambda i: (0, 0)),
            pl.BlockSpec((width, d), lambda i: (0, 0)),
            tab_spec, tab_spec, tab_spec,
        ],
        out_specs=(pl.BlockSpec((None, None, width_main, tm), out_map),
                   pl.BlockSpec((None, None, width_sm, tm), out_map)),
        compiler_params=_cparams(("parallel",)),
        name=name,
    )(x, g, wt, cos, sa, sb)


def _rope_tables(pos):
    half = ROT_DIM // 2
    inv_freq = jnp.exp(-math.log(ROPE_THETA) * jnp.arange(half, dtype=F32) / half)
    ang = pos.astype(F32)[:, None] * inv_freq[None, :]
    cos8, sin8 = jnp.cos(ang), jnp.sin(ang)
    t = pos.shape[0]
    cc = np.arange(LANES) % HEAD_DIM
    in_rot = jnp.asarray(cc < ROT_DIM)
    lo = jnp.asarray(cc < half)
    hi = jnp.asarray((cc >= half) & (cc < ROT_DIM))
    idx = jnp.asarray(cc % half)
    cos_l = jnp.take(cos8, idx, axis=1)
    sin_l = jnp.take(sin8, idx, axis=1)
    cos = jnp.where(in_rot[None, :], cos_l, 1.0)
    sa = jnp.where(hi[None, :], sin_l, 0.0)
    sb = jnp.where(lo[None, :], -sin_l, 0.0)
    del t
    return cos, sa, sb


def _attn_kernel(qt_ref, vt_ref, k_ref, qit_ref, wit_ref, ki_ref, kmeta_ref, vtmeta_ref,
                 o_ref, keys_ref, hi_ref, lo_ref, bias_ref, qm_ref, m_ref, l_ref, acc_ref, s_ref,
                 *, ksel, nbits):
    j = pl.program_id(1)
    tq = TQ
    nch = j + 1
    int_min = jnp.int32(INT_MIN)
    neg_inf = jnp.float32(-jnp.inf)

    row128 = lax.broadcasted_iota(I32, (LANES, tq), 0)
    for h in range(N_HEADS):
        slab = qt_ref[(h // 2) * LANES:(h // 2 + 1) * LANES, :]
        keep = (row128 < HEAD_DIM) if h % 2 == 0 else (row128 >= HEAD_DIM)
        scaled = slab.astype(F32) * (HEAD_DIM ** -0.5 * math.log2(math.e))
        qm_ref[h] = jnp.where(keep, scaled, 0.0).astype(qm_ref.dtype)

    w_all = wit_ref[0:IDX_HEADS, :] * (IDX_HEADS ** -0.5 * IDX_DIM ** -0.5)
    qi_bf = [qit_ref[h * IDX_DIM:(h + 1) * IDX_DIM, :].astype(BF16) for h in range(IDX_HEADS)]
    qidx = j * tq + lax.broadcasted_iota(I32, (tq, tq), 1)
    krow = lax.broadcasted_iota(I32, (tq, tq), 0)

    def score_body(c, carry):
        start = pl.multiple_of(c * tq, tq)
        kic = ki_ref[pl.ds(start, tq), :][:, :IDX_DIM].astype(BF16)
        sc = jnp.zeros((tq, tq), F32)
        for h in range(IDX_HEADS):
            lg = jnp.dot(kic, qi_bf[h], preferred_element_type=F32)
            sc = sc + jnp.maximum(lg, 0.0) * w_all[h:h + 1, :]
        bits = pltpu.bitcast(sc, I32)
        skey = jnp.where(bits < 0, bits ^ jnp.int32(0x7FFFFFFF), bits)
        skey = jnp.where(c * tq + krow <= qidx, skey, int_min)
        keys_ref[c] = skey
        hi_ref[c] = (skey >> 16).astype(I16)
        lo_ref[c] = ((skey & 0xFFFF) - 32768).astype(I16)
        return carry

    lax.fori_loop(0, nch, score_body, 0)

    @pl.when(nch % 2 == 1)
    def _():
        hi_ref[nch] = jnp.full((tq, tq), -32768, I16)
        lo_ref[nch] = jnp.full((tq, tq), -32768, I16)

    npair = (nch + 1) // 2

    def count(pred):
        def body(c, cnt):
            hit = pred(keys_ref[c], c).astype(I32)
            return cnt + hit.reshape(tq // 8, 8, tq).sum(axis=0)
        cnt8 = lax.fori_loop(0, nch, body, jnp.zeros((8, tq), I32))
        return jnp.sum(cnt8, axis=0, keepdims=True)

    def count16(ref, cand):
        cand16 = cand.astype(I16)
        one, zero = jnp.int16(1), jnp.int16(0)

        def body(i, cnt):
            for c in (2 * i, 2 * i + 1):
                hit = pltpu.bitcast(jnp.where(ref[c] >= cand16, one, zero), I32)
                cnt = cnt + hit.reshape(tq // 16, 8, tq).sum(axis=0)
            return cnt

        cnt8 = lax.fori_loop(0, npair, body, jnp.zeros((8, tq), I32))
        both = jnp.sum(cnt8, axis=0, keepdims=True)
        return (both & 0xFFFF) + lax.shift_right_logical(both, 16)

    def search16(ref, want):
        def bit_body(it, t_b):
            cand_b = t_b | lax.shift_left(jnp.int32(1), 15 - it)
            cnt = count16(ref, cand_b - 32768)
            return jnp.where(cnt >= want, cand_b, t_b)
        return lax.fori_loop(0, 16, bit_body, jnp.zeros((1, tq), I32))

    hi_b = search16(hi_ref, ksel)
    thr_hi = hi_b - 32768
    n_above = count16(hi_ref, jnp.minimum(thr_hi + 1, 32767))
    n_above = jnp.where(thr_hi >= 32767, 0, n_above)
    thr_hi16 = thr_hi.astype(I16)

    def bucket_body(c, carry):
        lo_ref[c] = jnp.where(hi_ref[c] == thr_hi16, lo_ref[c], jnp.int16(-32768))
        return carry

    lax.fori_loop(0, 2 * npair, bucket_body, 0)
    lo_b = search16(lo_ref, ksel - n_above)
    thr = lax.shift_left(thr_hi, 16) | lo_b
    n_gt = count(lambda kc, c: kc > thr)
    n_eq = count(lambda kc, c: kc == thr)
    need = ksel - n_gt
    fix = (n_eq > need) & (thr != int_min)
    thr_eff = jnp.maximum(thr, int_min + 1)

    def bias_body(c, carry):
        bias_ref[c] = jnp.where(keys_ref[c] >= thr_eff, 0.0, neg_inf)
        return carry

    lax.fori_loop(0, nch, bias_body, 0)

    @pl.when(jnp.max(fix.astype(I32)) > 0)
    def _():
        def idx_body(it, v):
            cand = v | lax.shift_left(jnp.int32(1), nbits - 1 - it)
            cnt = count(lambda kc, c: jnp.where(c * tq + krow < cand, kc, int_min) == thr)
            return jnp.where(cnt <= need - 1, cand, v)

        last = lax.fori_loop(0, nbits, idx_body, jnp.zeros((1, tq), I32))

        def fix_body(c, carry):
            kc = keys_ref[c]
            tie_lim = jnp.where(fix, last, jnp.int32(2 ** 30))
            tie_ok = jnp.where(c * tq + krow <= tie_lim, thr_eff, thr_eff + 1)
            bias_ref[c] = jnp.where(kc >= tie_ok, 0.0, neg_inf)
            return carry

        lax.fori_loop(0, nch, fix_body, 0)

    def chunk_step(kslab_of, vt_of, bias_c, n_keys, first):
        alphas = []
        ones = jnp.ones((16, n_keys), BF16)

        def scores(h):
            s = jnp.dot(kslab_of(h), qm_ref[h], preferred_element_type=F32) + bias_c
            s_ref[h, 0:n_keys, :] = s
            smax = jnp.max(s, axis=0, keepdims=True)
            if first:
                m_ref[h:h + 1, :] = smax
            else:
                m_old = m_ref[h:h + 1, :]
                m_new = jnp.maximum(m_old, smax)
                alphas.append(jnp.exp2(m_old - m_new))
                m_ref[h:h + 1, :] = m_new

        def values(h):
            p = jnp.exp2(s_ref[h, 0:n_keys, :] - m_ref[h:h + 1, :]).astype(BF16)
            res = jnp.dot(jnp.concatenate([vt_of(h), ones], axis=0), p,
                          preferred_element_type=F32)
            pv, psum = res[:HEAD_DIM, :], res[HEAD_DIM:HEAD_DIM + 1, :]
            rows = slice(h * HEAD_DIM, (h + 1) * HEAD_DIM)
            if first:
                l_ref[h:h + 1, :] = psum
                acc_ref[rows, :] = pv
            else:
                l_ref[h:h + 1, :] = l_ref[h:h + 1, :] * alphas[h] + psum
                acc_ref[rows, :] = acc_ref[rows, :] * alphas[h] + pv

        for h in range(N_HEADS + FLASH_LAG):
            if h < N_HEADS:
                scores(h)
            if h >= FLASH_LAG:
                values(h - FLASH_LAG)

    meta_row = lax.broadcasted_iota(I32, (ATTN_META_ROWS, tq), 0)
    bias_meta = jnp.where(meta_row >= ATTN_META_ROWS - N_META, 0.0, neg_inf)
    chunk_step(lambda h: kmeta_ref[:, (h // 2) * LANES:(h // 2 + 1) * LANES],
               lambda h: vtmeta_ref[h * HEAD_DIM:(h + 1) * HEAD_DIM, :], bias_meta,
               ATTN_META_ROWS, True)

    def flash_body(c, carry):
        start = pl.multiple_of(c * tq, tq)
        chunk_step(lambda h: k_ref[pl.ds(start, tq), (h // 2) * LANES:(h // 2 + 1) * LANES],
                   lambda h: vt_ref[c, h * HEAD_DIM:(h + 1) * HEAD_DIM, :], bias_ref[c], tq, False)
        return carry

    lax.fori_loop(0, nch, flash_body, 0)

    for h in range(N_HEADS):
        rows = slice(h * HEAD_DIM, (h + 1) * HEAD_DIM)
        acc_ref[rows, :] = acc_ref[rows, :] * (1.0 / l_ref[h:h + 1, :])
    o_ref[...] = acc_ref[...].T.astype(o_ref.dtype)


def _attention(qvt, nat, sm_t, sm_n, nat_meta, qvt_meta, *, nb, seq):
    nq = seq // TQ
    ksel = min(TOPK_KEYS_MAX, seq // 4)
    nbits = max(1, int(math.ceil(math.log2(seq))))
    kernel = functools.partial(_attn_kernel, ksel=ksel, nbits=nbits)
    return pl.pallas_call(
        kernel,
        out_shape=jax.ShapeDtypeStruct((nb * seq, ATTN_WIDTH), BF16),
        grid=(nb, nq),
        in_specs=[
            pl.BlockSpec((None, None, ATTN_WIDTH, TQ), lambda b, j: (b, j, 0, 0)),
            pl.BlockSpec((None, nq, ATTN_WIDTH, TQ), lambda b, j: (b, 0, 1, 0)),
            _nat_spec(seq, ATTN_WIDTH, lambda b, j: NAT_K, lambda b, j: b),
            pl.BlockSpec((None, None, IDX_HEADS * IDX_DIM, TQ), lambda b, j: (b, j, 0, 0)),
            pl.BlockSpec((None, None, LANES, TQ), lambda b, j: (b, j, ST_WI // LANES, 0)),
            pl.BlockSpec((seq, LANES), lambda b, j: (b, SN_KI // LANES)),
            _nat_spec(ATTN_META_ROWS, ATTN_WIDTH, lambda b, j: NAT_K,
                      lambda b, j: CHUNK // ATTN_META_ROWS - 1),
            pl.BlockSpec((ATTN_WIDTH, ATTN_META_ROWS), lambda b, j: (0, 0)),
        ],
        out_specs=pl.BlockSpec((TQ, ATTN_WIDTH), lambda b, j: (b * nq + j, 0)),
        scratch_shapes=[
            pltpu.VMEM((nq, TQ, TQ), I32),
            pltpu.VMEM((nq + nq % 2, TQ, TQ), I16),
            pltpu.VMEM((nq + nq % 2, TQ, TQ), I16),
            pltpu.VMEM((nq, TQ, TQ), F32),
            pltpu.VMEM((N_HEADS, LANES, TQ), BF16),
            pltpu.VMEM((N_HEADS, TQ), F32),
            pltpu.VMEM((N_HEADS, TQ), F32),
            pltpu.VMEM((ATTN_WIDTH, TQ), F32),
            pltpu.VMEM((N_HEADS, TQ, TQ), F32),
        ],
        compiler_params=_cparams(("parallel", "arbitrary")),
        name="attn",
    )(qvt, qvt, nat, sm_t, sm_t, sm_n, nat_meta, qvt_meta)


def _softplus(x):
    return jnp.maximum(x, 0.0) + jnp.log1p(jnp.exp(-jnp.abs(x)))


def _sigmoid(x):
    return 0.5 + 0.5 * jnp.tanh(0.5 * x)


def _silu(x):
    h = 0.5 * x
    return h + h * jnp.tanh(h)


def _pad_transpose(rows):
    padded = jnp.concatenate([rows, jnp.zeros((LANES - rows.shape[0], LANES), rows.dtype)], axis=0)
    return padded.T


STAGE_TAIL = 16
SSD_GROUPS_PER_STEP = 2


def _ssd_kernel(xs_ref, z_ref, b_ref, c_ref, dtt_ref,
                xs_m_ref, b_m_ref, c_m_ref, dtt_m_ref,
                cw_ref, cb_ref, dtb_ref, alog_ref, dsk_ref, ng_ref, exp_ref, tri_ref, shift_ref,
                o_ref, state_ref, dt_all_ref, cs_all_ref, *, n_chunks):
    gs = range(SSD_GROUPS_PER_STEP)
    a_neg = [-jnp.exp(alog_ref[g]) for g in gs]
    dt_bias = [dtb_ref[g] for g in gs]
    expand = exp_ref[...]
    tri_u = tri_ref[...]
    li = lax.broadcasted_iota(I32, (CHUNK, CHUNK), 0)
    si = lax.broadcasted_iota(I32, (CHUNK, CHUNK), 1)
    tril = si <= li
    lane = lax.broadcasted_iota(I32, (CHUNK, LANES), 1)
    low_half = lane < SSM_HEAD_DIM
    neg_inf = jnp.float32(-jnp.inf)

    xb, xc = GROUP_CH, GROUP_CH + SSM_STATE

    def cumsum_tokens(rows):
        return jnp.dot(rows, tri_u, precision=lax.Precision.HIGHEST, preferred_element_type=F32)

    def token_major(dt_r, cs_r):
        dt_c = _pad_transpose(dt_r)
        cs_c = _pad_transpose(cs_r)
        dec_c = jnp.exp(cs_c[CHUNK - 1:CHUNK, :] - cs_c)
        return cs_c, dt_c.astype(BF16), (dt_c * dec_c).astype(BF16), jnp.exp(cs_c).astype(BF16)

    widths = (GROUP_CH, SSM_STATE, SSM_STATE)
    gcols = lambda g: slice(g * GROUP_CH, (g + 1) * GROUP_CH)

    def pieces(refs, rows, g):
        return [r[rows, g * w:(g + 1) * w] for r, w in zip(refs, widths)]

    def chunk(src, rows, tail_src, trows, dt_of, out_rows, is_meta):
        stacks = []
        for g in gs:
            tail = ([jnp.zeros((STAGE_TAIL, w), BF16) for w in widths] if tail_src is None
                    else pieces(tail_src, trows, g))
            stacks.append(jnp.concatenate(
                [jnp.concatenate(tail, axis=1), jnp.concatenate(pieces(src, rows, g), axis=1)],
                axis=0))
        shifted = [jnp.dot(shift_ref[...], stacks[g], preferred_element_type=F32) for g in gs]
        x, bm_bf, cm_bf = [], [], []
        for g in gs:
            conv = cb_ref[g]
            for k in range(CONV_WIDTH):
                conv = conv + (shifted[g][k * CHUNK:(k + 1) * CHUNK, :]
                               * cw_ref[g, CONV_WIDTH - 1 - k:CONV_WIDTH - k, :])
            act = _silu(conv)
            x.append(act[:, 0:xb])
            bm_bf.append(act[:, xb:xc].astype(BF16))
            cm_bf.append(act[:, xc:].astype(BF16))
        if is_meta:
            dt_r = []
            for g in gs:
                d = _softplus(dt_of(g) + dt_bias[g])
                dt_r.append(jnp.where(lax.broadcasted_iota(I32, d.shape, 1) >= META_PAD, d, 0.0))
            cs_r = [cumsum_tokens(dt_r[g] * a_neg[g]) for g in gs]
        else:
            dt_r, cs_r = zip(*[dt_of(g) for g in gs])
        cs_c, dt_bf, dd_bf, ec_bf = zip(*[token_major(dt_r[g], cs_r[g]) for g in gs])
        dtx = [jnp.dot(dt_bf[g], expand, preferred_element_type=F32) for g in gs]
        decx = [jnp.dot(dd_bf[g], expand, preferred_element_type=F32) for g in gs]
        ecsx = [jnp.dot(ec_bf[g], expand, preferred_element_type=F32) for g in gs]
        state = [state_ref[g] for g in gs]
        new_part = [lax.dot_general(bm_bf[g], (x[g] * decx[g]).astype(BF16),
                                    (((0,), (0,)), ((), ())), preferred_element_type=F32)
                    for g in gs]
        if not is_meta:
            xdt = [(x[g] * dtx[g]).astype(BF16) for g in gs]
            cb = [lax.dot_general(cm_bf[g], bm_bf[g], (((1,), (1,)), ((), ())),
                                  preferred_element_type=F32) for g in gs]
            y_off = [jnp.dot(cm_bf[g], state[g].astype(BF16), preferred_element_type=F32)
                     * ecsx[g] for g in gs]
            slabs = [[] for _ in gs]
            for pp in range(GROUP_HEADS // 2):
                halves = [[] for _ in gs]
                for e in (2 * pp, 2 * pp + 1):
                    for g in gs:
                        diff = cs_c[g][:, e:e + 1] - cs_r[g][e:e + 1, :]
                        lmat = jnp.exp(jnp.where(tril, diff, neg_inf))
                        halves[g].append(jnp.dot((cb[g] * lmat).astype(BF16),
                                                 xdt[g][:, pp * LANES:(pp + 1) * LANES],
                                                 preferred_element_type=F32))
                for g in gs:
                    slabs[g].append(jnp.where(low_half, halves[g][0], halves[g][1]))
            for g in gs:
                y = jnp.concatenate(slabs[g], axis=1) + y_off[g] + x[g] * dsk_ref[:, gcols(g)]
                y = y * _silu(z_ref[rows, gcols(g)].astype(F32))
                ms = jnp.mean(y * y, axis=-1, keepdims=True)
                o_ref[out_rows, gcols(g)] = (y * lax.rsqrt(ms + RMS_EPS)
                                             * ng_ref[:, gcols(g)]).astype(o_ref.dtype)
        for g in gs:
            state_ref[g] = state[g] * ecsx[g][CHUNK - 1:CHUNK, :] + new_part[g]

    state_ref[...] = jnp.zeros_like(state_ref)
    n_rows = n_chunks * GROUP_HEADS
    for g in gs:
        dt_all = _softplus(dtt_ref[g].reshape(n_rows, CHUNK) + jnp.tile(dt_bias[g], (n_chunks, 1)))
        dt_all_ref[g] = dt_all
        cs_all_ref[g] = cumsum_tokens(dt_all * jnp.tile(a_neg[g], (n_chunks, 1)))

    def heads_of(ci):
        return (slice(ci * GROUP_HEADS, (ci + 1) * GROUP_HEADS) if isinstance(ci, int)
                else pl.ds(pl.multiple_of(ci * GROUP_HEADS, GROUP_HEADS), GROUP_HEADS))

    def dt_cs(ci):
        return lambda g: (dt_all_ref[g, heads_of(ci), :], cs_all_ref[g, heads_of(ci), :])

    real, meta = (xs_ref, b_ref, c_ref), (xs_m_ref, b_m_ref, c_m_ref)
    all_rows = slice(0, CHUNK)
    chunk(meta, all_rows, None, None, lambda g: dtt_m_ref[g], None, True)
    chunk(real, all_rows, meta, slice(CHUNK - STAGE_TAIL, CHUNK), dt_cs(0), all_rows, False)

    def body(ci, carry):
        rows = pl.ds(pl.multiple_of(ci * CHUNK, CHUNK), CHUNK)
        trows = pl.ds(pl.multiple_of(ci * CHUNK - STAGE_TAIL, STAGE_TAIL), STAGE_TAIL)
        chunk(real, rows, real, trows, dt_cs(ci), rows, False)
        return carry

    lax.fori_loop(1, n_chunks, body, 0)


def _ssd(nat, nat_meta, dtt, dtt_meta, conv_w, conv_b, dt_bias, a_log, d_skip, ssm_norm_g,
         *, nb, seq):
    gch, st = GROUP_CH, SSM_STATE
    expand = np.zeros((LANES, gch), np.float32)
    for e in range(GROUP_HEADS):
        expand[e, e * SSM_HEAD_DIM:(e + 1) * SSM_HEAD_DIM] = 1.0
    tri = np.triu(np.ones((CHUNK, CHUNK), np.float32))
    def group_cols(p):
        px = p[:, :SSM_INNER].reshape(-1, SSM_GROUPS, gch)
        pb = p[:, SSM_INNER:SSM_INNER + SSM_GROUPS * st].reshape(-1, SSM_GROUPS, st)
        pc = p[:, SSM_INNER + SSM_GROUPS * st:].reshape(-1, SSM_GROUPS, st)
        return jnp.transpose(jnp.concatenate([px, pb, pc], axis=2), (1, 0, 2))

    cw = group_cols(conv_w)
    cb = group_cols(conv_b.reshape(1, -1))
    stage_w = gch + 2 * st
    shift = np.zeros((CONV_WIDTH * CHUNK, STAGE_TAIL + CHUNK), np.float32)
    for k in range(CONV_WIDTH):
        shift[k * CHUNK + np.arange(CHUNK), STAGE_TAIL + np.arange(CHUNK) - k] = 1.0
    dsk = jnp.repeat(d_skip, SSM_HEAD_DIM).reshape(1, SSM_INNER)
    ng = ssm_norm_g.reshape(1, SSM_INNER)
    dtb = dt_bias.reshape(SSM_GROUPS, GROUP_HEADS, 1)
    alog = a_log.reshape(SSM_GROUPS, GROUP_HEADS, 1)
    gps = SSD_GROUPS_PER_STEP
    wide, narrow = gps * gch, gps * st
    batch_row, first_row = (lambda b, g: b), (lambda b, g: 0)
    return pl.pallas_call(
        functools.partial(_ssd_kernel, n_chunks=seq // CHUNK),
        out_shape=jax.ShapeDtypeStruct((nb * seq, SSM_INNER), BF16),
        grid=(nb, SSM_GROUPS // gps),
        in_specs=[
            _nat_spec(seq, wide, lambda b, g: NAT_XS + g * wide, batch_row),
            _nat_spec(seq, wide, lambda b, g: NAT_Z + g * wide, batch_row),
            _nat_spec(seq, narrow, lambda b, g: NAT_B + g * narrow, batch_row),
            _nat_spec(seq, narrow, lambda b, g: NAT_C + g * narrow, batch_row),
            pl.BlockSpec((gps, None, seq // CHUNK, GROUP_HEADS, CHUNK),
                         lambda b, g: (g, b, 0, 0, 0)),
            _nat_spec(CHUNK, wide, lambda b, g: NAT_XS + g * wide, first_row),
            _nat_spec(CHUNK, narrow, lambda b, g: NAT_B + g * narrow, first_row),
            _nat_spec(CHUNK, narrow, lambda b, g: NAT_C + g * narrow, first_row),
            pl.BlockSpec((gps, GROUP_HEADS, CHUNK), lambda b, g: (g, 0, 0)),
            pl.BlockSpec((gps, CONV_WIDTH, stage_w), lambda b, g: (g, 0, 0)),
            pl.BlockSpec((gps, 1, stage_w), lambda b, g: (g, 0, 0)),
            pl.BlockSpec((gps, GROUP_HEADS, 1), lambda b, g: (g, 0, 0)),
            pl.BlockSpec((gps, GROUP_HEADS, 1), lambda b, g: (g, 0, 0)),
            pl.BlockSpec((1, gps * gch), lambda b, g: (0, g)),
            pl.BlockSpec((1, gps * gch), lambda b, g: (0, g)),
            pl.BlockSpec((LANES, gch), lambda b, g: (0, 0)),
            pl.BlockSpec((CHUNK, CHUNK), lambda b, g: (0, 0)),
            pl.BlockSpec((CONV_WIDTH * CHUNK, STAGE_TAIL + CHUNK), lambda b, g: (0, 0)),
        ],
        out_specs=pl.BlockSpec((seq, gps * gch), lambda b, g: (b, g)),
        scratch_shapes=[pltpu.VMEM((gps, SSM_STATE, gch), F32),
                        pltpu.VMEM((gps, seq // CHUNK * GROUP_HEADS, CHUNK), F32),
                        pltpu.VMEM((gps, seq // CHUNK * GROUP_HEADS, CHUNK), F32)],
        compiler_params=_cparams(("parallel", "arbitrary")),
        name="ssd",
    )(nat, nat, nat, nat, dtt, nat_meta, nat_meta, nat_meta, dtt_meta,
      cw, cb, dtb, alog, dsk, ng,
      jnp.asarray(expand, BF16), jnp.asarray(tri), jnp.asarray(shift, BF16))


def _merge_kernel(attn_ref, ssm_ref, g0_ref, g1_ref, x_ref, woa_ref, wos_ref, wout_ref,
                  n2g_ref, wrh_ref, wrl_ref, br_ref, ltri_ref,
                  h1_ref, hn2_ref, ti_ref, tw_ref, hist_ref):
    a = jnp.dot(attn_ref[...], woa_ref[...], preferred_element_type=F32)
    s = jnp.dot(ssm_ref[...], wos_ref[...], preferred_element_type=F32)
    g0 = _sigmoid(g0_ref[...].astype(F32))
    g1 = _sigmoid(g1_ref[...].astype(F32))
    u = g0 * a + g1 * s
    h1 = x_ref[...] + jnp.dot(u.astype(BF16), wout_ref[...], preferred_element_type=F32)
    h1_ref[...] = h1
    hn2 = _rms_rows(h1, n2g_ref[...])
    hn2_hi = hn2.astype(BF16)
    hn2_ref[...] = hn2_hi
    hn2_lo = (hn2 - hn2_hi.astype(F32)).astype(BF16)
    logits = (jnp.dot(hn2_hi, wrh_ref[...], preferred_element_type=F32)
              + jnp.dot(hn2_lo, wrh_ref[...], preferred_element_type=F32)
              + jnp.dot(hn2_hi, wrl_ref[...], preferred_element_type=F32)
              + br_ref[...])
    lane = lax.broadcasted_iota(I32, logits.shape, 1)
    work = logits
    ids, vals = [], []
    for _ in range(TOP_K):
        mx = jnp.max(work, axis=1, keepdims=True)
        idx = jnp.min(jnp.where(work == mx, lane, LANES), axis=1, keepdims=True)
        ids.append(idx)
        vals.append(mx)
        work = jnp.where(lane == idx, -jnp.inf, work)
    es = [jnp.exp(v - vals[0]) for v in vals]
    den = es[0] + es[1] + es[2] + es[3]
    routed = jnp.zeros(logits.shape, F32)
    for k in range(TOP_K):
        routed = jnp.where(lane == ids[k], 1.0, routed)
    before = jnp.dot(ltri_ref[...], routed.astype(BF16), preferred_element_type=F32)
    hist_ref[...] = jnp.broadcast_to(jnp.sum(routed, axis=0, keepdims=True),
                                     hist_ref.shape).astype(I32)
    ti = jnp.zeros(logits.shape, I32)
    tw = jnp.zeros(logits.shape, F32)
    for k in range(TOP_K):
        rank = jnp.sum(jnp.where(lane == ids[k], before, 0.0), axis=1, keepdims=True)
        ti = jnp.where(lane == k, ids[k], ti)
        ti = jnp.where(lane == TOP_K + k, rank.astype(I32), ti)
        tw = jnp.where(lane == k, es[k] / den, tw)
    ti_ref[...] = ti
    tw_ref[...] = tw


def _merge(attn, ssm, nat, x, woa, wos, wout, n2g, wr_hi, wr_lo, br, *, tm):
    n = x.shape[0]
    const = lambda shape: pl.BlockSpec(shape, lambda i: (0, 0))
    ltri = jnp.asarray(np.tril(np.ones((tm, tm), np.float32), -1), BF16)
    return pl.pallas_call(
        _merge_kernel,
        out_shape=(jax.ShapeDtypeStruct((n, D_MODEL), F32),
                   jax.ShapeDtypeStruct((n, D_MODEL), BF16),
                   jax.ShapeDtypeStruct((n, LANES), I32),
                   jax.ShapeDtypeStruct((n, LANES), F32),
                   jax.ShapeDtypeStruct((n // tm * 8, LANES), I32)),
        grid=(n // tm,),
        in_specs=[
            pl.BlockSpec((tm, ATTN_WIDTH), lambda i: (i, 0)),
            pl.BlockSpec((tm, SSM_INNER), lambda i: (i, 0)),
            _nat_spec(tm, D_MODEL, lambda i: NAT_G, lambda i: i),
            _nat_spec(tm, D_MODEL, lambda i: NAT_G + D_MODEL, lambda i: i),
            pl.BlockSpec((tm, D_MODEL), lambda i: (i, 0)),
            const((ATTN_WIDTH, D_MODEL)), const((SSM_INNER, D_MODEL)), const((D_MODEL, D_MODEL)),
            const((1, D_MODEL)), const((D_MODEL, LANES)), const((D_MODEL, LANES)),
            const((1, LANES)), const((tm, tm)),
        ],
        out_specs=(pl.BlockSpec((tm, D_MODEL), lambda i: (i, 0)),
                   pl.BlockSpec((tm, D_MODEL), lambda i: (i, 0)),
                   pl.BlockSpec((tm, LANES), lambda i: (i, 0)),
                   pl.BlockSpec((tm, LANES), lambda i: (i, 0)),
                   pl.BlockSpec((8, LANES), lambda i: (i, 0))),
        compiler_params=_cparams(("parallel",)),
        name="merge",
    )(attn, ssm, nat, nat, x, woa, wos, wout, n2g, wr_hi, wr_lo, br, ltri)


def _moe_kernel(te_ref, nv_ref, x_ref, wgu_ref, bgu_ref, wd_ref, bd_ref, o_ref,
                wgu_bf_ref, wd_bf_ref):
    t = pl.program_id(0)
    live = t < nv_ref[0]

    @pl.when(live & ((t == 0) | (te_ref[t] != te_ref[jnp.maximum(t - 1, 0)])))
    def _():
        wgu_bf_ref[...] = wgu_ref[...].astype(BF16)
        wd_bf_ref[...] = wd_ref[...].astype(BF16)

    @pl.when(live)
    def _():
        gu = jnp.dot(x_ref[...], wgu_bf_ref[...], preferred_element_type=F32) + bgu_ref[...]
        gate = jnp.minimum(gu[:, :EXPERT_FF], SWIGLU_LIMIT)
        up = jnp.clip(gu[:, EXPERT_FF:], -SWIGLU_LIMIT, SWIGLU_LIMIT)
        act = (up + 1.0) * (gate * _sigmoid(SWIGLU_ALPHA * gate))
        y = jnp.dot(act.astype(BF16), wd_bf_ref[...], preferred_element_type=F32) + bd_ref[...]
        o_ref[...] = y.astype(o_ref.dtype)

    @pl.when(t >= nv_ref[0])
    def _():
        o_ref[...] = jnp.zeros_like(o_ref)


def _moe(tile_expert, n_valid, x_sorted, wgu, bgu, wd, bd):
    rows = x_sorted.shape[0]
    n_tiles = rows // MOE_TM
    grid_spec = pltpu.PrefetchScalarGridSpec(
        num_scalar_prefetch=2,
        grid=(n_tiles,),
        in_specs=[
            pl.BlockSpec((MOE_TM, D_MODEL), lambda t, te, nv: (t, 0)),
            pl.BlockSpec((None, D_MODEL, 2 * EXPERT_FF), lambda t, te, nv: (te[t], 0, 0)),
            pl.BlockSpec((None, 1, 2 * EXPERT_FF), lambda t, te, nv: (te[t], 0, 0)),
            pl.BlockSpec((None, EXPERT_FF, D_MODEL), lambda t, te, nv: (te[t], 0, 0)),
            pl.BlockSpec((None, 1, D_MODEL), lambda t, te, nv: (te[t], 0, 0)),
        ],
        out_specs=pl.BlockSpec((MOE_TM, D_MODEL), lambda t, te, nv: (t, 0)),
        scratch_shapes=[pltpu.VMEM((D_MODEL, 2 * EXPERT_FF), BF16),
                        pltpu.VMEM((EXPERT_FF, D_MODEL), BF16)],
    )
    return pl.pallas_call(
        _moe_kernel,
        out_shape=jax.ShapeDtypeStruct((rows, D_MODEL), BF16),
        grid_spec=grid_spec,
        compiler_params=_cparams(("arbitrary",)),
        name="moe",
    )(tile_expert, n_valid, x_sorted, wgu, bgu, wd, bd)


def _final_kernel(h1_ref, *refs, n_parts, steps_part):
    y_refs = refs[:n_parts * TOP_K]
    tw_ref, fg_ref, o_ref = refs[n_parts * TOP_K:]
    i = pl.program_id(0)
    tw = tw_ref[...]
    h1 = h1_ref[...]
    for part in range(n_parts):
        @pl.when((i >= part * steps_part) & (i < (part + 1) * steps_part))
        def _(part=part):
            h = h1
            for k in range(TOP_K):
                h = h + tw[:, k:k + 1] * y_refs[part * TOP_K + k][...].astype(F32)
            o_ref[...] = _rms_rows(h, fg_ref[...])


def _final(h1, ys_parts, tw, fg, *, tm):
    n = h1.shape[0]
    n_parts = len(ys_parts)
    steps_part = n // tm // n_parts
    row = lambda w: pl.BlockSpec((tm, w), lambda i: (i, 0))
    part_spec = lambda p: pl.BlockSpec(
        (tm, D_MODEL), lambda i: (jnp.clip(i - p * steps_part, 0, steps_part - 1), 0))
    return pl.pallas_call(
        functools.partial(_final_kernel, n_parts=n_parts, steps_part=steps_part),
        out_shape=jax.ShapeDtypeStruct((n, D_MODEL), F32),
        grid=(n // tm,),
        in_specs=[row(D_MODEL)] + [part_spec(p) for p in range(n_parts) for _ in range(TOP_K)]
                 + [row(LANES), pl.BlockSpec((1, D_MODEL), lambda i: (0, 0))],
        out_specs=row(D_MODEL),
        compiler_params=_cparams(("arbitrary",)),
        name="final",
    )(h1, *[y for ys in ys_parts for y in ys], tw, fg)


def _split_w_in(w_in):
    offs = np.cumsum((0,) + IN_SPLITS)
    names = ("q", "k", "v", "qi", "ki", "wi", "z", "xs", "b", "c", "dt", "gate")
    return {nm: w_in[:, offs[i]:offs[i + 1]] for i, nm in enumerate(names)}


def _expert_dispatch(top_i, rank, hist, n_rows_pad):
    n = top_i.shape[0]
    pairs = n * TOP_K
    counts = jnp.sum(hist, axis=0)
    grp_start = jnp.cumsum(counts) - counts
    tiles_per_e = (counts + MOE_TM - 1) // MOE_TM
    tile_end = jnp.cumsum(tiles_per_e)
    row_start = (tile_end - tiles_per_e) * MOE_TM
    offset = row_start - grp_start
    base = row_start[None, :] + jnp.cumsum(hist, axis=0) - hist
    base_tok = jnp.repeat(base, n // hist.shape[0], axis=0)
    hit = top_i[:, :, None] == jnp.arange(N_EXPERTS, dtype=I32)[None, None, :]
    dest = jnp.sum(jnp.where(hit, base_tok[:, None, :], 0), axis=2) + rank
    token_of_pair = jnp.broadcast_to(jnp.arange(n, dtype=I32)[:, None], (n, TOP_K))
    _, order_tok = lax.sort_key_val(dest.reshape(-1), token_of_pair.reshape(-1))
    n_tiles = n_rows_pad // MOE_TM
    n_valid = tile_end[-1].astype(I32)
    tile_ids = jnp.minimum(jnp.arange(n_tiles, dtype=I32), n_valid - 1)
    tile_expert = jnp.searchsorted(tile_end, tile_ids, side="right",
                                   method="compare_all").astype(I32)
    tile_expert = jnp.minimum(tile_expert, N_EXPERTS - 1)
    rows = jnp.arange(n_rows_pad, dtype=I32)
    e_row = jnp.repeat(tile_expert, MOE_TM)
    s_row = rows - offset[e_row]
    live = (s_row >= grp_start[e_row]) & (s_row < (grp_start + counts)[e_row])
    src_token = jnp.where(live, order_tok[jnp.clip(s_row, 0, pairs - 1)], rows % n)
    return src_token, dest, tile_expert, n_valid.reshape(1)


def kernel(x, meta_tokens, norm1_g, w_in, conv_w, conv_b, dt_bias, a_log, d_skip, ssm_norm_g,
           w_o_attn, w_o_ssm, w_out, norm2_g, w_router, b_router, w_gate_up, b_gate_up,
           w_down, b_down, final_g):
    nb, seq, d = x.shape
    n = nb * seq
    assert d == D_MODEL and seq % TQ == 0 and w_in.shape[0] == 1
    xf = x.reshape(n, d)
    meta_pad = jnp.concatenate([jnp.zeros((META_PAD, d), x.dtype), meta_tokens.astype(x.dtype)], 0)
    g1 = norm1_g[0].reshape(1, d)

    w = _split_w_in(w_in[0])
    w_nat = jnp.concatenate([w["k"], w["z"], w["xs"], w["b"], w["c"], w["gate"]], 1).astype(BF16)
    w_nat = jnp.transpose(w_nat.reshape(d, NAT_W // NAT_TILE, NAT_TILE), (1, 0, 2))
    w_qvt = jnp.concatenate([w["q"], w["v"]], 1).T.astype(BF16)
    zpad = lambda c: jnp.zeros((d, c), F32)
    w_sn = jnp.concatenate([w["ki"], zpad(SN_DT - IDX_DIM), w["dt"],
                            zpad(SN_W - SN_DT - SSM_HEADS)], 1).astype(BF16)
    w_st = jnp.concatenate([w["qi"], w["wi"], zpad(ST_W - ST_WI - IDX_HEADS)], 1).T.astype(BF16)

    tabs = _rope_tables(N_META + jnp.arange(seq, dtype=I32))
    tabs_t = tuple(t.T for t in tabs)
    pos_meta = jnp.maximum(jnp.arange(CHUNK, dtype=I32) - META_PAD, 0)
    tabs_m = _rope_tables(pos_meta)
    tabs_mt = tuple(t.T for t in tabs_m)

    tm_nat = 1024 if seq % 1024 == 0 else TQ
    w_t = jnp.concatenate([w_qvt, w_st], axis=0)
    q_slabs = ATTN_WIDTH // LANES
    tr_rope = tuple(range(q_slabs)) + tuple(
        2 * q_slabs + s for s in range(IDX_HEADS * IDX_DIM // LANES))
    nat, sm_n = _proj_nat(xf, g1, w_nat, w_sn, tabs, tm=tm_nat, rope_tiles=1,
                          seq_tiles=seq // tm_nat, name="proj_nat")
    qvt, sm_t = _proj_tr(xf, g1, w_t, tabs_t, tm=TQ, nb=nb, width_main=2 * ATTN_WIDTH,
                         rope_slabs=tr_rope, name="proj_t")
    nat_m, sm_n_m = _proj_nat(meta_pad, g1, w_nat, w_sn, tabs_m, tm=CHUNK, rope_tiles=1,
                              seq_tiles=1, name="proj_nat_meta")
    qvt_m, _ = _proj_tr(meta_pad, g1, w_t, tabs_mt, tm=CHUNK, nb=1, width_main=2 * ATTN_WIDTH,
                        rope_slabs=tr_rope, name="proj_t_meta")

    vt_meta = qvt_m[0, 0, ATTN_WIDTH:, CHUNK - ATTN_META_ROWS:]
    attn = _attention(qvt, nat, sm_t, sm_n, nat_m, vt_meta, nb=nb, seq=seq)

    dt_raw = sm_n[:, SN_DT:SN_DT + SSM_HEADS].reshape(nb, seq // CHUNK, CHUNK, SSM_GROUPS,
                                                      GROUP_HEADS)
    dtt = jnp.transpose(dt_raw, (3, 0, 1, 4, 2))
    dt_raw_m = sm_n_m[:, SN_DT:SN_DT + SSM_HEADS].reshape(CHUNK, SSM_GROUPS, GROUP_HEADS)
    dtt_m = jnp.transpose(dt_raw_m, (1, 2, 0))
    ssm = _ssd(nat, nat_m, dtt, dtt_m, conv_w[0], conv_b[0], dt_bias[0], a_log[0], d_skip[0],
               ssm_norm_g[0], nb=nb, seq=seq)

    wr = jnp.concatenate([w_router[0], jnp.zeros((d, LANES - N_EXPERTS), F32)], 1)
    br = jnp.concatenate([b_router[0], jnp.full((LANES - N_EXPERTS,), -1e30, F32)]).reshape(1, LANES)
    wr_hi = wr.astype(BF16)
    wr_lo = (wr - wr_hi.astype(F32)).astype(BF16)
    tm_merge = 512 if n % 512 == 0 else TQ
    h1, hn2, ti, tw, hist = _merge(attn, ssm, nat, xf, w_o_attn[0].astype(BF16),
                                   w_o_ssm[0].astype(BF16), w_out[0].astype(BF16),
                                   norm2_g[0].reshape(1, d), wr_hi, wr_lo, br, tm=tm_merge)

    tm_final = 512 if n % 512 == 0 else TQ
    parts = MOE_PARTS if n % (MOE_PARTS * max(tm_merge, tm_final)) == 0 else 1
    n_part = n // parts
    hist_e = hist[::8, :N_EXPERTS]
    tiles_part = hist_e.shape[0] // parts
    n_tiles = n_part * TOP_K // MOE_TM + N_EXPERTS
    bgu, bdn = b_gate_up[0].reshape(N_EXPERTS, 1, -1), b_down[0].reshape(N_EXPERTS, 1, -1)
    ys_parts = []
    for p in range(parts):
        rows = slice(p * n_part, (p + 1) * n_part)
        src_token, pos, tile_expert, n_valid = _expert_dispatch(
            ti[rows, :TOP_K], ti[rows, TOP_K:2 * TOP_K],
            hist_e[p * tiles_part:(p + 1) * tiles_part], n_tiles * MOE_TM)
        x_sorted = hn2.at[src_token + p * n_part].get(mode="promise_in_bounds")
        y_sorted = _moe(tile_expert, n_valid, x_sorted, w_gate_up[0], bgu, w_down[0], bdn)
        ys_parts.append([y_sorted.at[pos[:, k]].get(mode="promise_in_bounds")
                         for k in range(TOP_K)])

    out = _final(h1, ys_parts, tw, final_g.reshape(1, d), tm=tm_final)
    return out.reshape(nb, seq, d)
```
